```python
import math, functools
import jax, jax.numpy as jnp
from jax import lax
import numpy as np

D_MODEL = 2048
BATCH = 4
SEQ = 2048
DEPTH = 4
DEC_BATCH = 8
DEC_SEQ = 4
PAST_LEN = 16384
PAGE_SIZE = 128

N_A_LAYERS = DEPTH // 2
N_B_LAYERS = DEPTH - N_A_LAYERS
D_FF = 4 * D_MODEL
NORM_EPS = 1e-6
L2_EPS = 1e-6

GDN_HEAD_DIM = 128
GDN_QK_HEADS = D_MODEL // GDN_HEAD_DIM
GDN_V_HEADS = 2 * GDN_QK_HEADS
GDN_KEY_DIM = GDN_QK_HEADS * GDN_HEAD_DIM
GDN_VAL_DIM = GDN_V_HEADS * GDN_HEAD_DIM
GDN_CONV = 4
GDN_CHUNK = 64
GDN_CONV_DIM = 2 * GDN_KEY_DIM + GDN_VAL_DIM
GDN_IN_DIM = GDN_CONV_DIM + GDN_VAL_DIM + 2 * GDN_V_HEADS

NSA_HEAD_DIM = 128
NSA_HEADS = D_MODEL // NSA_HEAD_DIM
NSA_KV_HEADS = 4
NSA_GROUP = NSA_HEADS // NSA_KV_HEADS
CMP_BLOCK = 32
CMP_STRIDE = 16
CMP_HIDDEN = NSA_HEAD_DIM
SEL_BLOCK = 64
SEL_TOPK = 16
WINDOW = 512
NSA_QBLK = 32
NSA_KV_COLS = 6 * NSA_KV_HEADS * NSA_HEAD_DIM
NSA_IN_DIM = NSA_HEADS * NSA_HEAD_DIM + 3 * NSA_HEADS

REL_BUCKETS = 32
REL_MAX_DIST = 4096

kernel_name = 'yoco_gdn_nsa_hybrid_step'


def rmsnorm(x, w):
    xf = x.astype(jnp.float32)
    y = xf * lax.rsqrt(jnp.mean(xf * xf, axis=-1, keepdims=True) + NORM_EPS)
    return (y * w.astype(jnp.float32)).astype(x.dtype)


def l2norm(x):
    xf = x.astype(jnp.float32)
    return (xf * lax.rsqrt(jnp.sum(xf * xf, axis=-1, keepdims=True) + L2_EPS)).astype(x.dtype)


def sq_relu_mlp(h, w1, w2):
    a = jax.nn.relu(h @ w1)
    return (a * a) @ w2


def rel_bucket(dist):
    n = jnp.maximum(dist, 0)
    max_exact = REL_BUCKETS // 2
    nf = jnp.maximum(n, 1).astype(jnp.float32)
    large = max_exact + (jnp.log(nf / max_exact) / math.log(REL_MAX_DIST / max_exact)
                         * (REL_BUCKETS - max_exact)).astype(jnp.int32)
    return jnp.where(n < max_exact, n, jnp.minimum(large, REL_BUCKETS - 1))


def causal_dwconv(x, buf, w):
    full = jnp.concatenate([buf.astype(x.dtype), x], axis=1)
    y = lax.conv_general_dilated(full, w.astype(x.dtype)[:, None, :], window_strides=(1,), padding='VALID',
                                 dimension_numbers=('NWC', 'WIO', 'NWC'), feature_group_count=full.shape[-1])
    return y, full[:, full.shape[1] - (GDN_CONV - 1):]


def gated_delta_rule(q, k, v, beta, g, s0):
    f32 = jnp.float32
    B, T, H, DK = q.shape
    DV = v.shape[-1]
    C = min(GDN_CHUNK, T)
    n = -(-T // C)
    pad = n * C - T

    def to_chunks(a):
        a = a.astype(f32)
        a = jnp.pad(a, [(0, 0), (0, pad)] + [(0, 0)] * (a.ndim - 2))
        a = a.reshape((B, n, C) + a.shape[2:])
        return jnp.moveaxis(jnp.moveaxis(a, 3, 2), 1, 0)

    qc, kc, vc, bc, gc = (to_chunks(a) for a in (q, k, v, beta, g))
    gcum = jnp.cumsum(gc, axis=-1)
    tril = jnp.tril(jnp.ones((C, C), dtype=bool))
    decay = jnp.exp(jnp.where(tril, gcum[..., :, None] - gcum[..., None, :], -jnp.inf))
    kb = kc * bc[..., None]
    lmat = jnp.tril(jnp.einsum('nbhik,nbhjk->nbhij', kb, kc) * decay, -1)
    amat = lmat + jnp.eye(C, dtype=f32)
    rhs = jnp.concatenate([vc * bc[..., None], kb * jnp.exp(gcum)[..., None]], axis=-1)
    sol = lax.linalg.triangular_solve(amat, rhs, left_side=True, lower=True, unit_diagonal=True)
    u, w = sol[..., :DV], sol[..., DV:]
    attn = jnp.einsum('nbhik,nbhjk->nbhij', qc, kc) * decay
    qg = qc * jnp.exp(gcum)[..., None]
    kd = kc * jnp.exp(gcum[..., -1:] - gcum)[..., None]
    gl = jnp.exp(gcum[..., -1])

    def step(S, xs):
        u_i, w_i, qg_i, attn_i, kd_i, gl_i = xs
        v_new = u_i - jnp.einsum('bhck,bhkv->bhcv', w_i, S)
        o_i = jnp.einsum('bhck,bhkv->bhcv', qg_i, S) + jnp.einsum('bhij,bhjv->bhiv', attn_i, v_new)
        S = S * gl_i[..., None, None] + jnp.einsum('bhck,bhcv->bhkv', kd_i, v_new)
        return S, o_i

    s_fin, o = lax.scan(step, s0.astype(f32), (u, w, qg, attn, kd, gl))
    o = jnp.moveaxis(jnp.moveaxis(o, 0, 1), 2, 3).reshape(B, n * C, H, DV)[:, :T]
    return o, s_fin


def gdn_mixer(h, conv_buf, s0, w_in, conv_w, a_log, dt_bias, out_norm, w_out):
    B, T, _ = h.shape
    f32 = jnp.float32
    proj = h @ w_in
    qkv, z, b, a = jnp.split(proj, [GDN_CONV_DIM, GDN_CONV_DIM + GDN_VAL_DIM,
                                    GDN_CONV_DIM + GDN_VAL_DIM + GDN_V_HEADS], axis=-1)
    qkv, new_buf = causal_dwconv(qkv, conv_buf, conv_w)
    qkv = jax.nn.silu(qkv)
    q, k, v = jnp.split(qkv, [GDN_KEY_DIM, 2 * GDN_KEY_DIM], axis=-1)
    rep = GDN_V_HEADS // GDN_QK_HEADS
    q = jnp.repeat(l2norm(q.reshape(B, T, GDN_QK_HEADS, GDN_HEAD_DIM)), rep, axis=2) * (GDN_HEAD_DIM ** -0.5)
    k = jnp.repeat(l2norm(k.reshape(B, T, GDN_QK_HEADS, GDN_HEAD_DIM)), rep, axis=2)
    v = v.reshape(B, T, GDN_V_HEADS, GDN_HEAD_DIM)
    beta = jax.nn.sigmoid(b.astype(f32))
    g = -jnp.exp(a_log.astype(f32)) * jax.nn.softplus(a.astype(f32) + dt_bias.astype(f32))
    o, s_new = gated_delta_rule(q, k, v, beta, g, s0)
    o = rmsnorm(o, out_norm) * jax.nn.silu(z.reshape(B, T, GDN_V_HEADS, GDN_HEAD_DIM).astype(f32))
    return o.reshape(B, T, GDN_VAL_DIM).astype(h.dtype) @ w_out, new_buf, s_new


def compress(rows, pe, w1, b1, w2, b2):
    B, N = rows.shape[:2]
    nseg = -(-N // CMP_STRIDE)
    rows = jnp.pad(rows, [(0, 0), (0, nseg * CMP_STRIDE - N), (0, 0), (0, 0)])
    seg = rows.reshape(B, nseg, CMP_STRIDE, NSA_KV_HEADS, NSA_HEAD_DIM)
    pe = pe.reshape(2, CMP_STRIDE, 1, NSA_HEAD_DIM).astype(rows.dtype)
    w1 = w1.reshape(2, CMP_STRIDE, NSA_HEAD_DIM, CMP_HIDDEN)
    first = jnp.einsum('bsrhd,rdf->bshf', seg[:, :-1] + pe[0], w1[0])
    second = jnp.einsum('bsrhd,rdf->bshf', seg[:, 1:] + pe[1], w1[1])
    return jax.nn.silu(first + second + b1) @ w2 + b2


def nsa_shared_side(x, past_cmp, past_sel, past_win, q_pos0, kv_norm, nsa_w_kv, k_sel_norm, k_win_norm,
                    k_cmp_norm, cmp_pe, cmp_w1, cmp_b1, cmp_w2, cmp_b2):
    B, T, _ = x.shape
    rows = (rmsnorm(x, kv_norm) @ nsa_w_kv).reshape(B, T, 6, NSA_KV_HEADS, NSA_HEAD_DIM)
    cmp_rows = rows[:, :, 0:2]
    sel_rows = jnp.stack([rmsnorm(rows[:, :, 2], k_sel_norm), rows[:, :, 3]], axis=2)
    win_rows = jnp.stack([rmsnorm(rows[:, :, 4], k_win_norm), rows[:, :, 5]], axis=2)
    if past_cmp is None:
        cmp_seq, sel_seq, win_seq = cmp_rows, sel_rows, win_rows
    else:
        cmp_seq = jnp.concatenate([past_cmp.astype(rows.dtype), cmp_rows], axis=1)
        sel_seq = jnp.concatenate([past_sel.astype(rows.dtype), sel_rows], axis=1)
        win_seq = jnp.concatenate([past_win.astype(rows.dtype), win_rows], axis=1)
    w_pos0 = q_pos0 + T - win_seq.shape[1]
    kc = rmsnorm(compress(cmp_seq[:, :, 0], cmp_pe[0], cmp_w1[0], cmp_b1[0], cmp_w2[0], cmp_b2[0]), k_cmp_norm)
    vc = compress(cmp_seq[:, :, 1], cmp_pe[1], cmp_w1[1], cmp_b1[1], cmp_w2[1], cmp_b2[1])
    attend_kv = (kc, vc, sel_seq[:, :, 0], sel_seq[:, :, 1], win_seq[:, :, 0], win_seq[:, :, 1])
    n_keep = min(WINDOW, win_seq.shape[1])
    return attend_kv, w_pos0, cmp_rows, sel_rows, win_seq[:, win_seq.shape[1] - n_keep:]


def nsa_attend(q, gate, q_pos0, w_pos0, kc, vc, ks, vs, kw, vw, rel_bias):
    f32 = jnp.float32
    B, T = q.shape[:2]
    KV, G, DH = NSA_KV_HEADS, NSA_GROUP, NSA_HEAD_DIM
    QB = min(NSA_QBLK, T)
    nqb = -(-T // QB)
    Tp = nqb * QB
    qg = jnp.pad(q.reshape(B, T, KV, G, DH), [(0, 0), (0, Tp - T), (0, 0), (0, 0), (0, 0)])
    gg = jnp.pad(gate.reshape(B, T, KV, G, 3), [(0, 0), (0, Tp - T), (0, 0), (0, 0), (0, 0)])
    NC = kc.shape[1]
    cmp_start = jnp.arange(NC) * CMP_STRIDE
    cmp_end = cmp_start + (CMP_BLOCK - 1)
    N = ks.shape[1]
    NS = -(-N // SEL_BLOCK)

    def to_blocks(a):
        a = jnp.pad(a, [(0, 0), (0, NS * SEL_BLOCK - N), (0, 0), (0, 0)])
        return a.reshape(B, NS, SEL_BLOCK, KV, DH).transpose(0, 3, 1, 2, 4)

    ksb, vsb = to_blocks(ks), to_blocks(vs)
    blk_start = jnp.arange(NS) * SEL_BLOCK
    cmp2sel = ((cmp_start[:, None] < blk_start[None, :] + SEL_BLOCK)
               & (cmp_start[:, None] + CMP_BLOCK > blk_start[None, :])).astype(f32)
    top_k = min(SEL_TOPK, NS)
    wpad = [(0, 0), (WINDOW, Tp - T), (0, 0), (0, 0)]
    kwp, vwp = jnp.pad(kw, wpad), jnp.pad(vw, wpad)
    WL = WINDOW + QB - 1
    tab = rel_bias.astype(f32).reshape(REL_BUCKETS, KV, G)
    tab_kv = tab.transpose(1, 0, 2)
    gather_blocks = jax.vmap(jax.vmap(lambda blocks, ids: blocks[ids]))
    bias_per_kv = jax.vmap(lambda tb, bk: tb[bk], in_axes=(0, 1), out_axes=1)
    bidx = jnp.arange(NS)

    def masked_softmax(logits, mask):
        return jax.nn.softmax(jnp.where(mask, logits, -1e30), axis=-1) * mask

    def block(i):
        i0 = i * QB
        qb = lax.dynamic_slice_in_dim(qg, i0, QB, axis=1)
        gb = lax.dynamic_slice_in_dim(gg, i0, QB, axis=1)
        t = q_pos0 + i0 + jnp.arange(QB)
        dist_c = t[:, None] - cmp_end[None, :]
        lc = (jnp.einsum('bqhgd,bchd->bqhgc', qb, kc).astype(f32)
              + tab[rel_bucket(dist_c)].transpose(0, 2, 3, 1))
        pc = masked_softmax(lc, (dist_c >= 0)[None, :, None, None, :])
        oc = jnp.einsum('bqhgc,bchd->bqhgd', pc.astype(vc.dtype), vc)
        imp = jnp.einsum('bqhgc,cs->bqhs', pc, cmp2sel)
        cur = (t // SEL_BLOCK)[:, None]
        forced = (bidx[None, :] == 0) | (bidx[None, :] == cur) | (bidx[None, :] == cur - 1)
        valid = blk_start[None, :] <= t[:, None]
        score = jnp.where(forced[None, :, None, :], G + 1.0,
                          jnp.where(valid[None, :, None, :], imp, -1.0))
        sel = lax.top_k(score, top_k)[1].transpose(0, 2, 1, 3)
        kg = gather_blocks(ksb, sel).reshape(B, KV, QB, top_k * SEL_BLOCK, DH)
        vg = gather_blocks(vsb, sel).reshape(B, KV, QB, top_k * SEL_BLOCK, DH)
        pos_s = (sel[..., None] * SEL_BLOCK + jnp.arange(SEL_BLOCK)).reshape(B, KV, QB, top_k * SEL_BLOCK)
        dist_s = t[None, None, :, None] - pos_s
        bias_s = bias_per_kv(tab_kv, rel_bucket(dist_s))
        ls = jnp.einsum('bqhgd,bhqnd->bhqgn', qb, kg).astype(f32) + jnp.swapaxes(bias_s, -1, -2)
        ps = masked_softmax(ls, (dist_s >= 0)[:, :, :, None, :])
        osel = jnp.einsum('bhqgn,bhqnd->bqhgd', ps.astype(vg.dtype), vg)
        s0w = i0 + (q_pos0 - w_pos0) + 1
        kwb = lax.dynamic_slice_in_dim(kwp, s0w, WL, axis=1)
        vwb = lax.dynamic_slice_in_dim(vwp, s0w, WL, axis=1)
        pos_w = (w_pos0 - WINDOW) + s0w + jnp.arange(WL)
        dist_w = t[:, None] - pos_w[None, :]
        mask_w = (dist_w >= 0) & (dist_w < WINDOW) & (pos_w[None, :] >= w_pos0)
        lw = (jnp.einsum('bqhgd,bwhd->bqhgw', qb, kwb).astype(f32)
              + tab[rel_bucket(dist_w)].transpose(0, 2, 3, 1))
        pw = masked_softmax(lw, mask_w[None, :, None, None, :])
        ow = jnp.einsum('bqhgw,bwhd->bqhgd', pw.astype(vwb.dtype), vwb)
        return oc * gb[..., 0:1] + osel * gb[..., 1:2] + ow * gb[..., 2:3]

    out = lax.map(block, jnp.arange(nqb))
    return jnp.moveaxis(out, 0, 1).reshape(B, Tp, KV * G * DH)[:, :T]


def nsa_mixer(h, attend_kv, q_pos0, w_pos0, w_in, q_norm, w_out, rel_bias):
    B, T, _ = h.shape
    proj = h @ w_in
    q = proj[..., :NSA_HEADS * NSA_HEAD_DIM].reshape(B, T, NSA_HEADS, NSA_HEAD_DIM)
    q = rmsnorm(q, q_norm) * (NSA_HEAD_DIM ** -0.5)
    gate = jax.nn.sigmoid(proj[..., NSA_HEADS * NSA_HEAD_DIM:].reshape(B, T, NSA_HEADS, 3))
    o = nsa_attend(q, gate, q_pos0, w_pos0, *attend_kv, rel_bias)
    return o.astype(h.dtype) @ w_out


def forward_group(x, conv0, ssm0, past_cmp, past_sel, past_win, q_pos0,
                  mix_norm, mlp_norm, mlp_w1, mlp_w2,
                  gdn_w_in, gdn_conv_w, gdn_a_log, gdn_dt_bias, gdn_out_norm, gdn_w_out,
                  kv_norm, nsa_w_kv, k_sel_norm, k_win_norm, k_cmp_norm,
                  cmp_pe, cmp_w1, cmp_b1, cmp_w2, cmp_b2,
                  nsa_w_in, nsa_q_norm, nsa_w_out, rel_bias):
    conv_out, ssm_out = [], []
    for l in range(DEPTH):
        h = rmsnorm(x, mix_norm[l])
        if l < N_A_LAYERS:
            m, cb, st = gdn_mixer(h, conv0[l], ssm0[l], gdn_w_in[l], gdn_conv_w[l], gdn_a_log[l],
                                  gdn_dt_bias[l], gdn_out_norm[l], gdn_w_out[l])
            conv_out.append(cb)
            ssm_out.append(st)
        else:
            j = l - N_A_LAYERS
            m = nsa_mixer(h, attend_kv, q_pos0, w_pos0, nsa_w_in[j], nsa_q_norm[j], nsa_w_out[j], rel_bias)
        x = x + m
        x = x + sq_relu_mlp(rmsnorm(x, mlp_norm[l]), mlp_w1[l], mlp_w2[l])
        if l == N_A_LAYERS - 1:
            attend_kv, w_pos0, cmp_rows, sel_rows, win_state = nsa_shared_side(
                x, past_cmp, past_sel, past_win, q_pos0, kv_norm, nsa_w_kv, k_sel_norm, k_win_norm,
                k_cmp_norm, cmp_pe, cmp_w1, cmp_b1, cmp_w2, cmp_b2)
    return x, jnp.stack(conv_out), jnp.stack(ssm_out), cmp_rows, sel_rows, win_state


def setup_inputs(seed: int = 0) -> dict:
    key = jax.random.key(seed)
    keys = iter(jax.random.split(key, 40))
    f32 = jnp.float32

    def nrm(shape, scale):
        return jax.random.normal(next(keys), shape, f32) * scale

    def gain(shape):
        return 1.0 + 0.01 * jax.random.normal(next(keys), shape, f32)

    n_pages = PAST_LEN // PAGE_SIZE
    n_used = DEC_BATCH * n_pages
    n_pool = n_used + max(1, n_used // 4)
    win_buf = min(WINDOW, PAST_LEN)
    kvrow = (2, NSA_KV_HEADS, NSA_HEAD_DIM)
    x_prompt = nrm((BATCH, SEQ, D_MODEL), 1.0)
    x_sample = nrm((DEC_BATCH, DEC_SEQ, D_MODEL), 1.0)
    state_conv = nrm((N_A_LAYERS, DEC_BATCH, GDN_CONV - 1, GDN_CONV_DIM), 1.0)
    state_ssm = nrm((N_A_LAYERS, DEC_BATCH, GDN_V_HEADS, GDN_HEAD_DIM, GDN_HEAD_DIM), 0.05)
    cache_cmp = nrm((n_pool, PAGE_SIZE) + kvrow, 1.0)
    cache_sel = nrm((n_pool, PAGE_SIZE) + kvrow, 1.0)
    cache_win = nrm((DEC_BATCH, win_buf) + kvrow, 1.0)
    page_table = jax.random.permutation(next(keys), n_pool)[:n_used].reshape(DEC_BATCH, n_pages).astype(jnp.int32)
    dt = jnp.exp(jax.random.uniform(next(keys), (N_A_LAYERS, GDN_V_HEADS), f32, math.log(1e-3), math.log(1e-1)))
    gdn_dt_bias = dt + jnp.log(-jnp.expm1(-dt))
    gdn_a_log = jnp.log(jax.random.uniform(next(keys), (N_A_LAYERS, GDN_V_HEADS), f32, 1.0, 16.0))
    return {
        'x_prompt': x_prompt,
        'x_sample': x_sample,
        'state_conv': state_conv,
        'state_ssm': state_ssm,
        'cache_cmp': cache_cmp,
        'cache_sel': cache_sel,
        'cache_win': cache_win,
        'page_table': page_table,
        'mix_norm': gain((DEPTH, D_MODEL)),
        'mlp_norm': gain((DEPTH, D_MODEL)),
        'mlp_w1': nrm((DEPTH, D_MODEL, D_FF), D_MODEL ** -0.5),
        'mlp_w2': nrm((DEPTH, D_FF, D_MODEL), D_FF ** -0.5),
        'gdn_w_in': nrm((N_A_LAYERS, D_MODEL, GDN_IN_DIM), D_MODEL ** -0.5),
        'gdn_conv_w': nrm((N_A_LAYERS, GDN_CONV, GDN_CONV_DIM), GDN_CONV ** -0.5),
        'gdn_a_log': gdn_a_log,
        'gdn_dt_bias': gdn_dt_bias,
        'gdn_out_norm': gain((N_A_LAYERS, GDN_HEAD_DIM)),
        'gdn_w_out': nrm((N_A_LAYERS, GDN_VAL_DIM, D_MODEL), GDN_VAL_DIM ** -0.5),
        'kv_norm': gain((D_MODEL,)),
        'nsa_w_kv': nrm((D_MODEL, NSA_KV_COLS), D_MODEL ** -0.5),
        'k_sel_norm': gain((NSA_HEAD_DIM,)),
        'k_win_norm': gain((NSA_HEAD_DIM,)),
        'k_cmp_norm': gain((NSA_HEAD_DIM,)),
        'cmp_pe': nrm((2, CMP_BLOCK, NSA_HEAD_DIM), 0.02),
        'cmp_w1': nrm((2, CMP_BLOCK, NSA_HEAD_DIM, CMP_HIDDEN), (CMP_BLOCK * NSA_HEAD_DIM) ** -0.5),
        'cmp_b1': nrm((2, CMP_HIDDEN), 0.01),
        'cmp_w2': nrm((2, CMP_HIDDEN, NSA_HEAD_DIM), CMP_HIDDEN ** -0.5),
        'cmp_b2': nrm((2, NSA_HEAD_DIM), 0.01),
        'nsa_w_in': nrm((N_B_LAYERS, D_MODEL, NSA_IN_DIM), D_MODEL ** -0.5),
        'nsa_q_norm': gain((N_B_LAYERS, NSA_HEAD_DIM)),
        'nsa_w_out': nrm((N_B_LAYERS, NSA_HEADS * NSA_HEAD_DIM, D_MODEL), (NSA_HEADS * NSA_HEAD_DIM) ** -0.5),
        'rel_bias': nrm((REL_BUCKETS, NSA_HEADS), 0.1),
    }


def reference(x_prompt, x_sample, state_conv, state_ssm, cache_cmp, cache_sel, cache_win, page_table,
              mix_norm, mlp_norm, mlp_w1, mlp_w2,
              gdn_w_in, gdn_conv_w, gdn_a_log, gdn_dt_bias, gdn_out_norm, gdn_w_out,
              kv_norm, nsa_w_kv, k_sel_norm, k_win_norm, k_cmp_norm,
              cmp_pe, cmp_w1, cmp_b1, cmp_w2, cmp_b2,
              nsa_w_in, nsa_q_norm, nsa_w_out, rel_bias):
    run = functools.partial(
        forward_group,
        mix_norm=mix_norm, mlp_norm=mlp_norm, mlp_w1=mlp_w1, mlp_w2=mlp_w2,
        gdn_w_in=gdn_w_in, gdn_conv_w=gdn_conv_w, gdn_a_log=gdn_a_log, gdn_dt_bias=gdn_dt_bias,
        gdn_out_norm=gdn_out_norm, gdn_w_out=gdn_w_out,
        kv_norm=kv_norm, nsa_w_kv=nsa_w_kv, k_sel_norm=k_sel_norm, k_win_norm=k_win_norm, k_cmp_norm=k_cmp_norm,
        cmp_pe=cmp_pe, cmp_w1=cmp_w1, cmp_b1=cmp_b1, cmp_w2=cmp_w2, cmp_b2=cmp_b2,
        nsa_w_in=nsa_w_in, nsa_q_norm=nsa_q_norm, nsa_w_out=nsa_w_out, rel_bias=rel_bias)
    bp = x_prompt.shape[0]
    conv0 = jnp.zeros((N_A_LAYERS, bp, GDN_CONV - 1, GDN_CONV_DIM), x_prompt.dtype)
    ssm0 = jnp.zeros((N_A_LAYERS, bp, GDN_V_HEADS, GDN_HEAD_DIM, GDN_HEAD_DIM), jnp.float32)
    y_prompt, conv_p, ssm_p, cmp_p, sel_p, win_p = run(x_prompt, conv0, ssm0, None, None, None, 0)
    db = x_sample.shape[0]
    n_past = page_table.shape[1] * PAGE_SIZE
    past_cmp = cache_cmp[page_table].reshape(db, n_past, 2, NSA_KV_HEADS, NSA_HEAD_DIM)
    past_sel = cache_sel[page_table].reshape(db, n_past, 2, NSA_KV_HEADS, NSA_HEAD_DIM)
    y_sample, conv_s, ssm_s, cmp_s, sel_s, win_s = run(x_sample, state_conv, state_ssm, past_cmp, past_sel,
                                                       cache_win, n_past)
    return (y_prompt, y_sample, conv_p, ssm_p, cmp_p, sel_p, win_p, conv_s, ssm_s, cmp_s, sel_s, win_s)
```

```python
import functools
import math

import jax
import jax.numpy as jnp
from jax import lax
from jax.experimental import pallas as pl
from jax.experimental.pallas import tpu as pltpu

F32 = jnp.float32
BF16 = jnp.bfloat16

D_MODEL = 2048
D_FF = 4 * D_MODEL
NORM_EPS = 1e-6
L2_EPS = 1e-6
PAGE = 128

HD = 128
GDN_QK_HEADS = 16
GDN_V_HEADS = 32
GDN_KEY_DIM = GDN_QK_HEADS * HD
GDN_VAL_DIM = GDN_V_HEADS * HD
GDN_CONV = 4
GDN_CHUNK = 64
GDN_CONV_DIM = 2 * GDN_KEY_DIM + GDN_VAL_DIM

NSA_HEADS = 16
NSA_KV = 4
NSA_G = NSA_HEADS // NSA_KV
CMP_BLOCK = 32
CMP_STRIDE = 16
SEL_BLOCK = 64
SEL_TOPK = 16
WINDOW = 512
REL_BUCKETS = 32
REL_MAX_DIST = 4096
NEG = -1e30

TK = 128
N_DELTA = REL_MAX_DIST // TK + 2

VMEM_LIMIT = 56 * 1024 * 1024


def _cparams(sem):
    return pltpu.CompilerParams(dimension_semantics=sem, vmem_limit_bytes=VMEM_LIMIT)


def _sigmoid(x):
    return 1.0 / (1.0 + jnp.exp(-x))


def _softplus(x):
    return jnp.maximum(x, 0.0) + jnp.log(1.0 + jnp.exp(-jnp.abs(x)))


def _dot(a, b):
    return jnp.dot(a.astype(BF16), b.astype(BF16), preferred_element_type=F32)


def _dot_nt(a, b):
    return lax.dot_general(a.astype(BF16), b.astype(BF16), (((1,), (1,)), ((), ())),
                           preferred_element_type=F32)


def _dot_tn(a, b):
    return lax.dot_general(a.astype(BF16), b.astype(BF16), (((0,), (0,)), ((), ())),
                           preferred_element_type=F32)


def _headnorm(acc, gw):
    parts = []
    for g in range(acc.shape[1] // HD):
        a = acc[:, g * HD:(g + 1) * HD]
        parts.append(a * lax.rsqrt(jnp.mean(a * a, axis=-1, keepdims=True) + NORM_EPS))
    return jnp.concatenate(parts, axis=1) * gw


def _nmm_kernel(x_ref, nw_ref, w_ref, aux_ref, o_ref, h_ref, *, mode, norm_tiles, scale, seq, t_valid):
    i = pl.program_id(0)
    j = pl.program_id(1)

    @pl.when(j == 0)
    def _():
        x = x_ref[...]
        h = x * lax.rsqrt(jnp.mean(x * x, axis=-1, keepdims=True) + NORM_EPS) * nw_ref[...]
        h_ref[...] = h.astype(BF16)

    acc = jnp.dot(h_ref[...], w_ref[...], preferred_element_type=F32)
    if mode == "plain":
        o_ref[...] = acc.astype(o_ref.dtype)
    elif mode == "headnorm":
        if norm_tiles is None:
            o_ref[...] = (_headnorm(acc, aux_ref[0]) * scale).astype(o_ref.dtype)
        else:
            is_n = functools.reduce(jnp.logical_or, [j == t for t in norm_tiles])

            @pl.when(is_n)
            def _():
                o_ref[...] = (_headnorm(acc, aux_ref[0]) * scale).astype(o_ref.dtype)

            @pl.when(jnp.logical_not(is_n))
            def _():
                o_ref[...] = acc.astype(o_ref.dtype)
    elif mode == "sigmoid":
        o_ref[...] = _sigmoid(acc)
    elif mode == "gdn_gate":
        tm = acc.shape[0]
        aux = aux_ref[0]
        lane = lax.broadcasted_iota(jnp.int32, acc.shape, 1)
        row = lax.broadcasted_iota(jnp.int32, acc.shape, 0) + i * tm
        live = (row % seq) < t_valid
        beta = jnp.where(live, _sigmoid(acc), 0.0)
        g = jnp.where(live, -jnp.exp(aux[0:1, :]) * _softplus(acc + aux[1:2, :]), 0.0)
        g = jnp.where((lane >= GDN_V_HEADS) & (lane < 2 * GDN_V_HEADS), g, 0.0)
        r = lax.broadcasted_iota(jnp.int32, (tm, tm), 0)
        c = lax.broadcasted_iota(jnp.int32, (tm, tm), 1)
        tri = ((r // GDN_CHUNK) == (c // GDN_CHUNK)) & (c <= r)
        gcum = jnp.dot(jnp.where(tri, 1.0, 0.0), g, preferred_element_type=F32,
                       precision=lax.Precision.HIGHEST)
        o_ref[...] = jnp.where(lane < GDN_V_HEADS, beta, gcum)
    else:
        raise ValueError(mode)


def _nmm(x, nw, w, *, tn, out_dtype=F32, mode="plain", aux=None, norm_tiles=None, scale=1.0,
         seq=1, t_valid=1, tm=512):
    M, K = x.shape
    N = w.shape[1]
    tm = min(tm, M)
    assert M % tm == 0 and N % tn == 0
    if aux is None:
        aux = jnp.zeros((N // tn, 1, tn), F32)
    kern = functools.partial(_nmm_kernel, mode=mode, norm_tiles=norm_tiles, scale=scale, seq=seq,
                             t_valid=t_valid)
    return pl.pallas_call(
        kern,
        grid=(M // tm, N // tn),
        in_specs=[
            pl.BlockSpec((tm, K), lambda i, j: (i, 0)),
            pl.BlockSpec((1, K), lambda i, j: (0, 0)),
            pl.BlockSpec((K, tn), lambda i, j: (0, j)),
            pl.BlockSpec((1,) + aux.shape[1:], lambda i, j: (j, 0, 0)),
        ],
        out_specs=pl.BlockSpec((tm, tn), lambda i, j: (i, j)),
        out_shape=jax.ShapeDtypeStruct((M, N), out_dtype),
        scratch_shapes=[pltpu.VMEM((tm, K), BF16)],
        compiler_params=_cparams(("parallel", "arbitrary")),
    )(x, nw.reshape(1, K), w, aux)


def _mmres_kernel(x_ref, w_ref, r_ref, o_ref):
    o_ref[...] = r_ref[...] + jnp.dot(x_ref[...], w_ref[...], preferred_element_type=F32)


def _mm_res(x, w, res, *, tm=512, tn=512):
    M, K = x.shape
    N = w.shape[1]
    tm = min(tm, M)
    return pl.pallas_call(
        _mmres_kernel,
        grid=(M // tm, N // tn),
        in_specs=[
            pl.BlockSpec((tm, K), lambda i, j: (i, 0)),
            pl.BlockSpec((K, tn), lambda i, j: (0, j)),
            pl.BlockSpec((tm, tn), lambda i, j: (i, j)),
        ],
        out_specs=pl.BlockSpec((tm, tn), lambda i, j: (i, j)),
        out_shape=jax.ShapeDtypeStruct((M, N), F32),
        compiler_params=_cparams(("parallel", "arbitrary")),
    )(x, w, res)


def _mlp_kernel(x_ref, nw_ref, w1_ref, w2_ref, o_ref, h_ref, acc_ref):
    f = pl.program_id(1)

    @pl.when(f == 0)
    def _():
        x = x_ref[...]
        h = x * lax.rsqrt(jnp.mean(x * x, axis=-1, keepdims=True) + NORM_EPS) * nw_ref[...]
        h_ref[...] = h.astype(BF16)
        acc_ref[...] = x

    a = jnp.maximum(jnp.dot(h_ref[...], w1_ref[...], preferred_element_type=F32), 0.0)
    acc_ref[...] += jnp.dot((a * a).astype(BF16), w2_ref[...], preferred_element_type=F32)

    @pl.when(f == pl.num_programs(1) - 1)
    def _():
        o_ref[...] = acc_ref[...]


def _mlp(x, nw, w1, w2, *, tm=512, tf=512):
    M, D = x.shape
    Fdim = w1.shape[1]
    tm = min(tm, M)
    return pl.pallas_call(
        _mlp_kernel,
        grid=(M // tm, Fdim // tf),
        in_specs=[
            pl.BlockSpec((tm, D), lambda i, f: (i, 0)),
            pl.BlockSpec((1, D), lambda i, f: (0, 0)),
            pl.BlockSpec((D, tf), lambda i, f: (0, f)),
            pl.BlockSpec((tf, D), lambda i, f: (f, 0)),
        ],
        out_specs=pl.BlockSpec((tm, D), lambda i, f: (i, 0)),
        out_shape=jax.ShapeDtypeStruct((M, D), F32),
        scratch_shapes=[pltpu.VMEM((tm, D), BF16), pltpu.VMEM((tm, D), F32)],
        compiler_params=_cparams(("parallel", "arbitrary")),
    )(x, nw.reshape(1, D), w1, w2)


def _unit_lower_inverse(a, r, c):
    eye = jnp.where(r == c, 1.0, 0.0)
    d0 = jnp.where((r // 8) == (c // 8), a, 0.0)
    d2 = _dot(d0, d0)
    d4 = _dot(d2, d2)
    x = _dot(eye - d0, eye + d2)
    x = _dot(x, eye + d4)
    s = 8
    while s < GDN_CHUNK:
        b = jnp.where(((r // (2 * s)) == (c // (2 * s))) & ((r // s) != (c // s)), a, 0.0)
        x = x - _dot(x, _dot(b, x))
        s *= 2
    return x


def _gdn_kernel(q_ref, k_ref, v_ref, z_ref, bg_ref, wq_ref, wk_ref, wv_ref, cq_ref, ck_ref, cv_ref,
                s0_ref, onw_ref, o_ref, sout_ref, st_ref, bq_ref, bk_ref, bv_ref):
    C = GDN_CHUNK
    R = 2 * C
    j = pl.program_id(1)
    ch = pl.program_id(2)

    @pl.when(ch == 0)
    def _():
        st_ref[0:HD, :] = s0_ref[0, 0]
        st_ref[HD:2 * HD, :] = s0_ref[0, 1]
        bq_ref[5:8, :] = cq_ref[0]
        bk_ref[5:8, :] = ck_ref[0]
        bv_ref[5:8, :] = cv_ref[0]

    def conv_silu(x_ref, buf_ref, w_ref):
        buf_ref[8:8 + C, :] = x_ref[...]
        w = w_ref[...]
        y = w[0:1, :] * buf_ref[5:5 + C, :]
        for t in range(1, GDN_CONV):
            y = y + w[t:t + 1, :] * buf_ref[5 + t:5 + t + C, :]
        buf_ref[5:8, :] = buf_ref[5 + C:8 + C, :]
        return y * _sigmoid(y)

    qc = conv_silu(q_ref, bq_ref, wq_ref)
    kc = conv_silu(k_ref, bk_ref, wk_ref)
    vc = conv_silu(v_ref, bv_ref, wv_ref)
    qn = qc * lax.rsqrt(jnp.sum(qc * qc, axis=-1, keepdims=True) + L2_EPS) * (HD ** -0.5)
    kn = kc * lax.rsqrt(jnp.sum(kc * kc, axis=-1, keepdims=True) + L2_EPS)

    bg = bg_ref[...]
    lane = lax.broadcasted_iota(jnp.int32, bg.shape, 1)

    def col(idx):
        return jnp.sum(jnp.where(lane == idx, bg, 0.0), axis=-1, keepdims=True)

    beta2 = jnp.concatenate([col(2 * j), col(2 * j + 1)], axis=0)
    gca = col(GDN_V_HEADS + 2 * j)
    gcb = col(GDN_V_HEADS + 2 * j + 1)
    gc2 = jnp.concatenate([gca, gcb], axis=0)
    rcol = lax.broadcasted_iota(jnp.int32, (R, 1), 0)
    top = rcol < C
    gl2 = jnp.where(top, gca[C - 1:C, :], gcb[C - 1:C, :])

    r = lax.broadcasted_iota(jnp.int32, (R, R), 0)
    c = lax.broadcasted_iota(jnp.int32, (R, R), 1)
    same = (r // C) == (c // C)
    low = same & (c <= r)
    colm = jnp.broadcast_to(gc2, (R, R))
    rowm = colm.T
    dec = jnp.where(low, jnp.exp(jnp.where(low, colm - rowm, 0.0)), 0.0)

    k2 = jnp.concatenate([kn, kn], axis=0)
    q2 = jnp.concatenate([qn, qn], axis=0)
    v2 = jnp.concatenate([vc[:, :HD], vc[:, HD:]], axis=0)
    kk = _dot_nt(k2, k2)
    qk = _dot_nt(q2, k2)
    amat = jnp.where(same & (c < r), kk * beta2 * dec, 0.0)
    attn = qk * dec
    tinv = _unit_lower_inverse(amat, r, c)

    e2 = jnp.exp(gc2)
    kb2 = k2 * beta2
    sol = _dot(tinv, jnp.concatenate([v2 * beta2, kb2 * e2], axis=1))
    u2 = sol[:, :HD]
    w2 = sol[:, HD:]
    qg2 = q2 * e2

    def blocked(a):
        return jnp.concatenate([jnp.where(top, a, 0.0), jnp.where(top, 0.0, a)], axis=1)

    st = st_ref[...]
    ws = _dot(jnp.concatenate([blocked(w2), blocked(qg2)], axis=0), st)
    vnew = u2 - ws[:R]
    o2 = ws[R:] + _dot(attn, vnew)
    kd2 = k2 * jnp.exp(gl2 - gc2)
    srow = lax.broadcasted_iota(jnp.int32, (2 * HD, 1), 0)
    gls = jnp.exp(jnp.where(srow < HD, gca[C - 1:C, :], gcb[C - 1:C, :]))
    st_new = st * gls + _dot_tn(blocked(kd2), vnew)
    st_ref[...] = st_new

    zb = z_ref[...]
    z2 = jnp.concatenate([zb[:, :HD], zb[:, HD:]], axis=0)
    on = o2 * lax.rsqrt(jnp.mean(o2 * o2, axis=-1, keepdims=True) + NORM_EPS) * onw_ref[...]
    out2 = on * (z2 * _sigmoid(z2))
    o_ref[...] = jnp.concatenate([out2[:C], out2[C:]], axis=1).astype(o_ref.dtype)

    @pl.when(ch == pl.num_programs(2) - 1)
    def _():
        sout_ref[0, 0] = st_new[:HD]
        sout_ref[0, 1] = st_new[HD:]


def _gdn(proj, bg, conv_w, conv0, ssm0, out_norm, *, batch, seq):
    C = GDN_CHUNK
    nch = seq // C
    nqk = GDN_QK_HEADS
    row = lambda b, j, c: b * nch + c
    return pl.pallas_call(
        _gdn_kernel,
        grid=(batch, nqk, nch),
        in_specs=[
            pl.BlockSpec((C, HD), lambda b, j, c: (row(b, j, c), j)),
            pl.BlockSpec((C, HD), lambda b, j, c: (row(b, j, c), nqk + j)),
            pl.BlockSpec((C, 2 * HD), lambda b, j, c: (row(b, j, c), nqk + j)),
            pl.BlockSpec((C, 2 * HD), lambda b, j, c: (row(b, j, c), 2 * nqk + j)),
            pl.BlockSpec((C, HD), lambda b, j, c: (row(b, j, c), 0)),
            pl.BlockSpec((GDN_CONV, HD), lambda b, j, c: (0, j)),
            pl.BlockSpec((GDN_CONV, HD), lambda b, j, c: (0, nqk + j)),
            pl.BlockSpec((GDN_CONV, 2 * HD), lambda b, j, c: (0, nqk + j)),
            pl.BlockSpec((1, GDN_CONV - 1, HD), lambda b, j, c: (b, 0, j)),
            pl.BlockSpec((1, GDN_CONV - 1, HD), lambda b, j, c: (b, 0, nqk + j)),
            pl.BlockSpec((1, GDN_CONV - 1, 2 * HD), lambda b, j, c: (b, 0, nqk + j)),
            pl.BlockSpec((1, 2, HD, HD), lambda b, j, c: (b, j, 0, 0)),
            pl.BlockSpec((1, HD), lambda b, j, c: (0, 0)),
        ],
        out_specs=[
            pl.BlockSpec((C, 2 * HD), lambda b, j, c: (row(b, j, c), j)),
            pl.BlockSpec((1, 2, HD, HD), lambda b, j, c: (b, j, 0, 0)),
        ],
        out_shape=[
            jax.ShapeDtypeStruct((batch * seq, GDN_VAL_DIM), BF16),
            jax.ShapeDtypeStruct((batch, GDN_V_HEADS, HD, HD), F32),
        ],
        scratch_shapes=[
            pltpu.VMEM((2 * HD, HD), F32),
            pltpu.VMEM((8 + C, HD), F32),
            pltpu.VMEM((8 + C, HD), F32),
            pltpu.VMEM((8 + C, 2 * HD), F32),
        ],
        compiler_params=_cparams(("parallel", "parallel", "arbitrary")),
    )(proj, proj, proj, proj, bg, conv_w, conv_w, conv_w, conv0, conv0, conv0, ssm0,
      out_norm.reshape(1, HD))


CMP_PPS = 8


def _cmp1_kernel(pt_ref, *refs):
    page_refs = refs[:CMP_PPS]
    w_ref = refs[CMP_PPS]
    o_ref = refs[CMP_PPS + 1]
    nseg = PAGE // CMP_STRIDE
    pr = lax.broadcasted_iota(jnp.int32, (PAGE, PAGE), 0)
    pc = lax.broadcasted_iota(jnp.int32, (PAGE, PAGE), 1)
    perm = jnp.where(pc == (pr % nseg) * CMP_STRIDE + pr // nseg, 1.0, 0.0).astype(BF16)
    pages = [jnp.dot(perm, p[0].astype(BF16), preferred_element_type=F32) for p in page_refs]
    for cc in range(2):
        acc = jnp.zeros((NSA_KV * CMP_PPS * nseg, 2 * HD), F32)
        for rr in range(CMP_STRIDE):
            lhs = jnp.concatenate(
                [pg[rr * nseg:(rr + 1) * nseg, (cc * NSA_KV + h) * HD:(cc * NSA_KV + h + 1) * HD]
                 for h in range(NSA_KV) for pg in pages], axis=0)
            acc = acc + jnp.dot(lhs.astype(BF16), w_ref[cc, rr], preferred_element_type=F32)
        o_ref[0, cc] = acc


def _cmp_stage1(pages, ptab, w1cat):
    n = ptab.shape[0]
    nst = n // CMP_PPS
    nseg = PAGE // CMP_STRIDE

    def pmap(p):
        return lambda s, pt: (pt[s * CMP_PPS + p], 0, 0)

    grid_spec = pltpu.PrefetchScalarGridSpec(
        num_scalar_prefetch=1,
        grid=(nst,),
        in_specs=[pl.BlockSpec((1, PAGE, 2 * NSA_KV * HD), pmap(p)) for p in range(CMP_PPS)]
        + [pl.BlockSpec((2, CMP_STRIDE, HD, 2 * HD), lambda s, pt: (0, 0, 0, 0))],
        out_specs=pl.BlockSpec((1, 2, NSA_KV * CMP_PPS * nseg, 2 * HD), lambda s, pt: (s, 0, 0, 0)),
    )
    return pl.pallas_call(
        _cmp1_kernel,
        grid_spec=grid_spec,
        out_shape=jax.ShapeDtypeStruct((nst, 2, NSA_KV * CMP_PPS * nseg, 2 * HD), F32),
        compiler_params=_cparams(("arbitrary",)),
    )(ptab, *([pages] * CMP_PPS), w1cat)


def _cmp2_kernel(a_ref, b_ref, pe_ref, w1_ref, b1_ref, w2_ref, b2_ref, nw_ref, o_ref):
    cc = pl.program_id(0)
    pe = pe_ref[0]
    pec = jnp.dot(pe.astype(BF16), w1_ref[0], preferred_element_type=F32)[0:1, :]
    hid = a_ref[0] + b_ref[0] + pec + b1_ref[0]
    hid = hid * _sigmoid(hid)
    out = jnp.dot(hid.astype(BF16), w2_ref[0], preferred_element_type=F32) + b2_ref[0]

    @pl.when(cc == 0)
    def _():
        o_ref[0] = out * lax.rsqrt(jnp.mean(out * out, axis=-1, keepdims=True) + NORM_EPS) * nw_ref[...]

    @pl.when(cc != 0)
    def _():
        o_ref[0] = out


def _cmp_stage2(a, b, pe8, w1flat, b1, w2, b2, nw, *, tr):
    R = a.shape[1]
    return pl.pallas_call(
        _cmp2_kernel,
        grid=(2, R // tr),
        in_specs=[
            pl.BlockSpec((1, tr, HD), lambda c, i: (c, i, 0)),
            pl.BlockSpec((1, tr, HD), lambda c, i: (c, i, 0)),
            pl.BlockSpec((1, 8, CMP_BLOCK * HD), lambda c, i: (c, 0, 0)),
            pl.BlockSpec((1, CMP_BLOCK * HD, HD), lambda c, i: (c, 0, 0)),
            pl.BlockSpec((1, 1, HD), lambda c, i: (c, 0, 0)),
            pl.BlockSpec((1, HD, HD), lambda c, i: (c, 0, 0)),
            pl.BlockSpec((1, 1, HD), lambda c, i: (c, 0, 0)),
            pl.BlockSpec((1, HD), lambda c, i: (0, 0)),
        ],
        out_specs=pl.BlockSpec((1, tr, HD), lambda c, i: (c, i, 0)),
        out_shape=jax.ShapeDtypeStruct((2, R, HD), F32),
        compiler_params=_cparams(("arbitrary", "arbitrary")),
    )(a, b, pe8, w1flat, b1.reshape(2, 1, HD), w2, b2.reshape(2, 1, HD), nw.reshape(1, HD))


def _stack_heads(qb):
    return jnp.concatenate([qb[:, g * HD:(g + 1) * HD] for g in range(NSA_G)], axis=0)


def _unstack_heads(o, tq):
    return jnp.concatenate([o[g * tq:(g + 1) * tq] for g in range(NSA_G)], axis=1)


def _gate_rows(gt):
    return jnp.concatenate([gt[:, g:g + 1] for g in range(NSA_G)], axis=0)


def _attn_cmp_kernel(q_ref, kc_ref, vc_ref, bias_ref, gate_ref, o_ref, sel_ref, *, tq, q_pos0, nc, ns, nsp):
    i = pl.program_id(2)
    ncp = kc_ref.shape[2]
    qs = _stack_heads(q_ref[...])
    logits = _dot_nt(qs, kc_ref[0, 0])
    logits = logits + jnp.concatenate([bias_ref[0, g] for g in range(NSA_G)], axis=0)
    rows = NSA_G * tq
    t4 = q_pos0 + i * tq + lax.broadcasted_iota(jnp.int32, (rows, ncp), 0) % tq
    cidx = lax.broadcasted_iota(jnp.int32, (rows, ncp), 1)
    mask = (cidx * CMP_STRIDE + (CMP_BLOCK - 1) <= t4) & (cidx < nc)
    lg = jnp.where(mask, logits, NEG)
    mx = jnp.max(lg, axis=-1, keepdims=True)
    ex = jnp.exp(lg - mx)
    p = ex / jnp.sum(ex, axis=-1, keepdims=True) * jnp.where(mask, 1.0, 0.0)
    oc = _dot(p, vc_ref[0, 0])
    o_ref[...] = _unstack_heads(oc * _gate_rows(gate_ref[0, 0, 0]), tq)

    psum = p[0:tq]
    for g in range(1, NSA_G):
        psum = psum + p[g * tq:(g + 1) * tq]
    cr = lax.broadcasted_iota(jnp.int32, (ncp, nsp), 0)
    sc = lax.broadcasted_iota(jnp.int32, (ncp, nsp), 1)
    c2s = (cr * CMP_STRIDE < sc * SEL_BLOCK + SEL_BLOCK) & (cr * CMP_STRIDE + CMP_BLOCK > sc * SEL_BLOCK)
    c2s = jnp.where(c2s & (cr < nc) & (sc < ns), 1.0, 0.0)
    imp = jnp.dot(psum, c2s, preferred_element_type=F32, precision=lax.Precision.HIGHEST)

    t = q_pos0 + i * tq + lax.broadcasted_iota(jnp.int32, (tq, nsp), 0)
    s = lax.broadcasted_iota(jnp.int32, (tq, nsp), 1)
    cur = t // SEL_BLOCK
    forced = (s == 0) | (s == cur) | (s == cur - 1)
    valid = s * SEL_BLOCK <= t
    score = jnp.where(forced, NSA_G + 1.0, jnp.where(valid, imp, -1.0))
    score = jnp.where(s < ns, score, -2.0)
    rank = jnp.zeros((tq, nsp), F32)
    for sp in range(ns):
        other = score[:, sp:sp + 1]
        ahead = (other > score) | ((other == score) & (sp < s))
        rank = rank + jnp.where(ahead, 1.0, 0.0)
    top_k = min(SEL_TOPK, ns)
    sel_ref[0, 0] = jnp.where((rank < top_k) & (s < ns), 1.0, 0.0)


def _attn_cmp(q, kcvc, bias_c, gate, *, batch, seq, tq, q_pos0, nc, ns, nsp):
    nqt = seq // tq
    ncp = kcvc.shape[3]
    kern = functools.partial(_attn_cmp_kernel, tq=tq, q_pos0=q_pos0, nc=nc, ns=ns, nsp=nsp)
    rows_per_b = seq // tq
    return pl.pallas_call(
        kern,
        grid=(batch, NSA_KV, nqt),
        in_specs=[
            pl.BlockSpec((tq, NSA_G * HD), lambda b, h, i: (b * rows_per_b + i, h)),
            pl.BlockSpec((None, 1, 1, ncp, HD), lambda b, h, i: (0, b, h, 0, 0)),
            pl.BlockSpec((None, 1, 1, ncp, HD), lambda b, h, i: (1, b, h, 0, 0)),
            pl.BlockSpec((1, NSA_G, tq, ncp), lambda b, h, i: (h, 0, i, 0)),
            pl.BlockSpec((1, 1, 1, tq, NSA_G), lambda b, h, i: (0, b, h, i, 0)),
        ],
        out_specs=[
            pl.BlockSpec((tq, NSA_G * HD), lambda b, h, i: (b * rows_per_b + i, h)),
            pl.BlockSpec((1, 1, tq, nsp), lambda b, h, i: (b, h, i, 0)),
        ],
        out_shape=[
            jax.ShapeDtypeStruct((batch * seq, NSA_HEADS * HD), F32),
            jax.ShapeDtypeStruct((batch, NSA_KV, seq, nsp), F32),
        ],
        compiler_params=_cparams(("parallel", "parallel", "arbitrary")),
    )(q, kcvc, kcvc, bias_c, gate)


def _flash_kernel(pt_ref, *refs, cfg):
    pps, tq, has_tail, use_sel = cfg["pps"], cfg["tq"], cfg["has_tail"], cfg["use_sel"]
    it = iter(refs)
    q_ref = next(it)
    k_refs = [next(it) for _ in range(pps)]
    v_refs = [next(it) for _ in range(pps)]
    b_refs = [next(it) for _ in range(pps)]
    if has_tail:
        kt_ref, vt_ref, bt_ref = next(it), next(it), next(it)
    sel_ref = next(it) if use_sel else None
    gate_ref = next(it)
    prev_ref = next(it)
    o_ref = next(it)
    m_ref, l_ref, acc_ref = next(it), next(it), next(it)

    i = pl.program_id(2)
    st = pl.program_id(3)
    rows = NSA_G * tq

    @pl.when(st == 0)
    def _():
        m_ref[...] = jnp.full((rows, 1), NEG, F32)
        l_ref[...] = jnp.zeros((rows, 1), F32)
        acc_ref[...] = jnp.zeros((rows, HD), F32)

    t0 = cfg["q_pos0"] + i * tq

    def tile(k, v, bias4, kt, p0):
        qs = _stack_heads(q_ref[...])
        s = _dot_nt(qs, k) + jnp.concatenate([bias4[g] for g in range(NSA_G)], axis=0)
        tt = t0 + lax.broadcasted_iota(jnp.int32, (tq, TK), 0)
        pos = p0 + lax.broadcasted_iota(jnp.int32, (tq, TK), 1)
        dist = tt - pos
        ok = dist >= 0
        if use_sel:
            nsp = sel_ref.shape[3]
            sr = lax.broadcasted_iota(jnp.int32, (nsp, TK), 0)
            sc = lax.broadcasted_iota(jnp.int32, (nsp, TK), 1)
            expand = jnp.where(sr == kt * (TK // SEL_BLOCK) + sc // SEL_BLOCK, 1.0, 0.0)
            picked = _dot(sel_ref[0, 0], expand)
            ok = ok & (picked > 0.5)
        else:
            ok = ok & (dist < WINDOW) & (pos >= cfg["w_pos0"])
        okf = jnp.where(ok, 1.0, 0.0)
        ok4 = jnp.concatenate([okf] * NSA_G, axis=0)
        s = jnp.where(ok4 > 0.5, s, NEG)
        m_old = m_ref[...]
        m_new = jnp.maximum(m_old, jnp.max(s, axis=-1, keepdims=True))
        alpha = jnp.exp(m_old - m_new)
        p = jnp.exp(s - m_new) * ok4
        l_ref[...] = alpha * l_ref[...] + jnp.sum(p, axis=-1, keepdims=True)
        acc_ref[...] = alpha * acc_ref[...] + _dot(p, v)
        m_ref[...] = m_new

    for pp in range(pps):
        kt = cfg["tile_of"](i, st, pp)
        active = cfg["active"](i, st, pp)
        p0 = cfg["kbase"] + kt * TK
        if active is True:
            tile(k_refs[pp][0], v_refs[pp][0], b_refs[pp][0, 0], kt, p0)
        else:
            @pl.when(active)
            def _(pp=pp, kt=kt, p0=p0):
                tile(k_refs[pp][0], v_refs[pp][0], b_refs[pp][0, 0], kt, p0)

    if has_tail:
        @pl.when(st == pl.num_programs(3) - 1)
        def _():
            tile(kt_ref[0], vt_ref[0], bt_ref[0, 0], cfg["tail_tile"], cfg["kbase"] + cfg["tail_tile"] * TK)

    @pl.when(st == pl.num_programs(3) - 1)
    def _():
        o = acc_ref[...] / l_ref[...] * _gate_rows(gate_ref[0, 0, 0])
        o_ref[...] = (prev_ref[...] + _unstack_heads(o, tq)).astype(o_ref.dtype)


def _flash(q, pages, ptab, kcol, vcol, bias_tiles, tails, sel, gate, branch, prev, *, batch, seq, tq, q_pos0,
           pps, nsteps, tile_of, active, kbase, w_pos0, npt, tail_tile, out_dtype):
    nqt = seq // tq
    has_tail = tails is not None
    use_sel = sel is not None
    n_delta = bias_tiles.shape[1]
    cfg = dict(pps=pps, tq=tq, has_tail=has_tail, use_sel=use_sel, q_pos0=q_pos0, tile_of=tile_of,
               active=active, kbase=kbase, w_pos0=w_pos0, tail_tile=tail_tile)

    def page_idx(b, i, s, pp, pt):
        kt = jnp.clip(tile_of(i, s, pp), 0, npt - 1)
        return pt[b * npt + kt]

    def didx(i, s, pp):
        kt = tile_of(i, s, pp)
        return jnp.clip((q_pos0 + i * tq - kbase - kt * TK) // TK, 0, n_delta - 1)

    in_specs = [pl.BlockSpec((tq, NSA_G * HD), lambda b, h, i, s, pt: (b * nqt + i, h))]
    args = [q]
    for col in (kcol, vcol):
        for pp in range(pps):
            in_specs.append(pl.BlockSpec(
                (1, TK, HD), lambda b, h, i, s, pt, pp=pp, col=col: (page_idx(b, i, s, pp, pt), 0, col + h)))
            args.append(pages)
    for pp in range(pps):
        in_specs.append(pl.BlockSpec(
            (1, 1, NSA_G, tq, TK), lambda b, h, i, s, pt, pp=pp: (h, didx(i, s, pp), 0, 0, 0)))
        args.append(bias_tiles)
    if has_tail:
        tail_pages, tkcol, tvcol = tails
        tdelta = min(max((q_pos0 - kbase - tail_tile * TK) // TK, 0), n_delta - 1)
        in_specs.append(pl.BlockSpec((1, TK, HD), lambda b, h, i, s, pt: (b, 0, tkcol + h)))
        in_specs.append(pl.BlockSpec((1, TK, HD), lambda b, h, i, s, pt: (b, 0, tvcol + h)))
        in_specs.append(pl.BlockSpec((1, 1, NSA_G, tq, TK), lambda b, h, i, s, pt: (h, tdelta, 0, 0, 0)))
        args += [tail_pages, tail_pages, bias_tiles]
    if use_sel:
        nsp = sel.shape[3]
        in_specs.append(pl.BlockSpec((1, 1, tq, nsp), lambda b, h, i, s, pt: (b, h, i, 0)))
        args.append(sel)
    in_specs.append(pl.BlockSpec((1, 1, 1, tq, NSA_G), lambda b, h, i, s, pt: (branch, b, h, i, 0)))
    args.append(gate)
    in_specs.append(pl.BlockSpec((tq, NSA_G * HD), lambda b, h, i, s, pt: (b * nqt + i, h)))
    args.append(prev)

    rows = NSA_G * tq
    grid_spec = pltpu.PrefetchScalarGridSpec(
        num_scalar_prefetch=1,
        grid=(batch, NSA_KV, nqt, nsteps),
        in_specs=in_specs,
        out_specs=pl.BlockSpec((tq, NSA_G * HD), lambda b, h, i, s, pt: (b * nqt + i, h)),
        scratch_shapes=[pltpu.VMEM((rows, 1), F32), pltpu.VMEM((rows, 1), F32), pltpu.VMEM((rows, HD), F32)],
    )
    return pl.pallas_call(
        functools.partial(_flash_kernel, cfg=cfg),
        grid_spec=grid_spec,
        out_shape=jax.ShapeDtypeStruct((batch * seq, NSA_HEADS * HD), out_dtype),
        compiler_params=_cparams(("parallel", "parallel", "arbitrary", "arbitrary")),
    )(ptab, *args)


def _rel_bucket(dist):
    n = jnp.maximum(dist, 0)
    max_exact = REL_BUCKETS // 2
    nf = jnp.maximum(n, 1).astype(F32)
    large = max_exact + (jnp.log(nf / max_exact) / math.log(REL_MAX_DIST / max_exact)
                         * (REL_BUCKETS - max_exact)).astype(jnp.int32)
    return jnp.where(n < max_exact, n, jnp.minimum(large, REL_BUCKETS - 1))


def _bias_tiles(rel_bias, tq, n_delta):
    d = jnp.arange(n_delta)[:, None, None] * TK + jnp.arange(tq)[None, :, None] - jnp.arange(TK)[None, None, :]
    t = rel_bias.astype(F32)[_rel_bucket(d)]
    return t.reshape(n_delta, tq, TK, NSA_KV, NSA_G).transpose(3, 0, 4, 1, 2)


def _bias_cmp(rel_bias, q_pos0, tqs, ncp):
    t = q_pos0 + jnp.arange(tqs)
    dist = t[:, None] - (jnp.arange(ncp) * CMP_STRIDE + CMP_BLOCK - 1)[None, :]
    b = rel_bias.astype(F32)[_rel_bucket(dist)]
    return b.reshape(tqs, ncp, NSA_KV, NSA_G).transpose(2, 3, 0, 1)


def _forward_group(x, conv0, ssm0, past, P, *, batch, seq, t_valid, q_pos0, tq):
    M = batch * seq
    conv_out, ssm_out = [], []
    for l in range(2):
        proj = _nmm(x, P["mix_norm"][l], P["gdn_w_main"][l], tn=512)
        bg = _nmm(x, P["mix_norm"][l], P["gdn_w_gate"][l], tn=128, mode="gdn_gate", aux=P["gdn_gate_aux"][l],
                  seq=seq, t_valid=t_valid)
        o, s_new = _gdn(proj, bg, P["gdn_conv_w"][l], conv0[l], ssm0[l], P["gdn_out_norm"][l],
                        batch=batch, seq=seq)
        conv_out.append(proj.reshape(batch, seq, -1)[:, t_valid - (GDN_CONV - 1):t_valid, :GDN_CONV_DIM])
        ssm_out.append(s_new)
        x = _mm_res(o, P["gdn_w_out"][l], x)
        x = _mlp(x, P["mlp_norm"][l], P["mlp_w1"][l], P["mlp_w2"][l])
    x, cmp_rows, sel_rows, win_state = _nsa_layers(x, past, P, batch=batch, seq=seq, t_valid=t_valid,
                                                   q_pos0=q_pos0, tq=tq)
    return x, jnp.stack(conv_out), jnp.stack(ssm_out), cmp_rows, sel_rows, win_state


def _nsa_layers(x, past, P, *, batch, seq, t_valid, q_pos0, tq):
    M = batch * seq
    kv = _nmm(x, P["kv_norm"], P["nsa_w_kv"], tn=512, mode="headnorm", aux=P["kv_aux"], norm_tiles=(2, 4))
    kv3 = kv.reshape(batch, seq, 6 * NSA_KV * HD)
    new_rows = kv3[:, :t_valid]
    cmp_rows = new_rows[..., 0:1024].reshape(batch, t_valid, 2, NSA_KV, HD)
    sel_rows = new_rows[..., 1024:2048].reshape(batch, t_valid, 2, NSA_KV, HD)
    win_new = new_rows[..., 2048:3072]

    ident = jnp.arange(M // PAGE, dtype=jnp.int32) if seq % PAGE == 0 else None
    if past is None:
        n_tot = t_valid
        npages = seq // PAGE
        kv_pages = kv.reshape(M // PAGE, PAGE, 6 * NSA_KV * HD)
        first = _cmp_stage1(kv_pages, ident, P["cmp_w1cat"])
        nsb = npages // CMP_PPS
        nseg_tot = npages * (PAGE // CMP_STRIDE)
        f6 = first.reshape(batch, nsb, 2, NSA_KV, CMP_PPS * 8, 2, HD).transpose(2, 0, 3, 1, 4, 5, 6)
        f6 = f6.reshape(2, batch, NSA_KV, nseg_tot, 2, HD)
        win_seq = win_new
        w_pos0 = 0
    else:
        n_past = past["page_table"].shape[1] * PAGE
        n_tot = n_past + t_valid
        npages = n_past // PAGE
        ptab = past["page_table"].reshape(-1)
        first = _cmp_stage1(past["cmp_pages"], ptab, P["cmp_w1cat"])
        nsb = npages // CMP_PPS
        f6 = first.reshape(batch, nsb, 2, NSA_KV, CMP_PPS * 8, 2, HD).transpose(2, 0, 3, 1, 4, 5, 6)
        f6 = f6.reshape(2, batch, NSA_KV, npages * 8, 2, HD)
        tail_cmp = jnp.pad(new_rows[..., 0:1024], ((0, 0), (0, PAGE - t_valid), (0, 0)))
        tail_cmp = jnp.pad(tail_cmp, ((0, (-batch) % CMP_PPS), (0, 0), (0, 0)))
        tfirst = _cmp_stage1(tail_cmp, jnp.arange(tail_cmp.shape[0], dtype=jnp.int32), P["cmp_w1cat"])
        t6 = tfirst.reshape(-1, 2, NSA_KV, CMP_PPS, 8, 2, HD).transpose(1, 0, 3, 2, 4, 5, 6)
        t6 = t6.reshape(2, -1, NSA_KV, 8, 2, HD)[:, :batch, :, :(-(-t_valid // CMP_STRIDE))]
        f6 = jnp.concatenate([f6, t6], axis=3)
        nseg_tot = f6.shape[3]
        win_seq = jnp.concatenate([past["win"], win_new], axis=1)
        w_pos0 = q_pos0 + t_valid - win_seq.shape[1]
    nc = -(-n_tot // CMP_STRIDE) - 1
    ns = -(-n_tot // SEL_BLOCK)
    ncp = -(-nc // 128) * 128
    nsp = -(-ns // 128) * 128
    a = f6[:, :, :, 0:nc, 0, :]
    b = f6[:, :, :, 1:nc + 1, 1, :]
    if b.shape[3] < nc:
        b = jnp.pad(b, ((0, 0), (0, 0), (0, 0), (0, nc - b.shape[3]), (0, 0)))
    a = jnp.pad(a, ((0, 0), (0, 0), (0, 0), (0, ncp - nc), (0, 0))).reshape(2, batch * NSA_KV * ncp, HD)
    b = jnp.pad(b, ((0, 0), (0, 0), (0, 0), (0, ncp - nc), (0, 0))).reshape(2, batch * NSA_KV * ncp, HD)
    R = batch * NSA_KV * ncp
    kcvc = _cmp_stage2(a, b, P["cmp_pe8"], P["cmp_w1flat"], P["cmp_b1"], P["cmp_w2"], P["cmp_b2"],
                       P["k_cmp_norm"], tr=min(R, 2048))
    kcvc = kcvc.reshape(2, batch, NSA_KV, ncp, HD)

    n_keep = min(WINDOW, win_seq.shape[1])
    win_state = win_seq[:, win_seq.shape[1] - n_keep:].reshape(batch, n_keep, 2, NSA_KV, HD)

    seq_q = seq if past is None else tq
    bias_c = _bias_cmp(P["rel_bias"], q_pos0, seq_q, ncp)
    btiles = _bias_tiles(P["rel_bias"], tq, min(N_DELTA, (q_pos0 + seq_q) // TK + 1))
    if past is None:
        sel_pages, sel_ptab, sel_npt = kv_pages, ident, seq // PAGE
        sel_kcol, sel_vcol = 8, 12
        sel_tails = None
        sel_pps = 4
        sel_steps = -(-sel_npt // sel_pps)
        sel_tile_of = lambda i, s, pp: s * sel_pps + pp
        sel_active = lambda i, s, pp: (s * sel_pps + pp) * TK <= i * tq + tq - 1
        win_pages, win_ptab, win_npt = kv_pages, ident, seq // PAGE
        win_kcol, win_vcol = 16, 20
        win_pps = WINDOW // TK + tq // TK
        win_tile_of = lambda i, s, pp: (i * tq) // TK - WINDOW // TK + pp
        win_active = lambda i, s, pp: (i * tq) // TK - WINDOW // TK + pp >= 0
        win_kbase = 0
    else:
        sel_pages, sel_ptab, sel_npt = past["sel_pages"], past["page_table"].reshape(-1), npages
        sel_kcol, sel_vcol = 0, 4
        tail_sel = jnp.pad(new_rows[..., 1024:2048], ((0, 0), (0, PAGE - t_valid), (0, 0)))
        sel_tails = (tail_sel, 0, 4)
        sel_pps = 8
        sel_steps = sel_npt // sel_pps
        sel_tile_of = lambda i, s, pp: s * sel_pps + pp
        sel_active = lambda i, s, pp: True
        nwt = -(-win_seq.shape[1] // TK)
        win_pages = jnp.pad(win_seq, ((0, 0), (0, nwt * TK - win_seq.shape[1]), (0, 0)))
        win_pages = win_pages.reshape(batch * nwt, TK, 2 * NSA_KV * HD)
        win_ptab, win_npt = jnp.arange(batch * nwt, dtype=jnp.int32), nwt
        win_kcol, win_vcol = 0, 4
        win_pps = nwt
        win_tile_of = lambda i, s, pp: pp
        win_active = lambda i, s, pp: True
        win_kbase = w_pos0

    for jj in range(2):
        l = 2 + jj
        q = _nmm(x, P["mix_norm"][l], P["nsa_w_q"][jj], tn=512, out_dtype=BF16, mode="headnorm",
                 aux=P["nsa_q_aux"][jj], scale=HD ** -0.5)
        gates = _nmm(x, P["mix_norm"][l], P["nsa_w_g"][jj], tn=128, mode="sigmoid")
        gate = gates[:, :NSA_HEADS * 3].reshape(batch, seq, NSA_KV, NSA_G, 3).transpose(4, 0, 2, 1, 3)
        if seq_q != seq:
            q = q.reshape(batch, seq, -1)[:, :seq_q].reshape(batch * seq_q, -1)
            gate = gate[:, :, :, :seq_q]
        o_c, sel = _attn_cmp(q, kcvc, bias_c, gate, batch=batch, seq=seq_q, tq=tq, q_pos0=q_pos0, nc=nc, ns=ns,
                             nsp=nsp)
        o_s = _flash(q, sel_pages, sel_ptab, sel_kcol, sel_vcol, btiles, sel_tails, sel, gate, 1, o_c,
                     batch=batch, seq=seq_q, tq=tq, q_pos0=q_pos0, pps=sel_pps,
                     nsteps=sel_steps, tile_of=sel_tile_of, active=sel_active, kbase=0, w_pos0=0,
                     npt=sel_npt, tail_tile=sel_npt, out_dtype=F32)
        o_w = _flash(q, win_pages, win_ptab, win_kcol, win_vcol, btiles, None, None, gate, 2, o_s,
                     batch=batch, seq=seq_q, tq=tq, q_pos0=q_pos0, pps=win_pps,
                     nsteps=1, tile_of=win_tile_of, active=win_active, kbase=win_kbase, w_pos0=w_pos0,
                     npt=win_npt, tail_tile=0, out_dtype=BF16 if tq % 16 == 0 else F32)
        if seq_q != seq:
            o_w = jnp.pad(o_w.reshape(batch, seq_q, -1), ((0, 0), (0, seq - seq_q), (0, 0))).reshape(M, -1)
        x = _mm_res(o_w.astype(BF16), P["nsa_w_out"][jj], x)
        x = _mlp(x, P["mlp_norm"][l], P["mlp_w1"][l], P["mlp_w2"][l])
    return x, cmp_rows, sel_rows, win_state


def _prepare_params(mix_norm, mlp_norm, mlp_w1, mlp_w2, gdn_w_in, gdn_conv_w, gdn_a_log, gdn_dt_bias,
                    gdn_out_norm, gdn_w_out, kv_norm, nsa_w_kv, k_sel_norm, k_win_norm, k_cmp_norm, cmp_pe,
                    cmp_w1, cmp_b1, cmp_w2, cmp_b2, nsa_w_in, nsa_q_norm, nsa_w_out, rel_bias):
    n_lay = gdn_w_in.shape[0]
    main = GDN_CONV_DIM + GDN_VAL_DIM
    zpad = lambda n: jnp.zeros((1, n), F32)
    gate_aux = jnp.stack([
        jnp.concatenate([
            jnp.concatenate([zpad(GDN_V_HEADS), gdn_a_log[l][None].astype(F32), zpad(HD - 2 * GDN_V_HEADS)], 1),
            jnp.concatenate([zpad(GDN_V_HEADS), gdn_dt_bias[l][None].astype(F32), zpad(HD - 2 * GDN_V_HEADS)], 1),
        ], 0)[None] for l in range(n_lay)])
    tile4 = lambda w: jnp.tile(w.astype(F32), NSA_KV)[None, None]
    kv_aux = jnp.concatenate([jnp.ones((2, 1, 512), F32), tile4(k_sel_norm), jnp.ones((1, 1, 512), F32),
                              tile4(k_win_norm), jnp.ones((1, 1, 512), F32)], 0)
    nq = NSA_HEADS * HD
    w1r = cmp_w1.reshape(2, 2, CMP_STRIDE, HD, HD)
    P = dict(
        mix_norm=mix_norm, mlp_norm=mlp_norm,
        mlp_w1=mlp_w1.astype(BF16), mlp_w2=mlp_w2.astype(BF16),
        gdn_w_main=gdn_w_in[:, :, :main].astype(BF16),
        gdn_w_gate=jnp.pad(gdn_w_in[:, :, main:], ((0, 0), (0, 0), (0, HD - 2 * GDN_V_HEADS))).astype(BF16),
        gdn_gate_aux=gate_aux, gdn_conv_w=gdn_conv_w, gdn_out_norm=gdn_out_norm,
        gdn_w_out=gdn_w_out.astype(BF16),
        kv_norm=kv_norm, nsa_w_kv=nsa_w_kv.astype(BF16), kv_aux=kv_aux, k_cmp_norm=k_cmp_norm,
        cmp_w1cat=jnp.concatenate([w1r[:, 0], w1r[:, 1]], axis=-1).astype(BF16),
        cmp_w1flat=cmp_w1.reshape(2, CMP_BLOCK * HD, HD).astype(BF16),
        cmp_pe8=jnp.pad(cmp_pe.reshape(2, 1, CMP_BLOCK * HD), ((0, 0), (0, 7), (0, 0))),
        cmp_b1=cmp_b1, cmp_w2=cmp_w2.astype(BF16), cmp_b2=cmp_b2,
        nsa_w_q=nsa_w_in[:, :, :nq].astype(BF16),
        nsa_w_g=jnp.pad(nsa_w_in[:, :, nq:], ((0, 0), (0, 0), (0, HD - 3 * NSA_HEADS))).astype(BF16),
        nsa_q_aux=jnp.stack([jnp.tile(tile4(nsa_q_norm[jj]), (nq // 512, 1, 1)) for jj in range(2)]),
        nsa_w_out=nsa_w_out.astype(BF16), rel_bias=rel_bias,
    )
    return P


def kernel(x_prompt, x_sample, state_conv, state_ssm, cache_cmp, cache_sel, cache_win, page_table, mix_norm,
           mlp_norm, mlp_w1, mlp_w2, gdn_w_in, gdn_conv_w, gdn_a_log, gdn_dt_bias, gdn_out_norm, gdn_w_out,
           kv_norm, nsa_w_kv, k_sel_norm, k_win_norm, k_cmp_norm, cmp_pe, cmp_w1, cmp_b1, cmp_w2, cmp_b2,
           nsa_w_in, nsa_q_norm, nsa_w_out, rel_bias):
    bp, tp, _ = x_prompt.shape
    bs, ts, _ = x_sample.shape
    n_lay = gdn_w_in.shape[0]
    P = _prepare_params(mix_norm, mlp_norm, mlp_w1, mlp_w2, gdn_w_in, gdn_conv_w, gdn_a_log, gdn_dt_bias,
                        gdn_out_norm, gdn_w_out, kv_norm, nsa_w_kv, k_sel_norm, k_win_norm, k_cmp_norm, cmp_pe,
                        cmp_w1, cmp_b1, cmp_w2, cmp_b2, nsa_w_in, nsa_q_norm, nsa_w_out, rel_bias)

    conv0 =jnp.zeros((n_lay, bp, GDN_CONV - 1, GDN_CONV_DIM), F32)
    ssm0 = jnp.zeros((n_lay, bp, GDN_V_HEADS, HD, HD), F32)
    yp, conv_p, ssm_p, cmp_p, sel_p, win_p = _forward_group(
        x_prompt.reshape(bp * tp, D_MODEL), conv0, ssm0, None, P,
        batch=bp, seq=tp, t_valid=tp, q_pos0=0, tq=128)

    seq_s = GDN_CHUNK
    xs = jnp.pad(x_sample, ((0, 0), (0, seq_s - ts), (0, 0))).reshape(bs * seq_s, D_MODEL)
    n_pool = cache_cmp.shape[0]
    past = dict(cmp_pages=cache_cmp.reshape(n_pool, PAGE, 2 * NSA_KV * HD),
                sel_pages=cache_sel.reshape(n_pool, PAGE, 2 * NSA_KV * HD),
                page_table=page_table.astype(jnp.int32),
                win=cache_win.reshape(bs, cache_win.shape[1], 2 * NSA_KV * HD))
    n_past = page_table.shape[1] * PAGE
    ys, conv_s, ssm_s, cmp_s, sel_s, win_s = _forward_group(
        xs, state_conv, state_ssm, past, P, batch=bs, seq=seq_s, t_valid=ts, q_pos0=n_past, tq=8)
    y_sample = ys.reshape(bs, seq_s, D_MODEL)[:, :ts]
    return (yp.reshape(bp, tp, D_MODEL), y_sample, conv_p, ssm_p, cmp_p, sel_p, win_p,
            conv_s, ssm_s, cmp_s, sel_s, win_s)
```

```python
import functools
import math

import jax
import jax.numpy as jnp
from jax import lax
from jax.experimental import pallas as pl
from jax.experimental.pallas import tpu as pltpu

F32 = jnp.float32
BF16 = jnp.bfloat16

D_MODEL = 2048
D_FF = 4 * D_MODEL
NORM_EPS = 1e-6
L2_EPS = 1e-6
PAGE = 128

HD = 128
GDN_QK_HEADS = 16
GDN_V_HEADS = 32
GDN_KEY_DIM = GDN_QK_HEADS * HD
GDN_VAL_DIM = GDN_V_HEADS * HD
GDN_CONV = 4
GDN_CHUNK = 64
GDN_CONV_DIM = 2 * GDN_KEY_DIM + GDN_VAL_DIM

NSA_HEADS = 16
NSA_KV = 4
NSA_G = NSA_HEADS // NSA_KV
CMP_BLOCK = 32
CMP_STRIDE = 16
SEL_BLOCK = 64
SEL_TOPK = 16
WINDOW = 512
REL_BUCKETS = 32
REL_MAX_DIST = 4096
NEG = -1e30

TK = 128
N_DELTA = REL_MAX_DIST // TK + 2

VMEM_LIMIT = 56 * 1024 * 1024


def _cparams(sem):
    return pltpu.CompilerParams(dimension_semantics=sem, vmem_limit_bytes=VMEM_LIMIT)


def _sigmoid(x):
    return 1.0 / (1.0 + jnp.exp(-x))


def _softplus(x):
    return jnp.maximum(x, 0.0) + jnp.log(1.0 + jnp.exp(-jnp.abs(x)))


def _dot(a, b):
    return jnp.dot(a.astype(BF16), b.astype(BF16), preferred_element_type=F32)


def _dot_nt(a, b):
    return lax.dot_general(a.astype(BF16), b.astype(BF16), (((1,), (1,)), ((), ())),
                           preferred_element_type=F32)


def _dot_tn(a, b):
    return lax.dot_general(a.astype(BF16), b.astype(BF16), (((0,), (0,)), ((), ())),
                           preferred_element_type=F32)


def _headnorm(acc, gw):
    parts = []
    for g in range(acc.shape[1] // HD):
        a = acc[:, g * HD:(g + 1) * HD]
        parts.append(a * lax.rsqrt(jnp.mean(a * a, axis=-1, keepdims=True) + NORM_EPS))
    return jnp.concatenate(parts, axis=1) * gw


def _nmm_kernel(x_ref, nw_ref, w_ref, aux_ref, o_ref, h_ref, *, mode, norm_tiles, scale, seq, t_valid):
    i = pl.program_id(0)
    j = pl.program_id(1)

    @pl.when(j == 0)
    def _():
        x = x_ref[...]
        h = x * lax.rsqrt(jnp.mean(x * x, axis=-1, keepdims=True) + NORM_EPS) * nw_ref[...]
        h_ref[...] = h.astype(BF16)

    acc = jnp.dot(h_ref[...], w_ref[...], preferred_element_type=F32)
    if mode == "plain":
        o_ref[...] = acc.astype(o_ref.dtype)
    elif mode == "headnorm":
        if norm_tiles is None:
            o_ref[...] = (_headnorm(acc, aux_ref[0]) * scale).astype(o_ref.dtype)
        else:
            is_n = functools.reduce(jnp.logical_or, [j == t for t in norm_tiles])

            @pl.when(is_n)
            def _():
                o_ref[...] = (_headnorm(acc, aux_ref[0]) * scale).astype(o_ref.dtype)

            @pl.when(jnp.logical_not(is_n))
            def _():
                o_ref[...] = acc.astype(o_ref.dtype)
    elif mode == "sigmoid":
        o_ref[...] = _sigmoid(acc)
    elif mode == "gdn_gate":
        tm = acc.shape[0]
        aux = aux_ref[0]
        lane = lax.broadcasted_iota(jnp.int32, acc.shape, 1)
        row = lax.broadcasted_iota(jnp.int32, acc.shape, 0) + i * tm
        live = (row % seq) < t_valid
        beta = jnp.where(live, _sigmoid(acc), 0.0)
        g = jnp.where(live, -jnp.exp(aux[0:1, :]) * _softplus(acc + aux[1:2, :]), 0.0)
        g = jnp.where((lane >= GDN_V_HEADS) & (lane < 2 * GDN_V_HEADS), g, 0.0)
        r = lax.broadcasted_iota(jnp.int32, (tm, tm), 0)
        c = lax.broadcasted_iota(jnp.int32, (tm, tm), 1)
        tri = ((r // GDN_CHUNK) == (c // GDN_CHUNK)) & (c <= r)
        gcum = jnp.dot(jnp.where(tri, 1.0, 0.0), g, preferred_element_type=F32,
                       precision=lax.Precision.HIGHEST)
        o_ref[...] = jnp.where(lane < GDN_V_HEADS, beta, gcum)
    else:
        raise ValueError(mode)


def _nmm(x, nw, w, *, tn, out_dtype=F32, mode="plain", aux=None, norm_tiles=None, scale=1.0,
         seq=1, t_valid=1, tm=512):
    M, K = x.shape
    N = w.shape[1]
    tm = min(tm, M)
    assert M % tm == 0 and N % tn == 0
    if aux is None:
        aux = jnp.zeros((N // tn, 1, tn), F32)
    kern = functools.partial(_nmm_kernel, mode=mode, norm_tiles=norm_tiles, scale=scale, seq=seq,
                             t_valid=t_valid)
    return pl.pallas_call(
        kern,
        grid=(M // tm, N // tn),
        in_specs=[
            pl.BlockSpec((tm, K), lambda i, j: (i, 0)),
            pl.BlockSpec((1, K), lambda i, j: (0, 0)),
            pl.BlockSpec((K, tn), lambda i, j: (0, j)),
            pl.BlockSpec((1,) + aux.shape[1:], lambda i, j: (j, 0, 0)),
        ],
        out_specs=pl.BlockSpec((tm, tn), lambda i, j: (i, j)),
        out_shape=jax.ShapeDtypeStruct((M, N), out_dtype),
        scratch_shapes=[pltpu.VMEM((tm, K), BF16)],
        compiler_params=_cparams(("parallel", "arbitrary")),
    )(x, nw.reshape(1, K), w, aux)


def _mmres_kernel(x_ref, w_ref, r_ref, o_ref):
    o_ref[...] = r_ref[...] + jnp.dot(x_ref[...], w_ref[...], preferred_element_type=F32)


def _mm_res(x, w, res, *, tm=512, tn=512):
    M, K = x.shape
    N = w.shape[1]
    tm = min(tm, M)
    return pl.pallas_call(
        _mmres_kernel,
        grid=(M // tm, N // tn),
        in_specs=[
            pl.BlockSpec((tm, K), lambda i, j: (i, 0)),
            pl.BlockSpec((K, tn), lambda i, j: (0, j)),
            pl.BlockSpec((tm, tn), lambda i, j: (i, j)),
        ],
        out_specs=pl.BlockSpec((tm, tn), lambda i, j: (i, j)),
        out_shape=jax.ShapeDtypeStruct((M, N), F32),
        compiler_params=_cparams(("parallel", "arbitrary")),
    )(x, w, res)


def _mlp_kernel(x_ref, nw_ref, w1_ref, w2_ref, o_ref, h_ref, acc_ref):
    f = pl.program_id(1)

    @pl.when(f == 0)
    def _():
        x = x_ref[...]
        h = x * lax.rsqrt(jnp.mean(x * x, axis=-1, keepdims=True) + NORM_EPS) * nw_ref[...]
        h_ref[...] = h.astype(BF16)
        acc_ref[...] = x

    a = jnp.maximum(jnp.dot(h_ref[...], w1_ref[...], preferred_element_type=F32), 0.0)
    acc_ref[...] += jnp.dot((a * a).astype(BF16), w2_ref[...], preferred_element_type=F32)

    @pl.when(f == pl.num_programs(1) - 1)
    def _():
        o_ref[...] = acc_ref[...]


def _mlp(x, nw, w1, w2, *, tm=512, tf=512):
    M, D = x.shape
    Fdim = w1.shape[1]
    tm = min(tm, M)
    return pl.pallas_call(
        _mlp_kernel,
        grid=(M // tm, Fdim // tf),
        in_specs=[
            pl.BlockSpec((tm, D), lambda i, f: (i, 0)),
            pl.BlockSpec((1, D), lambda i, f: (0, 0)),
            pl.BlockSpec((D, tf), lambda i, f: (0, f)),
            pl.BlockSpec((tf, D), lambda i, f: (f, 0)),
        ],
        out_specs=pl.BlockSpec((tm, D), lambda i, f: (i, 0)),
        out_shape=jax.ShapeDtypeStruct((M, D), F32),
        scratch_shapes=[pltpu.VMEM((tm, D), BF16), pltpu.VMEM((tm, D), F32)],
        compiler_params=_cparams(("parallel", "arbitrary")),
    )(x, nw.reshape(1, D), w1, w2)


def _unit_lower_inverse(mats, r, c):
    eye = jnp.where(r == c, 1.0, 0.0)
    in8 = (r // 8) == (c // 8)
    d0 = [jnp.where(in8, a, 0.0) for a in mats]
    d2 = [_dot(d, d) for d in d0]
    d4 = [_dot(d, d) for d in d2]
    x = [_dot(eye - a, eye + b) for a, b in zip(d0, d2)]
    x = [_dot(a, eye + b) for a, b in zip(x, d4)]
    s = 8
    while s < GDN_CHUNK:
        off = ((r // (2 * s)) == (c // (2 * s))) & ((r // s) != (c // s))
        bx = [_dot(jnp.where(off, a, 0.0), xi) for a, xi in zip(mats, x)]
        xbx = [_dot(xi, b) for xi, b in zip(x, bx)]
        x = [xi - b for xi, b in zip(x, xbx)]
        s *= 2
    return x


GDN_PAIRS = 8


def _gdn_chunk(qn, kn, vc, zb, beta, gc, st, onw):
    C = GDN_CHUNK
    R = 2 * C
    n = len(qn)
    rcol = lax.broadcasted_iota(jnp.int32, (R, 1), 0)
    top = rcol < C
    r = lax.broadcasted_iota(jnp.int32, (R, R), 0)
    c = lax.broadcasted_iota(jnp.int32, (R, R), 1)
    same = (r // C) == (c // C)
    low = same & (c <= r)
    slow = same & (c < r)
    srow = lax.broadcasted_iota(jnp.int32, (2 * HD, 1), 0)

    def blocked(a):
        return jnp.concatenate([jnp.where(top, a, 0.0), jnp.where(top, 0.0, a)], axis=1)

    beta2 = [jnp.concatenate(b, axis=0) for b in beta]
    gc2 = [jnp.concatenate(g, axis=0) for g in gc]
    gl2 = [jnp.where(top, g[0][C - 1:C, :], g[1][C - 1:C, :]) for g in gc]
    gls = [jnp.exp(jnp.where(srow < HD, g[0][C - 1:C, :], g[1][C - 1:C, :])) for g in gc]
    dec = []
    for g2 in gc2:
        colm = jnp.broadcast_to(g2, (R, R))
        dec.append(jnp.where(low, jnp.exp(jnp.where(low, colm - colm.T, 0.0)), 0.0))
    k2 = [jnp.concatenate([k, k], axis=0) for k in kn]
    q2 = [jnp.concatenate([q, q], axis=0) for q in qn]
    v2 = [jnp.concatenate([v[:, :HD], v[:, HD:]], axis=0) for v in vc]
    kk = [_dot_nt(k, k) for k in k2]
    qk = [_dot_nt(q, k) for q, k in zip(q2, k2)]
    amat = [jnp.where(slow, kk[i] * beta2[i] * dec[i], 0.0) for i in range(n)]
    attn = [qk[i] * dec[i] for i in range(n)]
    tinv = _unit_lower_inverse(amat, r, c)

    e2 = [jnp.exp(g) for g in gc2]
    rhs = [jnp.concatenate([v2[i] * beta2[i], k2[i] * beta2[i] * e2[i]], axis=1) for i in range(n)]
    sol = [_dot(tinv[i], rhs[i]) for i in range(n)]
    lhs = [jnp.concatenate([blocked(sol[i][:, HD:]), blocked(q2[i] * e2[i])], axis=0) for i in range(n)]
    ws = [_dot(lhs[i], st[i]) for i in range(n)]
    vnew = [sol[i][:, :HD] - ws[i][:R] for i in range(n)]
    av = [_dot(attn[i], vnew[i]) for i in range(n)]
    kd = [blocked(k2[i] * jnp.exp(gl2[i] - gc2[i])) for i in range(n)]
    kv = [_dot_tn(kd[i], vnew[i]) for i in range(n)]
    st_new = [st[i] * gls[i] + kv[i] for i in range(n)]

    outs = []
    for i in range(n):
        o2 = ws[i][R:] + av[i]
        z2 = jnp.concatenate([zb[i][:, :HD], zb[i][:, HD:]], axis=0)
        on = o2 * lax.rsqrt(jnp.mean(o2 * o2, axis=-1, keepdims=True) + NORM_EPS) * onw
        out2 = on * (z2 * _sigmoid(z2))
        outs.append(jnp.concatenate([out2[:C], out2[C:]], axis=1))
    return outs, st_new


def _gdn_kernel(q_ref, k_ref, v_ref, z_ref, bg_ref, wq_ref, wk_ref, wv_ref, cq_ref, ck_ref, cv_ref,
                s0_ref, onw_ref, o_ref, sout_ref, st_ref, bq_ref, bk_ref, bv_ref):
    C = GDN_CHUNK
    G = GDN_PAIRS
    jg = pl.program_id(1)
    ch = pl.program_id(2)

    @pl.when(ch == 0)
    def _():
        for p in range(G):
            st_ref[p, 0:HD, :] = s0_ref[0, 2 * p]
            st_ref[p, HD:2 * HD, :] = s0_ref[0, 2 * p + 1]
        bq_ref[5:8, :] = cq_ref[0]
        bk_ref[5:8, :] = ck_ref[0]
        bv_ref[5:8, :] = cv_ref[0]

    def conv_silu(x_ref, buf_ref, w_ref):
        buf_ref[8:8 + C, :] = x_ref[...]
        w = w_ref[...]
        y = w[0:1, :] * buf_ref[5:5 + C, :]
        for t in range(1, GDN_CONV):
            y = y + w[t:t + 1, :] * buf_ref[5 + t:5 + t + C, :]
        buf_ref[5:8, :] = buf_ref[5 + C:8 + C, :]
        return y * _sigmoid(y)

    qc = conv_silu(q_ref, bq_ref, wq_ref)
    kc = conv_silu(k_ref, bk_ref, wk_ref)
    vc = conv_silu(v_ref, bv_ref, wv_ref)
    zb = z_ref[...]
    bg = bg_ref[...]
    lane = lax.broadcasted_iota(jnp.int32, bg.shape, 1)

    def col(idx):
        return jnp.sum(jnp.where(lane == idx, bg, 0.0), axis=-1, keepdims=True)

    qn, kn, beta, gc = [], [], [], []
    for p in range(G):
        qp = qc[:, p * HD:(p + 1) * HD]
        kp = kc[:, p * HD:(p + 1) * HD]
        qn.append(qp * lax.rsqrt(jnp.sum(qp * qp, axis=-1, keepdims=True) + L2_EPS) * (HD ** -0.5))
        kn.append(kp * lax.rsqrt(jnp.sum(kp * kp, axis=-1, keepdims=True) + L2_EPS))
        head = 2 * (jg * G + p)
        beta.append((col(head), col(head + 1)))
        gc.append((col(GDN_V_HEADS + head), col(GDN_V_HEADS + head + 1)))
    outs, new_states = _gdn_chunk(
        qn, kn, [vc[:, 2 * p * HD:2 * (p + 1) * HD] for p in range(G)],
        [zb[:, 2 * p * HD:2 * (p + 1) * HD] for p in range(G)], beta, gc,
        [st_ref[p] for p in range(G)], onw_ref[...])
    for p in range(G):
        st_ref[p] = new_states[p]
    o_ref[...] = jnp.concatenate(outs, axis=1).astype(o_ref.dtype)

    @pl.when(ch == pl.num_programs(2) - 1)
    def _():
        for p in range(G):
            sout_ref[0, 2 * p] = st_ref[p, 0:HD, :]
            sout_ref[0, 2 * p + 1] = st_ref[p, HD:2 * HD, :]


def _gdn(proj, bg, conv_w, conv0, ssm0, out_norm, *, batch, seq):
    C = GDN_CHUNK
    G = GDN_PAIRS
    nch = seq // C
    ng = GDN_QK_HEADS // G
    row = lambda b, j, c: b * nch + c
    return pl.pallas_call(
        _gdn_kernel,
        grid=(batch, ng, nch),
        in_specs=[
            pl.BlockSpec((C, G * HD), lambda b, j, c: (row(b, j, c), j)),
            pl.BlockSpec((C, G * HD), lambda b, j, c: (row(b, j, c), ng + j)),
            pl.BlockSpec((C, 2 * G * HD), lambda b, j, c: (row(b, j, c), ng + j)),
            pl.BlockSpec((C, 2 * G * HD), lambda b, j, c: (row(b, j, c), 2 * ng + j)),
            pl.BlockSpec((C, HD), lambda b, j, c: (row(b, j, c), 0)),
            pl.BlockSpec((GDN_CONV, G * HD), lambda b, j, c: (0, j)),
            pl.BlockSpec((GDN_CONV, G * HD), lambda b, j, c: (0, ng + j)),
            pl.BlockSpec((GDN_CONV, 2 * G * HD), lambda b, j, c: (0, ng + j)),
            pl.BlockSpec((1, GDN_CONV - 1, G * HD), lambda b, j, c: (b, 0, j)),
            pl.BlockSpec((1, GDN_CONV - 1, G * HD), lambda b, j, c: (b, 0, ng + j)),
            pl.BlockSpec((1, GDN_CONV - 1, 2 * G * HD), lambda b, j, c: (b, 0, ng + j)),
            pl.BlockSpec((1, 2 * G, HD, HD), lambda b, j, c: (b, j, 0, 0)),
            pl.BlockSpec((1, HD), lambda b, j, c: (0, 0)),
        ],
        out_specs=[
            pl.BlockSpec((C, 2 * G * HD), lambda b, j, c: (row(b, j, c), j)),
            pl.BlockSpec((1, 2 * G, HD, HD), lambda b, j, c: (b, j, 0, 0)),
        ],
        out_shape=[
            jax.ShapeDtypeStruct((batch * seq, GDN_VAL_DIM), BF16),
            jax.ShapeDtypeStruct((batch, GDN_V_HEADS, HD, HD), F32),
        ],
        scratch_shapes=[
            pltpu.VMEM((G, 2 * HD, HD), F32),
            pltpu.VMEM((8 + C, G * HD), F32),
            pltpu.VMEM((8 + C, G * HD), F32),
            pltpu.VMEM((8 + C, 2 * G * HD), F32),
        ],
        compiler_params=_cparams(("parallel", "parallel", "arbitrary")),
        name="gdn",
    )(proj, proj, proj, proj, bg, conv_w, conv_w, conv_w, conv0, conv0, conv0, ssm0,
      out_norm.reshape(1, HD))


CMP_PPS = 8


def _cmp1_kernel(pt_ref, *refs, row_packed):
    page_refs = refs[:CMP_PPS]
    w_ref = refs[CMP_PPS]
    o_ref = refs[CMP_PPS + 1]
    nseg = PAGE // CMP_STRIDE
    nch = 2 * NSA_KV
    pr = lax.broadcasted_iota(jnp.int32, (PAGE, PAGE), 0)
    pc = lax.broadcasted_iota(jnp.int32, (PAGE, PAGE), 1)
    perm = jnp.where(pc == (pr % nseg) * CMP_STRIDE + pr // nseg, 1.0, 0.0).astype(BF16)

    def slab(p, ch):
        if row_packed:
            return p[pl.ds(ch, PAGE, stride=nch), :]
        return p[0, :, ch * HD:(ch + 1) * HD]

    perm_slabs = [[jnp.dot(perm, slab(p, ch).astype(BF16), preferred_element_type=F32) for ch in range(nch)]
                  for p in page_refs]
    for cc in range(2):
        acc = jnp.zeros((NSA_KV * CMP_PPS * nseg, 2 * HD), F32)
        for rp in range(CMP_STRIDE // 2):
            lhs = jnp.concatenate(
                [jnp.concatenate([ps[cc * NSA_KV + h][(2 * rp) * nseg:(2 * rp + 1) * nseg],
                                  ps[cc * NSA_KV + h][(2 * rp + 1) * nseg:(2 * rp + 2) * nseg]], axis=1)
                 for h in range(NSA_KV) for ps in perm_slabs], axis=0)
            acc = acc + jnp.dot(lhs.astype(BF16), w_ref[cc, rp], preferred_element_type=F32)
        o_ref[0, cc] = acc


def _cmp_stage1(pages, ptab, w1cat, *, row_packed):
    n = ptab.shape[0]
    nst = n // CMP_PPS
    nseg = PAGE // CMP_STRIDE
    if row_packed:
        specs = [pl.BlockSpec((PAGE * 2 * NSA_KV, HD), lambda s, pt, p=p: (pt[s * CMP_PPS + p], 0))
                 for p in range(CMP_PPS)]
    else:
        specs = [pl.BlockSpec((1, PAGE, 2 * NSA_KV * HD), lambda s, pt, p=p: (pt[s * CMP_PPS + p], 0, 0))
                 for p in range(CMP_PPS)]
    grid_spec = pltpu.PrefetchScalarGridSpec(
        num_scalar_prefetch=1,
        grid=(nst,),
        in_specs=specs + [pl.BlockSpec((2, CMP_STRIDE // 2, 2 * HD, 2 * HD), lambda s, pt: (0, 0, 0, 0))],
        out_specs=pl.BlockSpec((1, 2, NSA_KV * CMP_PPS * nseg, 2 * HD), lambda s, pt: (s, 0, 0, 0)),
    )
    return pl.pallas_call(
        functools.partial(_cmp1_kernel, row_packed=row_packed),
        grid_spec=grid_spec,
        out_shape=jax.ShapeDtypeStruct((nst, 2, NSA_KV * CMP_PPS * nseg, 2 * HD), F32),
        compiler_params=_cparams(("arbitrary",)),
        name="cmp_stage1",
    )(ptab, *([pages] * CMP_PPS), w1cat.reshape(2, CMP_STRIDE // 2, 2 * HD, 2 * HD))


def _cmp2_kernel(a_ref, b_ref, pe_ref, w1_ref, b1_ref, w2_ref, b2_ref, nw_ref, o_ref):
    cc = pl.program_id(0)
    pe = pe_ref[0]
    pec = jnp.dot(pe.astype(BF16), w1_ref[0], preferred_element_type=F32)[0:1, :]
    hid = a_ref[0] + b_ref[0] + pec + b1_ref[0]
    hid = hid * _sigmoid(hid)
    out = jnp.dot(hid.astype(BF16), w2_ref[0], preferred_element_type=F32) + b2_ref[0]

    @pl.when(cc == 0)
    def _():
        o_ref[0] = out * lax.rsqrt(jnp.mean(out * out, axis=-1, keepdims=True) + NORM_EPS) * nw_ref[...]

    @pl.when(cc != 0)
    def _():
        o_ref[0] = out


def _cmp_stage2(a, b, pe8, w1flat, b1, w2, b2, nw, *, tr):
    R = a.shape[1]
    return pl.pallas_call(
        _cmp2_kernel,
        grid=(2, R // tr),
        in_specs=[
            pl.BlockSpec((1, tr, HD), lambda c, i: (c, i, 0)),
            pl.BlockSpec((1, tr, HD), lambda c, i: (c, i, 0)),
            pl.BlockSpec((1, 8, CMP_BLOCK * HD), lambda c, i: (c, 0, 0)),
            pl.BlockSpec((1, CMP_BLOCK * HD, HD), lambda c, i: (c, 0, 0)),
            pl.BlockSpec((1, 1, HD), lambda c, i: (c, 0, 0)),
            pl.BlockSpec((1, HD, HD), lambda c, i: (c, 0, 0)),
            pl.BlockSpec((1, 1, HD), lambda c, i: (c, 0, 0)),
            pl.BlockSpec((1, HD), lambda c, i: (0, 0)),
        ],
        out_specs=pl.BlockSpec((1, tr, HD), lambda c, i: (c, i, 0)),
        out_shape=jax.ShapeDtypeStruct((2, R, HD), F32),
        compiler_params=_cparams(("arbitrary", "arbitrary")),
    )(a, b, pe8, w1flat, b1.reshape(2, 1, HD), w2, b2.reshape(2, 1, HD), nw.reshape(1, HD))


def _stack_heads(qb):
    return jnp.concatenate([qb[:, g * HD:(g + 1) * HD] for g in range(NSA_G)], axis=0)


def _unstack_heads(o, tq):
    return jnp.concatenate([o[g * tq:(g + 1) * tq] for g in range(NSA_G)], axis=1)


def _gate_rows(gt):
    return jnp.concatenate([gt[:, g:g + 1] for g in range(NSA_G)], axis=0)


def _attn_cmp_kernel(q_ref, kc_ref, vc_ref, bias_ref, gate_ref, o_ref, sel_ref, *idx_ref, tq, q_pos0, nc, ns, nsp,
                     n_idx):
    i = pl.program_id(2)
    ncp = kc_ref.shape[2]
    qs = _stack_heads(q_ref[...])
    logits = _dot_nt(qs, kc_ref[0, 0])
    logits = logits + jnp.concatenate([bias_ref[0, g] for g in range(NSA_G)], axis=0)
    rows = NSA_G * tq
    t4 = q_pos0 + i * tq + lax.broadcasted_iota(jnp.int32, (rows, ncp), 0) % tq
    cidx = lax.broadcasted_iota(jnp.int32, (rows, ncp), 1)
    mask = (cidx * CMP_STRIDE + (CMP_BLOCK - 1) <= t4) & (cidx < nc)
    lg = jnp.where(mask, logits, NEG)
    mx = jnp.max(lg, axis=-1, keepdims=True)
    ex = jnp.exp(lg - mx)
    p = ex / jnp.sum(ex, axis=-1, keepdims=True) * jnp.where(mask, 1.0, 0.0)
    oc = _dot(p, vc_ref[0, 0])
    o_ref[...] = _unstack_heads(oc * _gate_rows(gate_ref[0, 0, 0]), tq)

    psum = p[0:tq]
    for g in range(1, NSA_G):
        psum = psum + p[g * tq:(g + 1) * tq]
    cr = lax.broadcasted_iota(jnp.int32, (ncp, nsp), 0)
    sc = lax.broadcasted_iota(jnp.int32, (ncp, nsp), 1)
    c2s = (cr * CMP_STRIDE < sc * SEL_BLOCK + SEL_BLOCK) & (cr * CMP_STRIDE + CMP_BLOCK > sc * SEL_BLOCK)
    c2s = jnp.where(c2s & (cr < nc) & (sc < ns), 1.0, 0.0)
    imp = jnp.dot(psum, c2s, preferred_element_type=F32, precision=lax.Precision.HIGHEST)

    t = q_pos0 + i * tq + lax.broadcasted_iota(jnp.int32, (tq, nsp), 0)
    s = lax.broadcasted_iota(jnp.int32, (tq, nsp), 1)
    cur = t // SEL_BLOCK
    forced = (s == 0) | (s == cur) | (s == cur - 1)
    valid = s * SEL_BLOCK <= t
    score = jnp.where(forced, NSA_G + 1.0, jnp.where(valid, imp, -1.0))
    score = jnp.where(s < ns, score, -2.0)
    rank = jnp.zeros((tq, nsp), F32)
    for sp in range(ns):
        other = score[:, sp:sp + 1]
        ahead = (other > score) | ((other == score) & (sp < s))
        rank = rank + jnp.where(ahead, 1.0, 0.0)
    top_k = min(SEL_TOPK, ns)
    picked = (rank < top_k) & (s < ns)
    sel_ref[0, 0] = jnp.where(picked, 1.0, 0.0)
    if n_idx:
        listed = jnp.where(picked & (s < ns - 1), 1.0, 0.0)
        before = _dot(listed, jnp.where(lax.broadcasted_iota(jnp.int32, (nsp, nsp), 0)
                                        < lax.broadcasted_iota(jnp.int32, (nsp, nsp), 1), 1.0, 0.0))
        lane = lax.broadcasted_iota(jnp.int32, (tq, HD), 1)
        sf = s.astype(F32)
        out = jnp.zeros((tq, HD), F32)
        for kk in range(n_idx):
            hit = (listed > 0.5) & (before == float(kk))
            out = out + jnp.where(lane == kk, jnp.sum(jnp.where(hit, sf, 0.0), axis=-1, keepdims=True), 0.0)
        idx_ref[0][0, 0] = out.astype(jnp.int32)


def _attn_cmp(q, kcvc, bias_c, gate, *, batch, seq, tq, q_pos0, nc, ns, nsp, n_idx=0):
    nqt = seq // tq
    ncp = kcvc.shape[3]
    kern = functools.partial(_attn_cmp_kernel, tq=tq, q_pos0=q_pos0, nc=nc, ns=ns, nsp=nsp, n_idx=n_idx)
    rows_per_b = seq // tq
    extra_specs = [pl.BlockSpec((1, 1, tq, HD), lambda b, h, i: (b, h, i, 0))] if n_idx else []
    extra_shapes = [jax.ShapeDtypeStruct((batch, NSA_KV, seq, HD), jnp.int32)] if n_idx else []
    return pl.pallas_call(
        kern,
        grid=(batch, NSA_KV, nqt),
        in_specs=[
            pl.BlockSpec((tq, NSA_G * HD), lambda b, h, i: (b * rows_per_b + i, h)),
            pl.BlockSpec((None, 1, 1, ncp, HD), lambda b, h, i: (0, b, h, 0, 0)),
            pl.BlockSpec((None, 1, 1, ncp, HD), lambda b, h, i: (1, b, h, 0, 0)),
            pl.BlockSpec((1, NSA_G, tq, ncp), lambda b, h, i: (h, 0, i, 0)),
            pl.BlockSpec((1, 1, 1, tq, NSA_G), lambda b, h, i: (0, b, h, i, 0)),
        ],
        out_specs=[
            pl.BlockSpec((tq, NSA_G * HD), lambda b, h, i: (b * rows_per_b + i, h)),
            pl.BlockSpec((1, 1, tq, nsp), lambda b, h, i: (b, h, i, 0)),
        ] + extra_specs,
        out_shape=[
            jax.ShapeDtypeStruct((batch * seq, NSA_HEADS * HD), F32),
            jax.ShapeDtypeStruct((batch, NSA_KV, seq, nsp), F32),
        ] + extra_shapes,
        compiler_params=_cparams(("parallel", "parallel", "arbitrary")),
        name="attn_cmp",
    )(q, kcvc, kcvc, bias_c, gate)


def _flash_kernel(pt_ref, *refs, cfg):
    pps, tq, has_tail, use_sel = cfg["pps"], cfg["tq"], cfg["has_tail"], cfg["use_sel"]
    it = iter(refs)
    q_ref = next(it)
    k_refs = [next(it) for _ in range(pps)]
    v_refs = [next(it) for _ in range(pps)]
    b_refs = [next(it) for _ in range(pps)]
    if has_tail:
        kt_ref, vt_ref, bt_ref = next(it), next(it), next(it)
    sel_ref = next(it) if use_sel else None
    gate_ref = next(it)
    prev_ref = next(it)
    o_ref = next(it)
    m_ref, l_ref, acc_ref = next(it), next(it), next(it)

    i = pl.program_id(2)
    st = pl.program_id(3)
    rows = NSA_G * tq

    @pl.when(st == 0)
    def _():
        m_ref[...] = jnp.full((rows, 1), NEG, F32)
        l_ref[...] = jnp.zeros((rows, 1), F32)
        acc_ref[...] = jnp.zeros((rows, HD), F32)

    t0 = cfg["q_pos0"] + i * tq

    def tile(k, v, bias4, kt, p0):
        qs = _stack_heads(q_ref[...])
        s = _dot_nt(qs, k) + jnp.concatenate([bias4[g] for g in range(NSA_G)], axis=0)
        tt = t0 + lax.broadcasted_iota(jnp.int32, (tq, TK), 0)
        pos = p0 + lax.broadcasted_iota(jnp.int32, (tq, TK), 1)
        dist = tt - pos
        ok = dist >= 0
        if use_sel:
            nsp = sel_ref.shape[3]
            sr = lax.broadcasted_iota(jnp.int32, (nsp, TK), 0)
            sc = lax.broadcasted_iota(jnp.int32, (nsp, TK), 1)
            expand = jnp.where(sr == kt * (TK // SEL_BLOCK) + sc // SEL_BLOCK, 1.0, 0.0)
            picked = _dot(sel_ref[0, 0], expand)
            ok = ok & (picked > 0.5)
        else:
            ok = ok & (dist < WINDOW) & (pos >= cfg["w_pos0"])
        okf = jnp.where(ok, 1.0, 0.0)
        ok4 = jnp.concatenate([okf] * NSA_G, axis=0)
        s = jnp.where(ok4 > 0.5, s, NEG)
        m_old = m_ref[...]
        m_new = jnp.maximum(m_old, jnp.max(s, axis=-1, keepdims=True))
        alpha = jnp.exp(m_old - m_new)
        p = jnp.exp(s - m_new) * ok4
        l_ref[...] = alpha * l_ref[...] + jnp.sum(p, axis=-1, keepdims=True)
        acc_ref[...] = alpha * acc_ref[...] + _dot(p, v)
        m_ref[...] = m_new

    for pp in range(pps):
        kt = cfg["tile_of"](i, st, pp)
        active = cfg["active"](i, st, pp)
        p0 = cfg["kbase"] + kt * TK
        if active is True:
            tile(k_refs[pp][0], v_refs[pp][0], b_refs[pp][0, 0], kt, p0)
        else:
            @pl.when(active)
            def _(pp=pp, kt=kt, p0=p0):
                tile(k_refs[pp][0], v_refs[pp][0], b_refs[pp][0, 0], kt, p0)

    if has_tail:
        @pl.when(st == pl.num_programs(3) - 1)
        def _():
            tile(kt_ref[0], vt_ref[0], bt_ref[0, 0], cfg["tail_tile"], cfg["kbase"] + cfg["tail_tile"] * TK)

    @pl.when(st == pl.num_programs(3) - 1)
    def _():
        o = acc_ref[...] / l_ref[...] * _gate_rows(gate_ref[0, 0, 0])
        o_ref[...] = (prev_ref[...] + _unstack_heads(o, tq)).astype(o_ref.dtype)


def _flash(q, pages, ptab, kcol, vcol, bias_tiles, tails, sel, gate, branch, prev, *, batch, seq, tq, q_pos0,
           pps, nsteps, tile_of, active, kbase, w_pos0, npt, tail_tile, out_dtype):
    nqt = seq // tq
    has_tail = tails is not None
    use_sel = sel is not None
    n_delta = bias_tiles.shape[1]
    cfg = dict(pps=pps, tq=tq, has_tail=has_tail, use_sel=use_sel, q_pos0=q_pos0, tile_of=tile_of,
               active=active, kbase=kbase, w_pos0=w_pos0, tail_tile=tail_tile)

    def page_idx(b, i, s, pp, pt):
        kt = jnp.clip(tile_of(i, s, pp), 0, npt - 1)
        return pt[b * npt + kt]

    def didx(i, s, pp):
        kt = tile_of(i, s, pp)
        return jnp.clip((q_pos0 + i * tq - kbase - kt * TK) // TK, 0, n_delta - 1)

    in_specs = [pl.BlockSpec((tq, NSA_G * HD), lambda b, h, i, s, pt: (b * nqt + i, h))]
    args = [q]
    for col in (kcol, vcol):
        for pp in range(pps):
            in_specs.append(pl.BlockSpec(
                (1, TK, HD), lambda b, h, i, s, pt, pp=pp, col=col: (page_idx(b, i, s, pp, pt), 0, col + h)))
            args.append(pages)
    for pp in range(pps):
        in_specs.append(pl.BlockSpec(
            (1, 1, NSA_G, tq, TK), lambda b, h, i, s, pt, pp=pp: (h, didx(i, s, pp), 0, 0, 0)))
        args.append(bias_tiles)
    if has_tail:
        tail_pages, tkcol, tvcol = tails
        tdelta = min(max((q_pos0 - kbase - tail_tile * TK) // TK, 0), n_delta - 1)
        in_specs.append(pl.BlockSpec((1, TK, HD), lambda b, h, i, s, pt: (b, 0, tkcol + h)))
        in_specs.append(pl.BlockSpec((1, TK, HD), lambda b, h, i, s, pt: (b, 0, tvcol + h)))
        in_specs.append(pl.BlockSpec((1, 1, NSA_G, tq, TK), lambda b, h, i, s, pt: (h, tdelta, 0, 0, 0)))
        args += [tail_pages, tail_pages, bias_tiles]
    if use_sel:
        nsp = sel.shape[3]
        in_specs.append(pl.BlockSpec((1, 1, tq, nsp), lambda b, h, i, s, pt: (b, h, i, 0)))
        args.append(sel)
    in_specs.append(pl.BlockSpec((1, 1, 1, tq, NSA_G), lambda b, h, i, s, pt: (branch, b, h, i, 0)))
    args.append(gate)
    in_specs.append(pl.BlockSpec((tq, NSA_G * HD), lambda b, h, i, s, pt: (b * nqt + i, h)))
    args.append(prev)

    rows = NSA_G * tq
    grid_spec = pltpu.PrefetchScalarGridSpec(
        num_scalar_prefetch=1,
        grid=(batch, NSA_KV, nqt, nsteps),
        in_specs=in_specs,
        out_specs=pl.BlockSpec((tq, NSA_G * HD), lambda b, h, i, s, pt: (b * nqt + i, h)),
        scratch_shapes=[pltpu.VMEM((rows, 1), F32), pltpu.VMEM((rows, 1), F32), pltpu.VMEM((rows, HD), F32)],
    )
    return pl.pallas_call(
        functools.partial(_flash_kernel, cfg=cfg),
        grid_spec=grid_spec,
        out_shape=jax.ShapeDtypeStruct((batch * seq, NSA_HEADS * HD), out_dtype),
        compiler_params=_cparams(("parallel", "parallel", "arbitrary", "arbitrary")),
    )(ptab, *args)


def _selg_kernel(idx_ref, pt_ref, q_ref, *refs, nblk, tq, q_pos0, tail_pos0):
    kv_refs = refs[:nblk]
    b_refs = refs[nblk:2 * nblk]
    tk_ref, tv_ref, tb_ref, gate_ref, prev_ref, o_ref, osc_ref = refs[2 * nblk:]
    b = pl.program_id(0)
    h = pl.program_id(1)
    qi = pl.program_id(2)
    nq = pl.num_programs(2)
    rows = NSA_G * tq
    nch = 2 * NSA_KV

    @pl.when(qi == 0)
    def _():
        osc_ref[...] = jnp.zeros((rows, HD), F32)

    qs = _stack_heads(q_ref[...])
    t = q_pos0 + qi
    base = ((b * NSA_KV + h) * nq + qi) * nblk
    jj = lax.broadcasted_iota(jnp.int32, (rows, SEL_BLOCK), 1)

    def bias_rows(bref):
        bb = bref[0, 0, 0]
        return jnp.concatenate([jnp.broadcast_to(bb[g:g + 1, :], (tq, SEL_BLOCK)) for g in range(NSA_G)], axis=0)

    scores, vals = [], []
    for n in range(nblk):
        k = kv_refs[n][pl.ds(h, SEL_BLOCK, stride=nch), :]
        v = kv_refs[n][pl.ds(NSA_KV + h, SEL_BLOCK, stride=nch), :]
        s = _dot_nt(qs, k) + bias_rows(b_refs[n])
        pos = idx_ref[base + n] * SEL_BLOCK + jj
        scores.append(jnp.where(pos <= t, s, NEG))
        vals.append(v)
    s = _dot_nt(qs, tk_ref[0]) + bias_rows(tb_ref)
    scores.append(jnp.where(tail_pos0 + jj <= t, s, NEG))
    vals.append(tv_ref[0])

    m = functools.reduce(jnp.maximum, [jnp.max(s, axis=-1, keepdims=True) for s in scores])
    l = jnp.zeros((rows, 1), F32)
    acc = jnp.zeros((rows, HD), F32)
    for s, v in zip(scores, vals):
        p = jnp.exp(s - m)
        l = l + jnp.sum(p, axis=-1, keepdims=True)
        acc = acc + _dot(p, v)
    rowq = lax.broadcasted_iota(jnp.int32, (rows, 1), 0) % tq
    osc = jnp.where(rowq == qi, acc / l, osc_ref[...])
    osc_ref[...] = osc

    @pl.when(qi == nq - 1)
    def _():
        o_ref[...] = prev_ref[...] + _unstack_heads(osc * _gate_rows(gate_ref[0, 0, 0]), tq)


def _sel_gather(q, cache_rows, ptab, idx, bias_blk, tail, gate, prev, *, batch, tq, nq, q_pos0, npt, ns):
    nblk = idx.shape[0] // (batch * NSA_KV * nq)
    half = SEL_BLOCK * 2 * NSA_KV
    per_page = PAGE // SEL_BLOCK

    def blk(b, h, qi, n, idx_ref):
        return idx_ref[((b * NSA_KV + h) * nq + qi) * nblk + n]

    def kv_map(n):
        def f(b, h, qi, idx_ref, pt_ref):
            s = blk(b, h, qi, n, idx_ref)
            return (pt_ref[b * npt + s // per_page] * per_page + s % per_page, 0)
        return f

    in_specs = [pl.BlockSpec((tq, NSA_G * HD), lambda b, h, qi, i_, p_: (b, h))]
    in_specs += [pl.BlockSpec((half, HD), kv_map(n)) for n in range(nblk)]
    in_specs += [pl.BlockSpec((1, 1, 1, NSA_G, SEL_BLOCK),
                              lambda b, h, qi, i_, p_, n=n: (qi, h, blk(b, h, qi, n, i_), 0, 0)) for n in range(nblk)]
    in_specs += [
        pl.BlockSpec((1, SEL_BLOCK, HD), lambda b, h, qi, i_, p_: (b, 0, h)),
        pl.BlockSpec((1, SEL_BLOCK, HD), lambda b, h, qi, i_, p_: (b, 0, NSA_KV + h)),
        pl.BlockSpec((1, 1, 1, NSA_G, SEL_BLOCK), lambda b, h, qi, i_, p_: (qi, h, ns - 1, 0, 0)),
        pl.BlockSpec((1, 1, 1, tq, NSA_G), lambda b, h, qi, i_, p_: (1, b, h, 0, 0)),
        pl.BlockSpec((tq, NSA_G * HD), lambda b, h, qi, i_, p_: (b, h)),
    ]
    grid_spec = pltpu.PrefetchScalarGridSpec(
        num_scalar_prefetch=2,
        grid=(batch, NSA_KV, nq),
        in_specs=in_specs,
        out_specs=pl.BlockSpec((tq, NSA_G * HD), lambda b, h, qi, i_, p_: (b, h)),
        scratch_shapes=[pltpu.VMEM((NSA_G * tq, HD), F32)],
    )
    kern = functools.partial(_selg_kernel, nblk=nblk, tq=tq, q_pos0=q_pos0, tail_pos0=(ns - 1) * SEL_BLOCK)
    return pl.pallas_call(
        kern,
        grid_spec=grid_spec,
        out_shape=jax.ShapeDtypeStruct((batch * tq, NSA_HEADS * HD), F32),
        compiler_params=_cparams(("parallel", "parallel", "arbitrary")),
        name="sel_gather",
    )(idx, ptab, q, *([cache_rows] * nblk), *([bias_blk] * nblk), tail, tail, bias_blk, gate, prev)


def _rel_bucket(dist):
    n = jnp.maximum(dist, 0)
    max_exact = REL_BUCKETS // 2
    nf = jnp.maximum(n, 1).astype(F32)
    large = max_exact + (jnp.log(nf / max_exact) / math.log(REL_MAX_DIST / max_exact)
                         * (REL_BUCKETS - max_exact)).astype(jnp.int32)
    return jnp.where(n < max_exact, n, jnp.minimum(large, REL_BUCKETS - 1))


def _bias_by_distance(rel_bias):
    return rel_bias.astype(F32)[_rel_bucket(jnp.arange(REL_MAX_DIST))]


def _windows(table, starts, width):
    cols = table.shape[1]
    return jax.vmap(lambda s: lax.dynamic_slice(table, (s, 0), (width, cols)))(starts)


def _bias_tiles(rel_bias, tq, n_delta):
    fd = _bias_by_distance(rel_bias)
    top = TK * (n_delta - 1) + tq - 1
    rev = fd[jnp.clip(top - jnp.arange(top + TK), 0, REL_MAX_DIST - 1)]
    starts = (top - TK * jnp.arange(n_delta)[:, None] - jnp.arange(tq)[None, :]).reshape(-1)
    t = _windows(rev, starts, TK)
    return t.reshape(n_delta, tq, TK, NSA_KV, NSA_G).transpose(3, 0, 4, 1, 2)


def _bias_cmp(rel_bias, q_pos0, tqs, ncp):
    fd = _bias_by_distance(rel_bias)
    u = q_pos0 + jnp.arange(tqs) - (CMP_BLOCK - 1)
    a, res = u // CMP_STRIDE, u % CMP_STRIDE
    a_top = (q_pos0 + tqs - 1 - (CMP_BLOCK - 1)) // CMP_STRIDE
    a_low = (q_pos0 - (CMP_BLOCK - 1)) // CMP_STRIDE
    length = a_top - a_low + ncp
    dist = CMP_STRIDE * (a_top - jnp.arange(length))[None, :] + jnp.arange(CMP_STRIDE)[:, None]
    rev = fd[jnp.clip(dist, 0, REL_MAX_DIST - 1)].reshape(CMP_STRIDE * length, NSA_HEADS)
    b = _windows(rev, res * length + (a_top - a), ncp)
    return b.reshape(tqs, ncp, NSA_KV, NSA_G).transpose(2, 3, 0, 1)


def _bias_blocks(rel_bias, q_pos0, nq, ns):
    fd = _bias_by_distance(rel_bias)
    top = q_pos0 + nq - 1
    rev = fd[jnp.clip(top - jnp.arange(nq + ns * SEL_BLOCK), 0, REL_MAX_DIST - 1)]
    starts = (top - q_pos0 - jnp.arange(nq)[:, None] + SEL_BLOCK * jnp.arange(ns)[None, :]).reshape(-1)
    t = _windows(rev, starts, SEL_BLOCK)
    return t.reshape(nq, ns, SEL_BLOCK, NSA_KV, NSA_G).transpose(0, 3, 1, 4, 2)


def _forward_group(x, conv0, ssm0, past, P, *, batch, seq, t_valid, q_pos0, tq):
    M = batch * seq
    conv_out, ssm_out = [], []
    for l in range(2):
        proj = _nmm(x, P["mix_norm"][l], P["gdn_w_main"][l], tn=512)
        bg = _nmm(x, P["mix_norm"][l], P["gdn_w_gate"][l], tn=128, mode="gdn_gate", aux=P["gdn_gate_aux"][l],
                  seq=seq, t_valid=t_valid)
        o, s_new = _gdn(proj, bg, P["gdn_conv_w"][l], conv0[l], ssm0[l], P["gdn_out_norm"][l],
                        batch=batch, seq=seq)
        conv_out.append(proj.reshape(batch, seq, -1)[:, t_valid - (GDN_CONV - 1):t_valid, :GDN_CONV_DIM])
        ssm_out.append(s_new)
        x = _mm_res(o, P["gdn_w_out"][l], x)
        x = _mlp(x, P["mlp_norm"][l], P["mlp_w1"][l], P["mlp_w2"][l])
    x, cmp_rows, sel_rows, win_state = _nsa_layers(x, past, P, batch=batch, seq=seq, t_valid=t_valid,
                                                   q_pos0=q_pos0, tq=tq)
    return x, jnp.stack(conv_out), jnp.stack(ssm_out), cmp_rows, sel_rows, win_state


def _nsa_layers(x, past, P, *, batch, seq, t_valid, q_pos0, tq):
    M = batch * seq
    kv = _nmm(x, P["kv_norm"], P["nsa_w_kv"], tn=512, mode="headnorm", aux=P["kv_aux"], norm_tiles=(2, 4))
    kv3 = kv.reshape(batch, seq, 6 * NSA_KV * HD)
    new_rows = kv3[:, :t_valid]
    cmp_rows = new_rows[..., 0:1024].reshape(batch, t_valid, 2, NSA_KV, HD)
    sel_rows = new_rows[..., 1024:2048].reshape(batch, t_valid, 2, NSA_KV, HD)
    win_new = new_rows[..., 2048:3072]

    ident = jnp.arange(M // PAGE, dtype=jnp.int32) if seq % PAGE == 0 else None
    if past is None:
        n_tot = t_valid
        npages = seq // PAGE
        kv_pages = kv.reshape(M // PAGE, PAGE, 6 * NSA_KV * HD)
        first = _cmp_stage1(kv_pages, ident, P["cmp_w1cat"], row_packed=False)
        nsb = npages // CMP_PPS
        nseg_tot = npages * (PAGE // CMP_STRIDE)
        f6 = first.reshape(batch, nsb, 2, NSA_KV, CMP_PPS * 8, 2, HD).transpose(2, 0, 3, 1, 4, 5, 6)
        f6 = f6.reshape(2, batch, NSA_KV, nseg_tot, 2, HD)
        win_seq = win_new
        w_pos0 = 0
    else:
        n_past = past["page_table"].shape[1] * PAGE
        n_tot = n_past + t_valid
        npages = n_past // PAGE
        ptab = past["page_table"].reshape(-1)
        first = _cmp_stage1(past["cmp_rows"], ptab, P["cmp_w1cat"], row_packed=True)
        nsb = npages // CMP_PPS
        f6 = first.reshape(batch, nsb, 2, NSA_KV, CMP_PPS * 8, 2, HD).transpose(2, 0, 3, 1, 4, 5, 6)
        f6 = f6.reshape(2, batch, NSA_KV, npages * 8, 2, HD)
        tail_cmp = jnp.pad(new_rows[..., 0:1024], ((0, 0), (0, PAGE - t_valid), (0, 0)))
        tail_cmp = jnp.pad(tail_cmp, ((0, (-batch) % CMP_PPS), (0, 0), (0, 0)))
        tfirst = _cmp_stage1(tail_cmp, jnp.arange(tail_cmp.shape[0], dtype=jnp.int32), P["cmp_w1cat"],
                             row_packed=False)
        t6 = tfirst.reshape(-1, 2, NSA_KV, CMP_PPS, 8, 2, HD).transpose(1, 0, 3, 2, 4, 5, 6)
        t6 = t6.reshape(2, -1, NSA_KV, 8, 2, HD)[:, :batch, :, :(-(-t_valid // CMP_STRIDE))]
        f6 = jnp.concatenate([f6, t6], axis=3)
        nseg_tot = f6.shape[3]
        win_seq = jnp.concatenate([past["win"], win_new], axis=1)
        w_pos0 = q_pos0 + t_valid - win_seq.shape[1]
    nc = -(-n_tot // CMP_STRIDE) - 1
    ns = -(-n_tot // SEL_BLOCK)
    ncp = -(-nc // 128) * 128
    nsp = -(-ns // 128) * 128
    a = f6[:, :, :, 0:nc, 0, :]
    b = f6[:, :, :, 1:nc + 1, 1, :]
    if b.shape[3] < nc:
        b = jnp.pad(b, ((0, 0), (0, 0), (0, 0), (0, nc - b.shape[3]), (0, 0)))
    a = jnp.pad(a, ((0, 0), (0, 0), (0, 0), (0, ncp - nc), (0, 0))).reshape(2, batch * NSA_KV * ncp, HD)
    b = jnp.pad(b, ((0, 0), (0, 0), (0, 0), (0, ncp - nc), (0, 0))).reshape(2, batch * NSA_KV * ncp, HD)
    R = batch * NSA_KV * ncp
    kcvc = _cmp_stage2(a, b, P["cmp_pe8"], P["cmp_w1flat"], P["cmp_b1"], P["cmp_w2"], P["cmp_b2"],
                       P["k_cmp_norm"], tr=min(R, 2048))
    kcvc = kcvc.reshape(2, batch, NSA_KV, ncp, HD)

    n_keep = min(WINDOW, win_seq.shape[1])
    win_state = win_seq[:, win_seq.shape[1] - n_keep:].reshape(batch, n_keep, 2, NSA_KV, HD)

    seq_q = seq if past is None else tq
    bias_c = _bias_cmp(P["rel_bias"], q_pos0, seq_q, ncp)
    btiles = _bias_tiles(P["rel_bias"], tq, min(N_DELTA, (q_pos0 + seq_q) // TK + 1))
    if past is None:
        sel_pages, sel_ptab, sel_npt = kv_pages, ident, seq // PAGE
        sel_kcol, sel_vcol = 8, 12
        sel_tails = None
        sel_pps = 4
        sel_steps = -(-sel_npt // sel_pps)
        sel_tile_of = lambda i, s, pp: s * sel_pps + pp
        sel_active = lambda i, s, pp: (s * sel_pps + pp) * TK <= i * tq + tq - 1
        win_pages, win_ptab, win_npt = kv_pages, ident, seq // PAGE
        win_kcol, win_vcol = 16, 20
        win_pps = WINDOW // TK + tq // TK
        win_tile_of = lambda i, s, pp: (i * tq) // TK - WINDOW // TK + pp
        win_active = lambda i, s, pp: (i * tq) // TK - WINDOW // TK + pp >= 0
        win_kbase = 0
    else:
        assert n_past % SEL_BLOCK == 0 and t_valid <= SEL_BLOCK and ns - 1 > SEL_TOPK
        tail_sel = jnp.pad(new_rows[..., 1024:2048], ((0, 0), (0, SEL_BLOCK - t_valid), (0, 0)))
        bias_blk = _bias_blocks(P["rel_bias"], q_pos0, t_valid, ns)
        nwt = -(-win_seq.shape[1] // TK)
        win_pages = jnp.pad(win_seq, ((0, 0), (0, nwt * TK - win_seq.shape[1]), (0, 0)))
        win_pages = win_pages.reshape(batch * nwt, TK, 2 * NSA_KV * HD)
        win_ptab, win_npt = jnp.arange(batch * nwt, dtype=jnp.int32), nwt
        win_kcol, win_vcol = 0, 4
        win_pps = nwt
        win_tile_of = lambda i, s, pp: pp
        win_active = lambda i, s, pp: True
        win_kbase = w_pos0

    for jj in range(2):
        l = 2 + jj
        q = _nmm(x, P["mix_norm"][l], P["nsa_w_q"][jj], tn=512, out_dtype=BF16, mode="headnorm",
                 aux=P["nsa_q_aux"][jj], scale=HD ** -0.5)
        gates = _nmm(x, P["mix_norm"][l], P["nsa_w_g"][jj], tn=128, mode="sigmoid")
        gate = gates[:, :NSA_HEADS * 3].reshape(batch, seq, NSA_KV, NSA_G, 3).transpose(4, 0, 2, 1, 3)
        if seq_q != seq:
            q = q.reshape(batch, seq, -1)[:, :seq_q].reshape(batch * seq_q, -1)
            gate = gate[:, :, :, :seq_q]
        if past is None:
            o_c, sel = _attn_cmp(q, kcvc, bias_c, gate, batch=batch, seq=seq_q, tq=tq, q_pos0=q_pos0, nc=nc,
                                 ns=ns, nsp=nsp)
            o_s = _flash(q, sel_pages, sel_ptab, sel_kcol, sel_vcol, btiles, sel_tails, sel, gate, 1, o_c,
                         batch=batch, seq=seq_q, tq=tq, q_pos0=q_pos0, pps=sel_pps,
                         nsteps=sel_steps, tile_of=sel_tile_of, active=sel_active, kbase=0, w_pos0=0,
                         npt=sel_npt, tail_tile=sel_npt, out_dtype=F32)
        else:
            o_c, sel, idx = _attn_cmp(q, kcvc, bias_c, gate, batch=batch, seq=seq_q, tq=tq, q_pos0=q_pos0,
                                      nc=nc, ns=ns, nsp=nsp, n_idx=SEL_TOPK - 1)
            o_s = _sel_gather(q, past["sel_rows"], past["page_table"].reshape(-1),
                              idx[:, :, :t_valid, :SEL_TOPK - 1].reshape(-1), bias_blk, tail_sel, gate, o_c,
                              batch=batch, tq=tq, nq=t_valid, q_pos0=q_pos0, npt=npages, ns=ns)
        o_w = _flash(q, win_pages, win_ptab, win_kcol, win_vcol, btiles, None, None, gate, 2, o_s,
                     batch=batch, seq=seq_q, tq=tq, q_pos0=q_pos0, pps=win_pps,
                     nsteps=1, tile_of=win_tile_of, active=win_active, kbase=win_kbase, w_pos0=w_pos0,
                     npt=win_npt, tail_tile=0, out_dtype=BF16 if tq % 16 == 0 else F32)
        if seq_q != seq:
            o_w = jnp.pad(o_w.reshape(batch, seq_q, -1), ((0, 0), (0, seq - seq_q), (0, 0))).reshape(M, -1)
        x = _mm_res(o_w.astype(BF16), P["nsa_w_out"][jj], x)
        x = _mlp(x, P["mlp_norm"][l], P["mlp_w1"][l], P["mlp_w2"][l])
    return x, cmp_rows, sel_rows, win_state


def _prepare_params(mix_norm, mlp_norm, mlp_w1, mlp_w2, gdn_w_in, gdn_conv_w, gdn_a_log, gdn_dt_bias,
                    gdn_out_norm, gdn_w_out, kv_norm, nsa_w_kv, k_sel_norm, k_win_norm, k_cmp_norm, cmp_pe,
                    cmp_w1, cmp_b1, cmp_w2, cmp_b2, nsa_w_in, nsa_q_norm, nsa_w_out, rel_bias):
    n_lay = gdn_w_in.shape[0]
    main = GDN_CONV_DIM + GDN_VAL_DIM
    zpad = lambda n: jnp.zeros((1, n), F32)
    gate_aux = jnp.stack([
        jnp.concatenate([
            jnp.concatenate([zpad(GDN_V_HEADS), gdn_a_log[l][None].astype(F32), zpad(HD - 2 * GDN_V_HEADS)], 1),
            jnp.concatenate([zpad(GDN_V_HEADS), gdn_dt_bias[l][None].astype(F32), zpad(HD - 2 * GDN_V_HEADS)], 1),
        ], 0)[None] for l in range(n_lay)])
    tile4 = lambda w: jnp.tile(w.astype(F32), NSA_KV)[None, None]
    kv_aux = jnp.concatenate([jnp.ones((2, 1, 512), F32), tile4(k_sel_norm), jnp.ones((1, 1, 512), F32),
                              tile4(k_win_norm), jnp.ones((1, 1, 512), F32)], 0)
    nq = NSA_HEADS * HD
    w1r = cmp_w1.reshape(2, 2, CMP_STRIDE, HD, HD)
    P = dict(
        mix_norm=mix_norm, mlp_norm=mlp_norm,
        mlp_w1=mlp_w1.astype(BF16), mlp_w2=mlp_w2.astype(BF16),
        gdn_w_main=gdn_w_in[:, :, :main].astype(BF16),
        gdn_w_gate=jnp.pad(gdn_w_in[:, :, main:], ((0, 0), (0, 0), (0, HD - 2 * GDN_V_HEADS))).astype(BF16),
        gdn_gate_aux=gate_aux, gdn_conv_w=gdn_conv_w, gdn_out_norm=gdn_out_norm,
        gdn_w_out=gdn_w_out.astype(BF16),
        kv_norm=kv_norm, nsa_w_kv=nsa_w_kv.astype(BF16), kv_aux=kv_aux, k_cmp_norm=k_cmp_norm,
        cmp_w1cat=jnp.concatenate([w1r[:, 0], w1r[:, 1]], axis=-1).astype(BF16),
        cmp_w1flat=cmp_w1.reshape(2, CMP_BLOCK * HD, HD).astype(BF16),
        cmp_pe8=jnp.pad(cmp_pe.reshape(2, 1, CMP_BLOCK * HD), ((0, 0), (0, 7), (0, 0))),
        cmp_b1=cmp_b1, cmp_w2=cmp_w2.astype(BF16), cmp_b2=cmp_b2,
        nsa_w_q=nsa_w_in[:, :, :nq].astype(BF16),
        nsa_w_g=jnp.pad(nsa_w_in[:, :, nq:], ((0, 0), (0, 0), (0, HD - 3 * NSA_HEADS))).astype(BF16),
        nsa_q_aux=jnp.stack([jnp.tile(tile4(nsa_q_norm[jj]), (nq // 512, 1, 1)) for jj in range(2)]),
        nsa_w_out=nsa_w_out.astype(BF16), rel_bias=rel_bias,
    )
    return P


def kernel(x_prompt, x_sample, state_conv, state_ssm, cache_cmp, cache_sel, cache_win, page_table, mix_norm,
           mlp_norm, mlp_w1, mlp_w2, gdn_w_in, gdn_conv_w, gdn_a_log, gdn_dt_bias, gdn_out_norm, gdn_w_out,
           kv_norm, nsa_w_kv, k_sel_norm, k_win_norm, k_cmp_norm, cmp_pe, cmp_w1, cmp_b1, cmp_w2, cmp_b2,
           nsa_w_in, nsa_q_norm, nsa_w_out, rel_bias):
    bp, tp, _ = x_prompt.shape
    bs, ts, _ = x_sample.shape
    n_lay = gdn_w_in.shape[0]
    P = _prepare_params(mix_norm, mlp_norm, mlp_w1, mlp_w2, gdn_w_in, gdn_conv_w, gdn_a_log, gdn_dt_bias,
                        gdn_out_norm, gdn_w_out, kv_norm, nsa_w_kv, k_sel_norm, k_win_norm, k_cmp_norm, cmp_pe,
                        cmp_w1, cmp_b1, cmp_w2, cmp_b2, nsa_w_in, nsa_q_norm, nsa_w_out, rel_bias)

    conv0 =jnp.zeros((n_lay, bp, GDN_CONV - 1, GDN_CONV_DIM), F32)
    ssm0 = jnp.zeros((n_lay, bp, GDN_V_HEADS, HD, HD), F32)
    yp, conv_p, ssm_p, cmp_p, sel_p, win_p = _forward_group(
        x_prompt.reshape(bp * tp, D_MODEL), conv0, ssm0, None, P,
        batch=bp, seq=tp, t_valid=tp, q_pos0=0, tq=128)

    seq_s = GDN_CHUNK
    xs = jnp.pad(x_sample, ((0, 0), (0, seq_s - ts), (0, 0))).reshape(bs * seq_s, D_MODEL)
    n_pool = cache_cmp.shape[0]
    past = dict(cmp_rows=cache_cmp.reshape(n_pool * PAGE * 2 * NSA_KV, HD),
                sel_rows=cache_sel.reshape(n_pool * PAGE * 2 * NSA_KV, HD),
                page_table=page_table.astype(jnp.int32),
                win=cache_win.reshape(bs, cache_win.shape[1], 2 * NSA_KV * HD))
    n_past = page_table.shape[1] * PAGE
    ys, conv_s, ssm_s, cmp_s, sel_s, win_s = _forward_group(
        xs, state_conv, state_ssm, past, P, batch=bs, seq=seq_s, t_valid=ts, q_pos0=n_past, tq=8)
    y_sample = ys.reshape(bs, seq_s, D_MODEL)[:, :ts]
    return (yp.reshape(bp, tp, D_MODEL), y_sample, conv_p, ssm_p, cmp_p, sel_p, win_p,
            conv_s, ssm_s, cmp_s, sel_s, win_s)
```

```python
import functools
import math

import jax
import jax.numpy as jnp
from jax import lax
from jax.experimental import pallas as pl
from jax.experimental.pallas import tpu as pltpu

F32 = jnp.float32
BF16 = jnp.bfloat16

D_MODEL = 2048
D_FF = 4 * D_MODEL
NORM_EPS = 1e-6
L2_EPS = 1e-6
PAGE = 128

HD = 128
GDN_QK_HEADS = 16
GDN_V_HEADS = 32
GDN_KEY_DIM = GDN_QK_HEADS * HD
GDN_VAL_DIM = GDN_V_HEADS * HD
GDN_CONV = 4
GDN_CHUNK = 64
GDN_CONV_DIM = 2 * GDN_KEY_DIM + GDN_VAL_DIM

NSA_HEADS = 16
NSA_KV = 4
NSA_G = NSA_HEADS // NSA_KV
CMP_BLOCK = 32
CMP_STRIDE = 16
SEL_BLOCK = 64
SEL_TOPK = 16
WINDOW = 512
REL_BUCKETS = 32
REL_MAX_DIST = 4096
NEG = -1e30

TK = 128
N_DELTA = REL_MAX_DIST // TK + 2

VMEM_LIMIT = 56 * 1024 * 1024


def _cparams(sem):
    return pltpu.CompilerParams(dimension_semantics=sem, vmem_limit_bytes=VMEM_LIMIT)


def _sigmoid(x):
    return 1.0 / (1.0 + jnp.exp(-x))


def _softplus(x):
    return jnp.maximum(x, 0.0) + jnp.log(1.0 + jnp.exp(-jnp.abs(x)))


def _dot(a, b):
    return jnp.dot(a.astype(BF16), b.astype(BF16), preferred_element_type=F32)


def _dot_nt(a, b):
    return lax.dot_general(a.astype(BF16), b.astype(BF16), (((1,), (1,)), ((), ())),
                           preferred_element_type=F32)


def _dot_tn(a, b):
    return lax.dot_general(a.astype(BF16), b.astype(BF16), (((0,), (0,)), ((), ())),
                           preferred_element_type=F32)


def _headnorm(acc, gw):
    parts = []
    for g in range(acc.shape[1] // HD):
        a = acc[:, g * HD:(g + 1) * HD]
        parts.append(a * lax.rsqrt(jnp.mean(a * a, axis=-1, keepdims=True) + NORM_EPS))
    return jnp.concatenate(parts, axis=1) * gw


def _nmm_kernel(x_ref, nw_ref, w_ref, aux_ref, o_ref, h_ref, *, mode, norm_tiles, scale, seq, t_valid):
    i = pl.program_id(0)
    j = pl.program_id(1)

    @pl.when(j == 0)
    def _():
        x = x_ref[...]
        h = x * lax.rsqrt(jnp.mean(x * x, axis=-1, keepdims=True) + NORM_EPS) * nw_ref[...]
        h_ref[...] = h.astype(BF16)

    acc = jnp.dot(h_ref[...], w_ref[...], preferred_element_type=F32)
    if mode == "plain":
        o_ref[...] = acc.astype(o_ref.dtype)
    elif mode == "headnorm":
        if norm_tiles is None:
            o_ref[...] = (_headnorm(acc, aux_ref[0]) * scale).astype(o_ref.dtype)
        else:
            is_n = functools.reduce(jnp.logical_or, [j == t for t in norm_tiles])

            @pl.when(is_n)
            def _():
                o_ref[...] = (_headnorm(acc, aux_ref[0]) * scale).astype(o_ref.dtype)

            @pl.when(jnp.logical_not(is_n))
            def _():
                o_ref[...] = acc.astype(o_ref.dtype)
    elif mode == "sigmoid":
        o_ref[...] = _sigmoid(acc)
    elif mode == "gdn_gate":
        tm = acc.shape[0]
        aux = aux_ref[0]
        lane = lax.broadcasted_iota(jnp.int32, acc.shape, 1)
        row = lax.broadcasted_iota(jnp.int32, acc.shape, 0) + i * tm
        live = (row % seq) < t_valid
        beta = jnp.where(live, _sigmoid(acc), 0.0)
        g = jnp.where(live, -jnp.exp(aux[0:1, :]) * _softplus(acc + aux[1:2, :]), 0.0)
        g = jnp.where((lane >= GDN_V_HEADS) & (lane < 2 * GDN_V_HEADS), g, 0.0)
        r = lax.broadcasted_iota(jnp.int32, (tm, tm), 0)
        c = lax.broadcasted_iota(jnp.int32, (tm, tm), 1)
        tri = ((r // GDN_CHUNK) == (c // GDN_CHUNK)) & (c <= r)
        gcum = jnp.dot(jnp.where(tri, 1.0, 0.0), g, preferred_element_type=F32,
                       precision=lax.Precision.HIGHEST)
        o_ref[...] = jnp.where(lane < GDN_V_HEADS, beta, gcum)
    else:
        raise ValueError(mode)


def _nmm(x, nw, w, *, tn, out_dtype=F32, mode="plain", aux=None, norm_tiles=None, scale=1.0,
         seq=1, t_valid=1, tm=512):
    M, K = x.shape
    N = w.shape[1]
    tm = min(tm, M)
    assert M % tm == 0 and N % tn == 0
    if aux is None:
        aux = jnp.zeros((N // tn, 1, tn), F32)
    kern = functools.partial(_nmm_kernel, mode=mode, norm_tiles=norm_tiles, scale=scale, seq=seq,
                             t_valid=t_valid)
    return pl.pallas_call(
        kern,
        grid=(M // tm, N // tn),
        in_specs=[
            pl.BlockSpec((tm, K), lambda i, j: (i, 0)),
            pl.BlockSpec((1, K), lambda i, j: (0, 0)),
            pl.BlockSpec((K, tn), lambda i, j: (0, j)),
            pl.BlockSpec((1,) + aux.shape[1:], lambda i, j: (j, 0, 0)),
        ],
        out_specs=pl.BlockSpec((tm, tn), lambda i, j: (i, j)),
        out_shape=jax.ShapeDtypeStruct((M, N), out_dtype),
        scratch_shapes=[pltpu.VMEM((tm, K), BF16)],
        compiler_params=_cparams(("parallel", "arbitrary")),
        name="nmm_" + mode,
    )(x, nw.reshape(1, K), w, aux)


def _mmres_kernel(x_ref, w_ref, r_ref, o_ref):
    o_ref[...] = r_ref[...] + jnp.dot(x_ref[...], w_ref[...], preferred_element_type=F32)


def _mm_res(x, w, res, *, tm=512, tn=512):
    M, K = x.shape
    N = w.shape[1]
    tm = min(tm, M)
    return pl.pallas_call(
        _mmres_kernel,
        grid=(M // tm, N // tn),
        in_specs=[
            pl.BlockSpec((tm, K), lambda i, j: (i, 0)),
            pl.BlockSpec((K, tn), lambda i, j: (0, j)),
            pl.BlockSpec((tm, tn), lambda i, j: (i, j)),
        ],
        out_specs=pl.BlockSpec((tm, tn), lambda i, j: (i, j)),
        out_shape=jax.ShapeDtypeStruct((M, N), F32),
        compiler_params=_cparams(("parallel", "arbitrary")),
        name="mm_res",
    )(x, w, res)


def _mlp_kernel(x_ref, nw_ref, w1_ref, w2_ref, o_ref, h_ref, acc_ref):
    f = pl.program_id(1)

    @pl.when(f == 0)
    def _():
        x = x_ref[...]
        h = x * lax.rsqrt(jnp.mean(x * x, axis=-1, keepdims=True) + NORM_EPS) * nw_ref[...]
        h_ref[...] = h.astype(BF16)
        acc_ref[...] = x

    a = jnp.maximum(jnp.dot(h_ref[...], w1_ref[...], preferred_element_type=F32), 0.0)
    acc_ref[...] += jnp.dot((a * a).astype(BF16), w2_ref[...], preferred_element_type=F32)

    @pl.when(f == pl.num_programs(1) - 1)
    def _():
        o_ref[...] = acc_ref[...]


def _mlp(x, nw, w1, w2, *, tm=512, tf=512):
    M, D = x.shape
    Fdim = w1.shape[1]
    tm = min(tm, M)
    return pl.pallas_call(
        _mlp_kernel,
        grid=(M // tm, Fdim // tf),
        in_specs=[
            pl.BlockSpec((tm, D), lambda i, f: (i, 0)),
            pl.BlockSpec((1, D), lambda i, f: (0, 0)),
            pl.BlockSpec((D, tf), lambda i, f: (0, f)),
            pl.BlockSpec((tf, D), lambda i, f: (f, 0)),
        ],
        out_specs=pl.BlockSpec((tm, D), lambda i, f: (i, 0)),
        out_shape=jax.ShapeDtypeStruct((M, D), F32),
        scratch_shapes=[pltpu.VMEM((tm, D), BF16), pltpu.VMEM((tm, D), F32)],
        compiler_params=_cparams(("parallel", "arbitrary")),
        name="mlp",
    )(x, nw.reshape(1, D), w1, w2)


def _unit_lower_inverse(mats, r, c):
    eye = jnp.where(r == c, 1.0, 0.0)
    in8 = (r // 8) == (c // 8)
    d0 = [jnp.where(in8, a, 0.0) for a in mats]
    d2 = [_dot(d, d) for d in d0]
    d4 = [_dot(d, d) for d in d2]
    x = [_dot(eye - a, eye + b) for a, b in zip(d0, d2)]
    x = [_dot(a, eye + b) for a, b in zip(x, d4)]
    s = 8
    while s < GDN_CHUNK:
        off = ((r // (2 * s)) == (c // (2 * s))) & ((r // s) != (c // s))
        bx = [_dot(jnp.where(off, a, 0.0), xi) for a, xi in zip(mats, x)]
        xbx = [_dot(xi, b) for xi, b in zip(x, bx)]
        x = [xi - b for xi, b in zip(x, xbx)]
        s *= 2
    return x


GDN_PAIRS = 8


def _gdn_chunk(qn, kn, vc, zb, beta, gc, st, onw):
    C = GDN_CHUNK
    R = 2 * C
    n = len(qn)
    rcol = lax.broadcasted_iota(jnp.int32, (R, 1), 0)
    top = rcol < C
    r = lax.broadcasted_iota(jnp.int32, (R, R), 0)
    c = lax.broadcasted_iota(jnp.int32, (R, R), 1)
    same = (r // C) == (c // C)
    low = same & (c <= r)
    slow = same & (c < r)
    srow = lax.broadcasted_iota(jnp.int32, (2 * HD, 1), 0)

    def blocked(a):
        return jnp.concatenate([jnp.where(top, a, 0.0), jnp.where(top, 0.0, a)], axis=1)

    beta2 = [jnp.concatenate(b, axis=0) for b in beta]
    gc2 = [jnp.concatenate(g, axis=0) for g in gc]
    gl2 = [jnp.where(top, g[0][C - 1:C, :], g[1][C - 1:C, :]) for g in gc]
    gls = [jnp.exp(jnp.where(srow < HD, g[0][C - 1:C, :], g[1][C - 1:C, :])) for g in gc]
    dec = []
    for g2 in gc2:
        colm = jnp.broadcast_to(g2, (R, R))
        dec.append(jnp.where(low, jnp.exp(jnp.where(low, colm - colm.T, 0.0)), 0.0))
    k2 = [jnp.concatenate([k, k], axis=0) for k in kn]
    q2 = [jnp.concatenate([q, q], axis=0) for q in qn]
    v2 = [jnp.concatenate([v[:, :HD], v[:, HD:]], axis=0) for v in vc]
    kk = [_dot_nt(k, k) for k in k2]
    qk = [_dot_nt(q, k) for q, k in zip(q2, k2)]
    amat = [jnp.where(slow, kk[i] * beta2[i] * dec[i], 0.0) for i in range(n)]
    attn = [qk[i] * dec[i] for i in range(n)]
    tinv = _unit_lower_inverse(amat, r, c)

    e2 = [jnp.exp(g) for g in gc2]
    rhs = [jnp.concatenate([v2[i] * beta2[i], k2[i] * beta2[i] * e2[i]], axis=1) for i in range(n)]
    sol = [_dot(tinv[i], rhs[i]) for i in range(n)]
    lhs = [jnp.concatenate([blocked(sol[i][:, HD:]), blocked(q2[i] * e2[i])], axis=0) for i in range(n)]
    ws = [_dot(lhs[i], st[i]) for i in range(n)]
    vnew = [sol[i][:, :HD] - ws[i][:R] for i in range(n)]
    av = [_dot(attn[i], vnew[i]) for i in range(n)]
    kd = [blocked(k2[i] * jnp.exp(gl2[i] - gc2[i])) for i in range(n)]
    kv = [_dot_tn(kd[i], vnew[i]) for i in range(n)]
    st_new = [st[i] * gls[i] + kv[i] for i in range(n)]

    outs = []
    for i in range(n):
        o2 = ws[i][R:] + av[i]
        z2 = jnp.concatenate([zb[i][:, :HD], zb[i][:, HD:]], axis=0)
        on = o2 * lax.rsqrt(jnp.mean(o2 * o2, axis=-1, keepdims=True) + NORM_EPS) * onw
        out2 = on * (z2 * _sigmoid(z2))
        outs.append(jnp.concatenate([out2[:C], out2[C:]], axis=1))
    return outs, st_new


def _gdn_kernel(q_ref, k_ref, v_ref, z_ref, bg_ref, wq_ref, wk_ref, wv_ref, cq_ref, ck_ref, cv_ref,
                s0_ref, onw_ref, o_ref, sout_ref, st_ref, bq_ref, bk_ref, bv_ref):
    C = GDN_CHUNK
    G = GDN_PAIRS
    jg = pl.program_id(1)
    ch = pl.program_id(2)

    @pl.when(ch == 0)
    def _():
        for p in range(G):
            st_ref[p, 0:HD, :] = s0_ref[0, 2 * p]
            st_ref[p, HD:2 * HD, :] = s0_ref[0, 2 * p + 1]
        bq_ref[5:8, :] = cq_ref[0]
        bk_ref[5:8, :] = ck_ref[0]
        bv_ref[5:8, :] = cv_ref[0]

    def conv_silu(x_ref, buf_ref, w_ref):
        buf_ref[8:8 + C, :] = x_ref[...]
        w = w_ref[...]
        y = w[0:1, :] * buf_ref[5:5 + C, :]
        for t in range(1, GDN_CONV):
            y = y + w[t:t + 1, :] * buf_ref[5 + t:5 + t + C, :]
        buf_ref[5:8, :] = buf_ref[5 + C:8 + C, :]
        return y * _sigmoid(y)

    qc = conv_silu(q_ref, bq_ref, wq_ref)
    kc = conv_silu(k_ref, bk_ref, wk_ref)
    vc = conv_silu(v_ref, bv_ref, wv_ref)
    zb = z_ref[...]
    bg = bg_ref[...]
    lane = lax.broadcasted_iota(jnp.int32, bg.shape, 1)

    def col(idx):
        return jnp.sum(jnp.where(lane == idx, bg, 0.0), axis=-1, keepdims=True)

    qn, kn, beta, gc = [], [], [], []
    for p in range(G):
        qp = qc[:, p * HD:(p + 1) * HD]
        kp = kc[:, p * HD:(p + 1) * HD]
        qn.append(qp * lax.rsqrt(jnp.sum(qp * qp, axis=-1, keepdims=True) + L2_EPS) * (HD ** -0.5))
        kn.append(kp * lax.rsqrt(jnp.sum(kp * kp, axis=-1, keepdims=True) + L2_EPS))
        head = 2 * (jg * G + p)
        beta.append((col(head), col(head + 1)))
        gc.append((col(GDN_V_HEADS + head), col(GDN_V_HEADS + head + 1)))
    outs, new_states = _gdn_chunk(
        qn, kn, [vc[:, 2 * p * HD:2 * (p + 1) * HD] for p in range(G)],
        [zb[:, 2 * p * HD:2 * (p + 1) * HD] for p in range(G)], beta, gc,
        [st_ref[p] for p in range(G)], onw_ref[...])
    for p in range(G):
        st_ref[p] = new_states[p]
    o_ref[...] = jnp.concatenate(outs, axis=1).astype(o_ref.dtype)

    @pl.when(ch == pl.num_programs(2) - 1)
    def _():
        for p in range(G):
            sout_ref[0, 2 * p] = st_ref[p, 0:HD, :]
            sout_ref[0, 2 * p + 1] = st_ref[p, HD:2 * HD, :]


def _gdn(proj, bg, conv_w, conv0, ssm0, out_norm, *, batch, seq):
    C = GDN_CHUNK
    G = GDN_PAIRS
    nch = seq // C
    ng = GDN_QK_HEADS // G
    row = lambda b, j, c: b * nch + c
    return pl.pallas_call(
        _gdn_kernel,
        grid=(batch, ng, nch),
        in_specs=[
            pl.BlockSpec((C, G * HD), lambda b, j, c: (row(b, j, c), j)),
            pl.BlockSpec((C, G * HD), lambda b, j, c: (row(b, j, c), ng + j)),
            pl.BlockSpec((C, 2 * G * HD), lambda b, j, c: (row(b, j, c), ng + j)),
            pl.BlockSpec((C, 2 * G * HD), lambda b, j, c: (row(b, j, c), 2 * ng + j)),
            pl.BlockSpec((C, HD), lambda b, j, c: (row(b, j, c), 0)),
            pl.BlockSpec((GDN_CONV, G * HD), lambda b, j, c: (0, j)),
            pl.BlockSpec((GDN_CONV, G * HD), lambda b, j, c: (0, ng + j)),
            pl.BlockSpec((GDN_CONV, 2 * G * HD), lambda b, j, c: (0, ng + j)),
            pl.BlockSpec((1, GDN_CONV - 1, G * HD), lambda b, j, c: (b, 0, j)),
            pl.BlockSpec((1, GDN_CONV - 1, G * HD), lambda b, j, c: (b, 0, ng + j)),
            pl.BlockSpec((1, GDN_CONV - 1, 2 * G * HD), lambda b, j, c: (b, 0, ng + j)),
            pl.BlockSpec((1, 2 * G, HD, HD), lambda b, j, c: (b, j, 0, 0)),
            pl.BlockSpec((1, HD), lambda b, j, c: (0, 0)),
        ],
        out_specs=[
            pl.BlockSpec((C, 2 * G * HD), lambda b, j, c: (row(b, j, c), j)),
            pl.BlockSpec((1, 2 * G, HD, HD), lambda b, j, c: (b, j, 0, 0)),
        ],
        out_shape=[
            jax.ShapeDtypeStruct((batch * seq, GDN_VAL_DIM), BF16),
            jax.ShapeDtypeStruct((batch, GDN_V_HEADS, HD, HD), F32),
        ],
        scratch_shapes=[
            pltpu.VMEM((G, 2 * HD, HD), F32),
            pltpu.VMEM((8 + C, G * HD), F32),
            pltpu.VMEM((8 + C, G * HD), F32),
            pltpu.VMEM((8 + C, 2 * G * HD), F32),
        ],
        compiler_params=_cparams(("parallel", "parallel", "arbitrary")),
        name="gdn",
    )(proj, proj, proj, proj, bg, conv_w, conv_w, conv_w, conv0, conv0, conv0, ssm0,
      out_norm.reshape(1, HD))


CMP_PPS = 8


def _cmp1_kernel(pt_ref, *refs, row_packed):
    page_refs = refs[:CMP_PPS]
    w_ref = refs[CMP_PPS]
    o_ref = refs[CMP_PPS + 1]
    nseg = PAGE // CMP_STRIDE
    nch = 2 * NSA_KV
    pr = lax.broadcasted_iota(jnp.int32, (PAGE, PAGE), 0)
    pc = lax.broadcasted_iota(jnp.int32, (PAGE, PAGE), 1)
    perm = jnp.where(pc == (pr % nseg) * CMP_STRIDE + pr // nseg, 1.0, 0.0).astype(BF16)

    def slab(p, ch):
        if row_packed:
            return p[pl.ds(ch, PAGE, stride=nch), :]
        return p[0, :, ch * HD:(ch + 1) * HD]

    perm_slabs = [[jnp.dot(perm, slab(p, ch).astype(BF16), preferred_element_type=F32) for ch in range(nch)]
                  for p in page_refs]
    for cc in range(2):
        acc = jnp.zeros((NSA_KV * CMP_PPS * nseg, 2 * HD), F32)
        for rp in range(CMP_STRIDE // 2):
            lhs = jnp.concatenate(
                [jnp.concatenate([ps[cc * NSA_KV + h][(2 * rp) * nseg:(2 * rp + 1) * nseg],
                                  ps[cc * NSA_KV + h][(2 * rp + 1) * nseg:(2 * rp + 2) * nseg]], axis=1)
                 for h in range(NSA_KV) for ps in perm_slabs], axis=0)
            acc = acc + jnp.dot(lhs.astype(BF16), w_ref[cc, rp], preferred_element_type=F32)
        o_ref[0, cc] = acc


def _cmp_stage1(pages, ptab, w1cat, *, row_packed):
    n = ptab.shape[0]
    nst = n // CMP_PPS
    nseg = PAGE // CMP_STRIDE
    if row_packed:
        specs = [pl.BlockSpec((PAGE * 2 * NSA_KV, HD), lambda s, pt, p=p: (pt[s * CMP_PPS + p], 0))
                 for p in range(CMP_PPS)]
    else:
        specs = [pl.BlockSpec((1, PAGE, 2 * NSA_KV * HD), lambda s, pt, p=p: (pt[s * CMP_PPS + p], 0, 0))
                 for p in range(CMP_PPS)]
    grid_spec = pltpu.PrefetchScalarGridSpec(
        num_scalar_prefetch=1,
        grid=(nst,),
        in_specs=specs + [pl.BlockSpec((2, CMP_STRIDE // 2, 2 * HD, 2 * HD), lambda s, pt: (0, 0, 0, 0))],
        out_specs=pl.BlockSpec((1, 2, NSA_KV * CMP_PPS * nseg, 2 * HD), lambda s, pt: (s, 0, 0, 0)),
    )
    return pl.pallas_call(
        functools.partial(_cmp1_kernel, row_packed=row_packed),
        grid_spec=grid_spec,
        out_shape=jax.ShapeDtypeStruct((nst, 2, NSA_KV * CMP_PPS * nseg, 2 * HD), F32),
        compiler_params=_cparams(("arbitrary",)),
        name="cmp_stage1",
    )(ptab, *([pages] * CMP_PPS), w1cat.reshape(2, CMP_STRIDE // 2, 2 * HD, 2 * HD))


def _cmp2_kernel(a_ref, b_ref, pe_ref, w1_ref, b1_ref, w2_ref, b2_ref, nw_ref, o_ref):
    cc = pl.program_id(0)
    pe = pe_ref[0]
    pec = jnp.dot(pe.astype(BF16), w1_ref[0], preferred_element_type=F32)[0:1, :]
    hid = a_ref[0] + b_ref[0] + pec + b1_ref[0]
    hid = hid * _sigmoid(hid)
    out = jnp.dot(hid.astype(BF16), w2_ref[0], preferred_element_type=F32) + b2_ref[0]

    @pl.when(cc == 0)
    def _():
        o_ref[0] = out * lax.rsqrt(jnp.mean(out * out, axis=-1, keepdims=True) + NORM_EPS) * nw_ref[...]

    @pl.when(cc != 0)
    def _():
        o_ref[0] = out


def _cmp_stage2(a, b, pe8, w1flat, b1, w2, b2, nw, *, tr):
    R = a.shape[1]
    return pl.pallas_call(
        _cmp2_kernel,
        grid=(2, R // tr),
        in_specs=[
            pl.BlockSpec((1, tr, HD), lambda c, i: (c, i, 0)),
            pl.BlockSpec((1, tr, HD), lambda c, i: (c, i, 0)),
            pl.BlockSpec((1, 8, CMP_BLOCK * HD), lambda c, i: (c, 0, 0)),
            pl.BlockSpec((1, CMP_BLOCK * HD, HD), lambda c, i: (c, 0, 0)),
            pl.BlockSpec((1, 1, HD), lambda c, i: (c, 0, 0)),
            pl.BlockSpec((1, HD, HD), lambda c, i: (c, 0, 0)),
            pl.BlockSpec((1, 1, HD), lambda c, i: (c, 0, 0)),
            pl.BlockSpec((1, HD), lambda c, i: (0, 0)),
        ],
        out_specs=pl.BlockSpec((1, tr, HD), lambda c, i: (c, i, 0)),
        out_shape=jax.ShapeDtypeStruct((2, R, HD), F32),
        compiler_params=_cparams(("arbitrary", "arbitrary")),
        name="cmp_stage2",
    )(a, b, pe8, w1flat, b1.reshape(2, 1, HD), w2, b2.reshape(2, 1, HD), nw.reshape(1, HD))


def _stack_heads(qb):
    return jnp.concatenate([qb[:, g * HD:(g + 1) * HD] for g in range(NSA_G)], axis=0)


def _unstack_heads(o, tq):
    return jnp.concatenate([o[g * tq:(g + 1) * tq] for g in range(NSA_G)], axis=1)


def _gate_rows(gt):
    return jnp.concatenate([gt[:, g:g + 1] for g in range(NSA_G)], axis=0)


def _attn_cmp_kernel(q_ref, kc_ref, vc_ref, bias_ref, gate_ref, o_ref, sel_ref, *idx_ref, tq, q_pos0, nc, ns, nsp,
                     n_idx):
    i = pl.program_id(2)
    ncp = kc_ref.shape[2]
    qs = _stack_heads(q_ref[...])
    logits = _dot_nt(qs, kc_ref[0, 0])
    logits = logits + jnp.concatenate([bias_ref[0, g] for g in range(NSA_G)], axis=0)
    rows = NSA_G * tq
    t4 = q_pos0 + i * tq + lax.broadcasted_iota(jnp.int32, (rows, ncp), 0) % tq
    cidx = lax.broadcasted_iota(jnp.int32, (rows, ncp), 1)
    mask = (cidx * CMP_STRIDE + (CMP_BLOCK - 1) <= t4) & (cidx < nc)
    lg = jnp.where(mask, logits, NEG)
    mx = jnp.max(lg, axis=-1, keepdims=True)
    ex = jnp.exp(lg - mx)
    p = ex / jnp.sum(ex, axis=-1, keepdims=True) * jnp.where(mask, 1.0, 0.0)
    oc = _dot(p, vc_ref[0, 0])
    o_ref[...] = _unstack_heads(oc * _gate_rows(gate_ref[0, 0, 0]), tq)

    psum = p[0:tq]
    for g in range(1, NSA_G):
        psum = psum + p[g * tq:(g + 1) * tq]
    cr = lax.broadcasted_iota(jnp.int32, (ncp, nsp), 0)
    sc = lax.broadcasted_iota(jnp.int32, (ncp, nsp), 1)
    c2s = (cr * CMP_STRIDE < sc * SEL_BLOCK + SEL_BLOCK) & (cr * CMP_STRIDE + CMP_BLOCK > sc * SEL_BLOCK)
    c2s = jnp.where(c2s & (cr < nc) & (sc < ns), 1.0, 0.0)
    imp = jnp.dot(psum, c2s, preferred_element_type=F32, precision=lax.Precision.HIGHEST)

    t = q_pos0 + i * tq + lax.broadcasted_iota(jnp.int32, (tq, nsp), 0)
    s = lax.broadcasted_iota(jnp.int32, (tq, nsp), 1)
    cur = t // SEL_BLOCK
    forced = (s == 0) | (s == cur) | (s == cur - 1)
    valid = s * SEL_BLOCK <= t
    score = jnp.where(forced, NSA_G + 1.0, jnp.where(valid, imp, -1.0))
    score = jnp.where(s < ns, score, -2.0)
    rank = jnp.zeros((tq, nsp), F32)
    for sp in range(ns):
        other = score[:, sp:sp + 1]
        ahead = (other > score) | ((other == score) & (sp < s))
        rank = rank + jnp.where(ahead, 1.0, 0.0)
    top_k = min(SEL_TOPK, ns)
    picked = (rank < top_k) & (s < ns)
    sel_ref[0, 0] = jnp.where(picked, 1.0, 0.0)
    if n_idx:
        listed = jnp.where(picked & (s < ns - 1), 1.0, 0.0)
        before = _dot(listed, jnp.where(lax.broadcasted_iota(jnp.int32, (nsp, nsp), 0)
                                        < lax.broadcasted_iota(jnp.int32, (nsp, nsp), 1), 1.0, 0.0))
        lane = lax.broadcasted_iota(jnp.int32, (tq, HD), 1)
        sf = s.astype(F32)
        out = jnp.zeros((tq, HD), F32)
        for kk in range(n_idx):
            hit = (listed > 0.5) & (before == float(kk))
            out = out + jnp.where(lane == kk, jnp.sum(jnp.where(hit, sf, 0.0), axis=-1, keepdims=True), 0.0)
        idx_ref[0][0, 0] = out.astype(jnp.int32)


def _attn_cmp(q, kcvc, bias_c, gate, *, batch, seq, tq, q_pos0, nc, ns, nsp, n_idx=0):
    nqt = seq // tq
    ncp = kcvc.shape[3]
    kern = functools.partial(_attn_cmp_kernel, tq=tq, q_pos0=q_pos0, nc=nc, ns=ns, nsp=nsp, n_idx=n_idx)
    rows_per_b = seq // tq
    extra_specs = [pl.BlockSpec((1, 1, tq, HD), lambda b, h, i: (b, h, i, 0))] if n_idx else []
    extra_shapes = [jax.ShapeDtypeStruct((batch, NSA_KV, seq, HD), jnp.int32)] if n_idx else []
    return pl.pallas_call(
        kern,
        grid=(batch, NSA_KV, nqt),
        in_specs=[
            pl.BlockSpec((tq, NSA_G * HD), lambda b, h, i: (b * rows_per_b + i, h)),
            pl.BlockSpec((None, 1, 1, ncp, HD), lambda b, h, i: (0, b, h, 0, 0)),
            pl.BlockSpec((None, 1, 1, ncp, HD), lambda b, h, i: (1, b, h, 0, 0)),
            pl.BlockSpec((1, NSA_G, tq, ncp), lambda b, h, i: (h, 0, i, 0)),
            pl.BlockSpec((1, 1, 1, tq, NSA_G), lambda b, h, i: (0, b, h, i, 0)),
        ],
        out_specs=[
            pl.BlockSpec((tq, NSA_G * HD), lambda b, h, i: (b * rows_per_b + i, h)),
            pl.BlockSpec((1, 1, tq, nsp), lambda b, h, i: (b, h, i, 0)),
        ] + extra_specs,
        out_shape=[
            jax.ShapeDtypeStruct((batch * seq, NSA_HEADS * HD), F32),
            jax.ShapeDtypeStruct((batch, NSA_KV, seq, nsp), F32),
        ] + extra_shapes,
        compiler_params=_cparams(("parallel", "parallel", "arbitrary")),
        name="attn_cmp",
    )(q, kcvc, kcvc, bias_c, gate)


def _flash_kernel(pt_ref, *refs, cfg):
    pps, tq, has_tail, use_sel = cfg["pps"], cfg["tq"], cfg["has_tail"], cfg["use_sel"]
    it = iter(refs)
    q_ref = next(it)
    k_refs = [next(it) for _ in range(pps)]
    v_refs = [next(it) for _ in range(pps)]
    b_refs = [next(it) for _ in range(pps)]
    if has_tail:
        kt_ref, vt_ref, bt_ref = next(it), next(it), next(it)
    sel_ref = next(it) if use_sel else None
    gate_ref = next(it)
    prev_ref = next(it)
    o_ref = next(it)
    m_ref, l_ref, acc_ref = next(it), next(it), next(it)

    i = pl.program_id(2)
    st = pl.program_id(3)
    rows = NSA_G * tq

    @pl.when(st == 0)
    def _():
        m_ref[...] = jnp.full((rows, 1), NEG, F32)
        l_ref[...] = jnp.zeros((rows, 1), F32)
        acc_ref[...] = jnp.zeros((rows, HD), F32)

    t0 = cfg["q_pos0"] + i * tq

    def tile(k, v, bias4, kt, p0):
        qs = _stack_heads(q_ref[...])
        s = _dot_nt(qs, k) + jnp.concatenate([bias4[g] for g in range(NSA_G)], axis=0)
        tt = t0 + lax.broadcasted_iota(jnp.int32, (tq, TK), 0)
        pos = p0 + lax.broadcasted_iota(jnp.int32, (tq, TK), 1)
        dist = tt - pos
        ok = dist >= 0
        if use_sel:
            nsp = sel_ref.shape[3]
            sr = lax.broadcasted_iota(jnp.int32, (nsp, TK), 0)
            sc = lax.broadcasted_iota(jnp.int32, (nsp, TK), 1)
            expand = jnp.where(sr == kt * (TK // SEL_BLOCK) + sc // SEL_BLOCK, 1.0, 0.0)
            picked = _dot(sel_ref[0, 0], expand)
            ok = ok & (picked > 0.5)
        else:
            ok = ok & (dist < WINDOW) & (pos >= cfg["w_pos0"])
        okf = jnp.where(ok, 1.0, 0.0)
        ok4 = jnp.concatenate([okf] * NSA_G, axis=0)
        s = jnp.where(ok4 > 0.5, s, NEG)
        m_old = m_ref[...]
        m_new = jnp.maximum(m_old, jnp.max(s, axis=-1, keepdims=True))
        alpha = jnp.exp(m_old - m_new)
        p = jnp.exp(s - m_new) * ok4
        l_ref[...] = alpha * l_ref[...] + jnp.sum(p, axis=-1, keepdims=True)
        acc_ref[...] = alpha * acc_ref[...] + _dot(p, v)
        m_ref[...] = m_new

    for pp in range(pps):
        kt = cfg["tile_of"](i, st, pp)
        active = cfg["active"](i, st, pp)
        p0 = cfg["kbase"] + kt * TK
        if active is True:
            tile(k_refs[pp][0], v_refs[pp][0], b_refs[pp][0, 0], kt, p0)
        else:
            @pl.when(active)
            def _(pp=pp, kt=kt, p0=p0):
                tile(k_refs[pp][0], v_refs[pp][0], b_refs[pp][0, 0], kt, p0)

    if has_tail:
        @pl.when(st == pl.num_programs(3) - 1)
        def _():
            tile(kt_ref[0], vt_ref[0], bt_ref[0, 0], cfg["tail_tile"], cfg["kbase"] + cfg["tail_tile"] * TK)

    @pl.when(st == pl.num_programs(3) - 1)
    def _():
        o = acc_ref[...] / l_ref[...] * _gate_rows(gate_ref[0, 0, 0])
        o_ref[...] = (prev_ref[...] + _unstack_heads(o, tq)).astype(o_ref.dtype)


def _flash(q, pages, ptab, kcol, vcol, bias_tiles, tails, sel, gate, branch, prev, *, batch, seq, tq, q_pos0,
           pps, nsteps, tile_of, active, kbase, w_pos0, npt, tail_tile, out_dtype):
    nqt = seq // tq
    has_tail = tails is not None
    use_sel = sel is not None
    n_delta = bias_tiles.shape[1]
    cfg = dict(pps=pps, tq=tq, has_tail=has_tail, use_sel=use_sel, q_pos0=q_pos0, tile_of=tile_of,
               active=active, kbase=kbase, w_pos0=w_pos0, tail_tile=tail_tile)

    def page_idx(b, i, s, pp, pt):
        kt = jnp.clip(tile_of(i, s, pp), 0, npt - 1)
        return pt[b * npt + kt]

    def didx(i, s, pp):
        kt = tile_of(i, s, pp)
        return jnp.clip((q_pos0 + i * tq - kbase - kt * TK) // TK, 0, n_delta - 1)

    in_specs = [pl.BlockSpec((tq, NSA_G * HD), lambda b, h, i, s, pt: (b * nqt + i, h))]
    args = [q]
    for col in (kcol, vcol):
        for pp in range(pps):
            in_specs.append(pl.BlockSpec(
                (1, TK, HD), lambda b, h, i, s, pt, pp=pp, col=col: (page_idx(b, i, s, pp, pt), 0, col + h)))
            args.append(pages)
    for pp in range(pps):
        in_specs.append(pl.BlockSpec(
            (1, 1, NSA_G, tq, TK), lambda b, h, i, s, pt, pp=pp: (h, didx(i, s, pp), 0, 0, 0)))
        args.append(bias_tiles)
    if has_tail:
        tail_pages, tkcol, tvcol = tails
        tdelta = min(max((q_pos0 - kbase - tail_tile * TK) // TK, 0), n_delta - 1)
        in_specs.append(pl.BlockSpec((1, TK, HD), lambda b, h, i, s, pt: (b, 0, tkcol + h)))
        in_specs.append(pl.BlockSpec((1, TK, HD), lambda b, h, i, s, pt: (b, 0, tvcol + h)))
        in_specs.append(pl.BlockSpec((1, 1, NSA_G, tq, TK), lambda b, h, i, s, pt: (h, tdelta, 0, 0, 0)))
        args += [tail_pages, tail_pages, bias_tiles]
    if use_sel:
        nsp = sel.shape[3]
        in_specs.append(pl.BlockSpec((1, 1, tq, nsp), lambda b, h, i, s, pt: (b, h, i, 0)))
        args.append(sel)
    in_specs.append(pl.BlockSpec((1, 1, 1, tq, NSA_G), lambda b, h, i, s, pt: (branch, b, h, i, 0)))
    args.append(gate)
    in_specs.append(pl.BlockSpec((tq, NSA_G * HD), lambda b, h, i, s, pt: (b * nqt + i, h)))
    args.append(prev)

    rows = NSA_G * tq
    grid_spec = pltpu.PrefetchScalarGridSpec(
        num_scalar_prefetch=1,
        grid=(batch, NSA_KV, nqt, nsteps),
        in_specs=in_specs,
        out_specs=pl.BlockSpec((tq, NSA_G * HD), lambda b, h, i, s, pt: (b * nqt + i, h)),
        scratch_shapes=[pltpu.VMEM((rows, 1), F32), pltpu.VMEM((rows, 1), F32), pltpu.VMEM((rows, HD), F32)],
    )
    return pl.pallas_call(
        functools.partial(_flash_kernel, cfg=cfg),
        grid_spec=grid_spec,
        out_shape=jax.ShapeDtypeStruct((batch * seq, NSA_HEADS * HD), out_dtype),
        compiler_params=_cparams(("parallel", "parallel", "arbitrary", "arbitrary")),
        name="flash_sel" if use_sel else "flash_win",
    )(ptab, *args)


def _selg_kernel(idx_ref, pt_ref, q_ref, *refs, nblk, tq, q_pos0, tail_pos0):
    kv_refs = refs[:nblk]
    b_refs = refs[nblk:2 * nblk]
    tk_ref, tv_ref, tb_ref, gate_ref, prev_ref, o_ref, osc_ref = refs[2 * nblk:]
    b = pl.program_id(0)
    h = pl.program_id(1)
    qi = pl.program_id(2)
    nq = pl.num_programs(2)
    rows = NSA_G * tq
    nch = 2 * NSA_KV

    @pl.when(qi == 0)
    def _():
        osc_ref[...] = jnp.zeros((rows, HD), F32)

    qs = _stack_heads(q_ref[...])
    t = q_pos0 + qi
    base = ((b * NSA_KV + h) * nq + qi) * nblk
    jj = lax.broadcasted_iota(jnp.int32, (rows, SEL_BLOCK), 1)

    def bias_rows(bref):
        bb = bref[0, 0, 0]
        return jnp.concatenate([jnp.broadcast_to(bb[g:g + 1, :], (tq, SEL_BLOCK)) for g in range(NSA_G)], axis=0)

    scores, vals = [], []
    for n in range(nblk):
        k = kv_refs[n][pl.ds(h, SEL_BLOCK, stride=nch), :]
        v = kv_refs[n][pl.ds(NSA_KV + h, SEL_BLOCK, stride=nch), :]
        s = _dot_nt(qs, k) + bias_rows(b_refs[n])
        pos = idx_ref[base + n] * SEL_BLOCK + jj
        scores.append(jnp.where(pos <= t, s, NEG))
        vals.append(v)
    s = _dot_nt(qs, tk_ref[0]) + bias_rows(tb_ref)
    scores.append(jnp.where(tail_pos0 + jj <= t, s, NEG))
    vals.append(tv_ref[0])

    m = functools.reduce(jnp.maximum, [jnp.max(s, axis=-1, keepdims=True) for s in scores])
    l = jnp.zeros((rows, 1), F32)
    acc = jnp.zeros((rows, HD), F32)
    for s, v in zip(scores, vals):
        p = jnp.exp(s - m)
        l = l + jnp.sum(p, axis=-1, keepdims=True)
        acc = acc + _dot(p, v)
    rowq = lax.broadcasted_iota(jnp.int32, (rows, 1), 0) % tq
    osc = jnp.where(rowq == qi, acc / l, osc_ref[...])
    osc_ref[...] = osc

    @pl.when(qi == nq - 1)
    def _():
        o_ref[...] = prev_ref[...] + _unstack_heads(osc * _gate_rows(gate_ref[0, 0, 0]), tq)


def _sel_gather(q, cache_rows, ptab, idx, bias_blk, tail, gate, prev, *, batch, tq, nq, q_pos0, npt, ns):
    nblk = idx.shape[0] // (batch * NSA_KV * nq)
    half = SEL_BLOCK * 2 * NSA_KV
    per_page = PAGE // SEL_BLOCK

    def blk(b, h, qi, n, idx_ref):
        return idx_ref[((b * NSA_KV + h) * nq + qi) * nblk + n]

    def kv_map(n):
        def f(b, h, qi, idx_ref, pt_ref):
            s = blk(b, h, qi, n, idx_ref)
            return (pt_ref[b * npt + s // per_page] * per_page + s % per_page, 0)
        return f

    in_specs = [pl.BlockSpec((tq, NSA_G * HD), lambda b, h, qi, i_, p_: (b, h))]
    in_specs += [pl.BlockSpec((half, HD), kv_map(n)) for n in range(nblk)]
    in_specs += [pl.BlockSpec((1, 1, 1, NSA_G, SEL_BLOCK),
                              lambda b, h, qi, i_, p_, n=n: (qi, h, blk(b, h, qi, n, i_), 0, 0)) for n in range(nblk)]
    in_specs += [
        pl.BlockSpec((1, SEL_BLOCK, HD), lambda b, h, qi, i_, p_: (b, 0, h)),
        pl.BlockSpec((1, SEL_BLOCK, HD), lambda b, h, qi, i_, p_: (b, 0, NSA_KV + h)),
        pl.BlockSpec((1, 1, 1, NSA_G, SEL_BLOCK), lambda b, h, qi, i_, p_: (qi, h, ns - 1, 0, 0)),
        pl.BlockSpec((1, 1, 1, tq, NSA_G), lambda b, h, qi, i_, p_: (1, b, h, 0, 0)),
        pl.BlockSpec((tq, NSA_G * HD), lambda b, h, qi, i_, p_: (b, h)),
    ]
    grid_spec = pltpu.PrefetchScalarGridSpec(
        num_scalar_prefetch=2,
        grid=(batch, NSA_KV, nq),
        in_specs=in_specs,
        out_specs=pl.BlockSpec((tq, NSA_G * HD), lambda b, h, qi, i_, p_: (b, h)),
        scratch_shapes=[pltpu.VMEM((NSA_G * tq, HD), F32)],
    )
    kern = functools.partial(_selg_kernel, nblk=nblk, tq=tq, q_pos0=q_pos0, tail_pos0=(ns - 1) * SEL_BLOCK)
    return pl.pallas_call(
        kern,
        grid_spec=grid_spec,
        out_shape=jax.ShapeDtypeStruct((batch * tq, NSA_HEADS * HD), F32),
        compiler_params=_cparams(("parallel", "parallel", "arbitrary")),
        name="sel_gather",
    )(idx, ptab, q, *([cache_rows] * nblk), *([bias_blk] * nblk), tail, tail, bias_blk, gate, prev)


def _rel_bucket(dist):
    n = jnp.maximum(dist, 0)
    max_exact = REL_BUCKETS // 2
    nf = jnp.maximum(n, 1).astype(F32)
    large = max_exact + (jnp.log(nf / max_exact) / math.log(REL_MAX_DIST / max_exact)
                         * (REL_BUCKETS - max_exact)).astype(jnp.int32)
    return jnp.where(n < max_exact, n, jnp.minimum(large, REL_BUCKETS - 1))


def _bias_by_distance(rel_bias):
    return rel_bias.astype(F32)[_rel_bucket(jnp.arange(REL_MAX_DIST))]


def _toeplitz(r, nrows, ncols):
    p = r.shape[-1]
    flat = jnp.tile(r, (1,) * (r.ndim - 1) + (nrows,))[..., :nrows * (p - 1)]
    return flat.reshape(r.shape[:-1] + (nrows, p - 1))[..., :ncols]


def _bias_tiles(rel_bias, tq, n_delta):
    fd = _bias_by_distance(rel_bias).T
    hi = TK * (n_delta + 1) - REL_MAX_DIST
    padded = jnp.concatenate([jnp.tile(fd[:, :1], (1, TK)), fd[:, :TK * (n_delta + 1)]]
                             + ([jnp.tile(fd[:, -1:], (1, hi))] if hi > 0 else []), axis=1)
    blocks = padded[:, 1:1 + TK * (n_delta + 1)].reshape(NSA_HEADS, n_delta + 1, TK)
    z = jnp.concatenate([blocks[:, :n_delta], blocks[:, 1:, :tq - 1]], axis=2)
    r = jnp.roll(z[:, :, ::-1], -(tq - 1), axis=2)
    t = _toeplitz(r, tq, TK)
    return t.reshape(NSA_KV, NSA_G, n_delta, tq, TK).transpose(0, 2, 1, 3, 4)


def _bias_cmp(rel_bias, q_pos0, tqs, ncp):
    fd = _bias_by_distance(rel_bias)
    last = CMP_BLOCK - 1
    if tqs <= SEL_BLOCK:
        dist = (q_pos0 + jnp.arange(tqs))[:, None] - (jnp.arange(ncp) * CMP_STRIDE + last)[None, :]
        b = fd[jnp.clip(dist, 0, REL_MAX_DIST - 1)]
        return b.reshape(tqs, ncp, NSA_KV, NSA_G).transpose(2, 3, 0, 1)
    assert q_pos0 == 0 and tqs % CMP_STRIDE == 0
    ntau = tqs // CMP_STRIDE
    period = ntau + ncp
    kappa = period - jnp.arange(period)
    dist = CMP_STRIDE * kappa[None, :] + jnp.arange(CMP_STRIDE)[:, None] - last
    dist = jnp.where(kappa[None, :] < ntau, dist, 0)
    r = fd[jnp.clip(dist, 0, REL_MAX_DIST - 1)]
    t = _toeplitz(r.transpose(2, 0, 1), ntau, ncp)
    return t.transpose(0, 2, 1, 3).reshape(NSA_KV, NSA_G, tqs, ncp)


def _bias_blocks(rel_bias, q_pos0, nq, ns):
    fd = _bias_by_distance(rel_bias)
    n = ns * SEL_BLOCK
    rows = []
    for qq in range(nq):
        t = q_pos0 + qq
        far = max(min(t - (REL_MAX_DIST - 1), n), 0)
        mid_hi = min(t + 1, n)
        parts = [jnp.tile(fd[-1:], (far, 1))] if far else []
        if mid_hi > far:
            parts.append(fd[t - mid_hi + 1:t - far + 1][::-1])
        if n > mid_hi:
            parts.append(jnp.tile(fd[:1], (n - mid_hi, 1)))
        rows.append(jnp.concatenate(parts, axis=0))
    t = jnp.stack(rows)
    return t.reshape(nq, ns, SEL_BLOCK, NSA_KV, NSA_G).transpose(0, 3, 1, 4, 2)


def _forward_group(x, conv0, ssm0, past, P, *, batch, seq, t_valid, q_pos0, tq):
    M = batch * seq
    conv_out, ssm_out = [], []
    for l in range(2):
        proj = _nmm(x, P["mix_norm"][l], P["gdn_w_main"][l], tn=512)
        bg = _nmm(x, P["mix_norm"][l], P["gdn_w_gate"][l], tn=128, mode="gdn_gate", aux=P["gdn_gate_aux"][l],
                  seq=seq, t_valid=t_valid)
        o, s_new = _gdn(proj, bg, P["gdn_conv_w"][l], conv0[l], ssm0[l], P["gdn_out_norm"][l],
                        batch=batch, seq=seq)
        conv_out.append(proj.reshape(batch, seq, -1)[:, t_valid - (GDN_CONV - 1):t_valid, :GDN_CONV_DIM])
        ssm_out.append(s_new)
        x = _mm_res(o, P["gdn_w_out"][l], x)
        x = _mlp(x, P["mlp_norm"][l], P["mlp_w1"][l], P["mlp_w2"][l])
    x, cmp_rows, sel_rows, win_state = _nsa_layers(x, past, P, batch=batch, seq=seq, t_valid=t_valid,
                                                   q_pos0=q_pos0, tq=tq)
    return x, jnp.stack(conv_out), jnp.stack(ssm_out), cmp_rows, sel_rows, win_state


def _nsa_layers(x, past, P, *, batch, seq, t_valid, q_pos0, tq):
    M = batch * seq
    kv = _nmm(x, P["kv_norm"], P["nsa_w_kv"], tn=512, mode="headnorm", aux=P["kv_aux"], norm_tiles=(2, 4))
    kv3 = kv.reshape(batch, seq, 6 * NSA_KV * HD)
    new_rows = kv3[:, :t_valid]
    cmp_rows = new_rows[..., 0:1024].reshape(batch, t_valid, 2, NSA_KV, HD)
    sel_rows = new_rows[..., 1024:2048].reshape(batch, t_valid, 2, NSA_KV, HD)
    win_new = new_rows[..., 2048:3072]

    ident = jnp.arange(M // PAGE, dtype=jnp.int32) if seq % PAGE == 0 else None
    if past is None:
        n_tot = t_valid
        npages = seq // PAGE
        kv_pages = kv.reshape(M // PAGE, PAGE, 6 * NSA_KV * HD)
        first = _cmp_stage1(kv_pages, ident, P["cmp_w1cat"], row_packed=False)
        nsb = npages // CMP_PPS
        nseg_tot = npages * (PAGE // CMP_STRIDE)
        f6 = first.reshape(batch, nsb, 2, NSA_KV, CMP_PPS * 8, 2, HD).transpose(2, 0, 3, 1, 4, 5, 6)
        f6 = f6.reshape(2, batch, NSA_KV, nseg_tot, 2, HD)
        win_seq = win_new
        w_pos0 = 0
    else:
        n_past = past["page_table"].shape[1] * PAGE
        n_tot = n_past + t_valid
        npages = n_past // PAGE
        ptab = past["page_table"].reshape(-1)
        first = _cmp_stage1(past["cmp_rows"], ptab, P["cmp_w1cat"], row_packed=True)
        nsb = npages // CMP_PPS
        f6 = first.reshape(batch, nsb, 2, NSA_KV, CMP_PPS * 8, 2, HD).transpose(2, 0, 3, 1, 4, 5, 6)
        f6 = f6.reshape(2, batch, NSA_KV, npages * 8, 2, HD)
        tail_cmp = jnp.pad(new_rows[..., 0:1024], ((0, 0), (0, PAGE - t_valid), (0, 0)))
        tail_cmp = jnp.pad(tail_cmp, ((0, (-batch) % CMP_PPS), (0, 0), (0, 0)))
        tfirst = _cmp_stage1(tail_cmp, jnp.arange(tail_cmp.shape[0], dtype=jnp.int32), P["cmp_w1cat"],
                             row_packed=False)
        t6 = tfirst.reshape(-1, 2, NSA_KV, CMP_PPS, 8, 2, HD).transpose(1, 0, 3, 2, 4, 5, 6)
        t6 = t6.reshape(2, -1, NSA_KV, 8, 2, HD)[:, :batch, :, :(-(-t_valid // CMP_STRIDE))]
        f6 = jnp.concatenate([f6, t6], axis=3)
        nseg_tot = f6.shape[3]
        win_seq = jnp.concatenate([past["win"], win_new], axis=1)
        w_pos0 = q_pos0 + t_valid - win_seq.shape[1]
    nc = -(-n_tot // CMP_STRIDE) - 1
    ns = -(-n_tot // SEL_BLOCK)
    ncp = -(-nc // 128) * 128
    nsp = -(-ns // 128) * 128
    a = f6[:, :, :, 0:nc, 0, :]
    b = f6[:, :, :, 1:nc + 1, 1, :]
    if b.shape[3] < nc:
        b = jnp.pad(b, ((0, 0), (0, 0), (0, 0), (0, nc - b.shape[3]), (0, 0)))
    a = jnp.pad(a, ((0, 0), (0, 0), (0, 0), (0, ncp - nc), (0, 0))).reshape(2, batch * NSA_KV * ncp, HD)
    b = jnp.pad(b, ((0, 0), (0, 0), (0, 0), (0, ncp - nc), (0, 0))).reshape(2, batch * NSA_KV * ncp, HD)
    R = batch * NSA_KV * ncp
    kcvc = _cmp_stage2(a, b, P["cmp_pe8"], P["cmp_w1flat"], P["cmp_b1"], P["cmp_w2"], P["cmp_b2"],
                       P["k_cmp_norm"], tr=min(R, 2048))
    kcvc = kcvc.reshape(2, batch, NSA_KV, ncp, HD)

    n_keep = min(WINDOW, win_seq.shape[1])
    win_state = win_seq[:, win_seq.shape[1] - n_keep:].reshape(batch, n_keep, 2, NSA_KV, HD)

    seq_q = seq if past is None else tq
    bias_c = _bias_cmp(P["rel_bias"], q_pos0, seq_q, ncp)
    btiles = _bias_tiles(P["rel_bias"], tq, min(N_DELTA, (q_pos0 + seq_q) // TK + 1))
    if past is None:
        sel_pages, sel_ptab, sel_npt = kv_pages, ident, seq // PAGE
        sel_kcol, sel_vcol = 8, 12
        sel_tails = None
        sel_pps = 4
        sel_steps = -(-sel_npt // sel_pps)
        sel_tile_of = lambda i, s, pp: s * sel_pps + pp
        sel_active = lambda i, s, pp: (s * sel_pps + pp) * TK <= i * tq + tq - 1
        win_pages, win_ptab, win_npt = kv_pages, ident, seq // PAGE
        win_kcol, win_vcol = 16, 20
        win_pps = WINDOW // TK + tq // TK
        win_tile_of = lambda i, s, pp: (i * tq) // TK - WINDOW // TK + pp
        win_active = lambda i, s, pp: (i * tq) // TK - WINDOW // TK + pp >= 0
        win_kbase = 0
    else:
        assert n_past % SEL_BLOCK == 0 and t_valid <= SEL_BLOCK and ns - 1 > SEL_TOPK
        tail_sel = jnp.pad(new_rows[..., 1024:2048], ((0, 0), (0, SEL_BLOCK - t_valid), (0, 0)))
        bias_blk = _bias_blocks(P["rel_bias"], q_pos0, t_valid, ns)
        nwt = -(-win_seq.shape[1] // TK)
        win_pages = jnp.pad(win_seq, ((0, 0), (0, nwt * TK - win_seq.shape[1]), (0, 0)))
        win_pages = win_pages.reshape(batch * nwt, TK, 2 * NSA_KV * HD)
        win_ptab, win_npt = jnp.arange(batch * nwt, dtype=jnp.int32), nwt
        win_kcol, win_vcol = 0, 4
        win_pps = nwt
        win_tile_of = lambda i, s, pp: pp
        win_active = lambda i, s, pp: True
        win_kbase = w_pos0

    for jj in range(2):
        l = 2 + jj
        q = _nmm(x, P["mix_norm"][l], P["nsa_w_q"][jj], tn=512, out_dtype=BF16, mode="headnorm",
                 aux=P["nsa_q_aux"][jj], scale=HD ** -0.5)
        gates = _nmm(x, P["mix_norm"][l], P["nsa_w_g"][jj], tn=128, mode="sigmoid")
        gate = gates[:, :NSA_HEADS * 3].reshape(batch, seq, NSA_KV, NSA_G, 3).transpose(4, 0, 2, 1, 3)
        if seq_q != seq:
            q = q.reshape(batch, seq, -1)[:, :seq_q].reshape(batch * seq_q, -1)
            gate = gate[:, :, :, :seq_q]
        if past is None:
            o_c, sel = _attn_cmp(q, kcvc, bias_c, gate, batch=batch, seq=seq_q, tq=tq, q_pos0=q_pos0, nc=nc,
                                 ns=ns, nsp=nsp)
            o_s = _flash(q, sel_pages, sel_ptab, sel_kcol, sel_vcol, btiles, sel_tails, sel, gate, 1, o_c,
                         batch=batch, seq=seq_q, tq=tq, q_pos0=q_pos0, pps=sel_pps,
                         nsteps=sel_steps, tile_of=sel_tile_of, active=sel_active, kbase=0, w_pos0=0,
                         npt=sel_npt, tail_tile=sel_npt, out_dtype=F32)
        else:
            o_c, sel, idx = _attn_cmp(q, kcvc, bias_c, gate, batch=batch, seq=seq_q, tq=tq, q_pos0=q_pos0,
                                      nc=nc, ns=ns, nsp=nsp, n_idx=SEL_TOPK - 1)
            o_s = _sel_gather(q, past["sel_rows"], past["page_table"].reshape(-1),
                              idx[:, :, :t_valid, :SEL_TOPK - 1].reshape(-1), bias_blk, tail_sel, gate, o_c,
                              batch=batch, tq=tq, nq=t_valid, q_pos0=q_pos0, npt=npages, ns=ns)
        o_w = _flash(q, win_pages, win_ptab, win_kcol, win_vcol, btiles, None, None, gate, 2, o_s,
                     batch=batch, seq=seq_q, tq=tq, q_pos0=q_pos0, pps=win_pps,
                     nsteps=1, tile_of=win_tile_of, active=win_active, kbase=win_kbase, w_pos0=w_pos0,
                     npt=win_npt, tail_tile=0, out_dtype=BF16 if tq % 16 == 0 else F32)
        if seq_q != seq:
            o_w = jnp.pad(o_w.reshape(batch, seq_q, -1), ((0, 0), (0, seq - seq_q), (0, 0))).reshape(M, -1)
        x = _mm_res(o_w.astype(BF16), P["nsa_w_out"][jj], x)
        x = _mlp(x, P["mlp_norm"][l], P["mlp_w1"][l], P["mlp_w2"][l])
    return x, cmp_rows, sel_rows, win_state


def _prepare_params(mix_norm, mlp_norm, mlp_w1, mlp_w2, gdn_w_in, gdn_conv_w, gdn_a_log, gdn_dt_bias,
                    gdn_out_norm, gdn_w_out, kv_norm, nsa_w_kv, k_sel_norm, k_win_norm, k_cmp_norm, cmp_pe,
                    cmp_w1, cmp_b1, cmp_w2, cmp_b2, nsa_w_in, nsa_q_norm, nsa_w_out, rel_bias):
    n_lay = gdn_w_in.shape[0]
    main = GDN_CONV_DIM + GDN_VAL_DIM
    zpad = lambda n: jnp.zeros((1, n), F32)
    gate_aux = jnp.stack([
        jnp.concatenate([
            jnp.concatenate([zpad(GDN_V_HEADS), gdn_a_log[l][None].astype(F32), zpad(HD - 2 * GDN_V_HEADS)], 1),
            jnp.concatenate([zpad(GDN_V_HEADS), gdn_dt_bias[l][None].astype(F32), zpad(HD - 2 * GDN_V_HEADS)], 1),
        ], 0)[None] for l in range(n_lay)])
    tile4 = lambda w: jnp.tile(w.astype(F32), NSA_KV)[None, None]
    kv_aux = jnp.concatenate([jnp.ones((2, 1, 512), F32), tile4(k_sel_norm), jnp.ones((1, 1, 512), F32),
                              tile4(k_win_norm), jnp.ones((1, 1, 512), F32)], 0)
    nq = NSA_HEADS * HD
    w1r = cmp_w1.reshape(2, 2, CMP_STRIDE, HD, HD)
    P = dict(
        mix_norm=mix_norm, mlp_norm=mlp_norm,
        mlp_w1=mlp_w1.astype(BF16), mlp_w2=mlp_w2.astype(BF16),
        gdn_w_main=gdn_w_in[:, :, :main].astype(BF16),
        gdn_w_gate=jnp.pad(gdn_w_in[:, :, main:], ((0, 0), (0, 0), (0, HD - 2 * GDN_V_HEADS))).astype(BF16),
        gdn_gate_aux=gate_aux, gdn_conv_w=gdn_conv_w, gdn_out_norm=gdn_out_norm,
        gdn_w_out=gdn_w_out.astype(BF16),
        kv_norm=kv_norm, nsa_w_kv=nsa_w_kv.astype(BF16), kv_aux=kv_aux, k_cmp_norm=k_cmp_norm,
        cmp_w1cat=jnp.concatenate([w1r[:, 0], w1r[:, 1]], axis=-1).astype(BF16),
        cmp_w1flat=cmp_w1.reshape(2, CMP_BLOCK * HD, HD).astype(BF16),
        cmp_pe8=jnp.pad(cmp_pe.reshape(2, 1, CMP_BLOCK * HD), ((0, 0), (0, 7), (0, 0))),
        cmp_b1=cmp_b1, cmp_w2=cmp_w2.astype(BF16), cmp_b2=cmp_b2,
        nsa_w_q=nsa_w_in[:, :, :nq].astype(BF16),
        nsa_w_g=jnp.pad(nsa_w_in[:, :, nq:], ((0, 0), (0, 0), (0, HD - 3 * NSA_HEADS))).astype(BF16),
        nsa_q_aux=jnp.stack([jnp.tile(tile4(nsa_q_norm[jj]), (nq // 512, 1, 1)) for jj in range(2)]),
        nsa_w_out=nsa_w_out.astype(BF16), rel_bias=rel_bias,
    )
    return P


def kernel(x_prompt, x_sample, state_conv, state_ssm, cache_cmp, cache_sel, cache_win, page_table, mix_norm,
           mlp_norm, mlp_w1, mlp_w2, gdn_w_in, gdn_conv_w, gdn_a_log, gdn_dt_bias, gdn_out_norm, gdn_w_out,
           kv_norm, nsa_w_kv, k_sel_norm, k_win_norm, k_cmp_norm, cmp_pe, cmp_w1, cmp_b1, cmp_w2, cmp_b2,
           nsa_w_in, nsa_q_norm, nsa_w_out, rel_bias):
    bp, tp, _ = x_prompt.shape
    bs, ts, _ = x_sample.shape
    n_lay = gdn_w_in.shape[0]
    P = _prepare_params(mix_norm, mlp_norm, mlp_w1, mlp_w2, gdn_w_in, gdn_conv_w, gdn_a_log, gdn_dt_bias,
                        gdn_out_norm, gdn_w_out, kv_norm, nsa_w_kv, k_sel_norm, k_win_norm, k_cmp_norm, cmp_pe,
                        cmp_w1, cmp_b1, cmp_w2, cmp_b2, nsa_w_in, nsa_q_norm, nsa_w_out, rel_bias)

    conv0 =jnp.zeros((n_lay, bp, GDN_CONV - 1, GDN_CONV_DIM), F32)
    ssm0 = jnp.zeros((n_lay, bp, GDN_V_HEADS, HD, HD), F32)
    yp, conv_p, ssm_p, cmp_p, sel_p, win_p = _forward_group(
        x_prompt.reshape(bp * tp, D_MODEL), conv0, ssm0, None, P,
        batch=bp, seq=tp, t_valid=tp, q_pos0=0, tq=128)

    seq_s = GDN_CHUNK
    xs = jnp.pad(x_sample, ((0, 0), (0, seq_s - ts), (0, 0))).reshape(bs * seq_s, D_MODEL)
    n_pool = cache_cmp.shape[0]
    past = dict(cmp_rows=cache_cmp.reshape(n_pool * PAGE * 2 * NSA_KV, HD),
                sel_rows=cache_sel.reshape(n_pool * PAGE * 2 * NSA_KV, HD),
                page_table=page_table.astype(jnp.int32),
                win=cache_win.reshape(bs, cache_win.shape[1], 2 * NSA_KV * HD))
    n_past = page_table.shape[1] * PAGE
    ys, conv_s, ssm_s, cmp_s, sel_s, win_s = _forward_group(
        xs, state_conv, state_ssm, past, P, batch=bs, seq=seq_s, t_valid=ts, q_pos0=n_past, tq=8)
    y_sample = ys.reshape(bs, seq_s, D_MODEL)[:, :ts]
    return (yp.reshape(bp, tp, D_MODEL), y_sample, conv_p, ssm_p, cmp_p, sel_p, win_p,
            conv_s, ssm_s, cmp_s, sel_s, win_s)
```

```python
import functools
import math

import jax
import jax.numpy as jnp
from jax import lax
from jax.experimental import pallas as pl
from jax.experimental.pallas import tpu as pltpu

F32 = jnp.float32
BF16 = jnp.bfloat16

D_MODEL = 2048
D_FF = 4 * D_MODEL
NORM_EPS = 1e-6
L2_EPS = 1e-6
PAGE = 128

HD = 128
GDN_QK_HEADS = 16
GDN_V_HEADS = 32
GDN_KEY_DIM = GDN_QK_HEADS * HD
GDN_VAL_DIM = GDN_V_HEADS * HD
GDN_CONV = 4
GDN_CHUNK = 64
GDN_CONV_DIM = 2 * GDN_KEY_DIM + GDN_VAL_DIM

NSA_HEADS = 16
NSA_KV = 4
NSA_G = NSA_HEADS // NSA_KV
CMP_BLOCK = 32
CMP_STRIDE = 16
SEL_BLOCK = 64
SEL_TOPK = 16
WINDOW = 512
REL_BUCKETS = 32
REL_MAX_DIST = 4096
NEG = -1e30

TK = 128
N_DELTA = REL_MAX_DIST // TK + 2

VMEM_LIMIT = 56 * 1024 * 1024


def _cparams(sem):
    return pltpu.CompilerParams(dimension_semantics=sem, vmem_limit_bytes=VMEM_LIMIT)


def _sigmoid(x):
    return 1.0 / (1.0 + jnp.exp(-x))


def _softplus(x):
    return jnp.maximum(x, 0.0) + jnp.log(1.0 + jnp.exp(-jnp.abs(x)))


def _dot(a, b):
    return jnp.dot(a.astype(BF16), b.astype(BF16), preferred_element_type=F32)


def _dot_nt(a, b):
    return lax.dot_general(a.astype(BF16), b.astype(BF16), (((1,), (1,)), ((), ())),
                           preferred_element_type=F32)


def _dot_tn(a, b):
    return lax.dot_general(a.astype(BF16), b.astype(BF16), (((0,), (0,)), ((), ())),
                           preferred_element_type=F32)


def _headnorm(acc, gw):
    parts = []
    for g in range(acc.shape[1] // HD):
        a = acc[:, g * HD:(g + 1) * HD]
        parts.append(a * lax.rsqrt(jnp.mean(a * a, axis=-1, keepdims=True) + NORM_EPS))
    return jnp.concatenate(parts, axis=1) * gw


def _nmm_kernel(x_ref, nw_ref, w_ref, aux_ref, o_ref, h_ref, *, mode, norm_tiles, scale, seq, t_valid):
    i = pl.program_id(0)
    j = pl.program_id(1)

    @pl.when(j == 0)
    def _():
        x = x_ref[...]
        h = x * lax.rsqrt(jnp.mean(x * x, axis=-1, keepdims=True) + NORM_EPS) * nw_ref[...]
        h_ref[...] = h.astype(BF16)

    acc = jnp.dot(h_ref[...], w_ref[...], preferred_element_type=F32)
    if mode == "plain":
        o_ref[...] = acc.astype(o_ref.dtype)
    elif mode == "headnorm":
        if norm_tiles is None:
            o_ref[...] = (_headnorm(acc, aux_ref[0]) * scale).astype(o_ref.dtype)
        else:
            is_n = functools.reduce(jnp.logical_or, [j == t for t in norm_tiles])

            @pl.when(is_n)
            def _():
                o_ref[...] = (_headnorm(acc, aux_ref[0]) * scale).astype(o_ref.dtype)

            @pl.when(jnp.logical_not(is_n))
            def _():
                o_ref[...] = acc.astype(o_ref.dtype)
    elif mode == "sigmoid":
        o_ref[...] = _sigmoid(acc)
    elif mode == "gdn_gate":
        tm = acc.shape[0]
        aux = aux_ref[0]
        lane = lax.broadcasted_iota(jnp.int32, acc.shape, 1)
        row = lax.broadcasted_iota(jnp.int32, acc.shape, 0) + i * tm
        live = (row % seq) < t_valid
        beta = jnp.where(live, _sigmoid(acc), 0.0)
        g = jnp.where(live, -jnp.exp(aux[0:1, :]) * _softplus(acc + aux[1:2, :]), 0.0)
        g = jnp.where((lane >= GDN_V_HEADS) & (lane < 2 * GDN_V_HEADS), g, 0.0)
        r = lax.broadcasted_iota(jnp.int32, (tm, tm), 0)
        c = lax.broadcasted_iota(jnp.int32, (tm, tm), 1)
        tri = ((r // GDN_CHUNK) == (c // GDN_CHUNK)) & (c <= r)
        gcum = jnp.dot(jnp.where(tri, 1.0, 0.0), g, preferred_element_type=F32,
                       precision=lax.Precision.HIGHEST)
        o_ref[...] = jnp.where(lane < GDN_V_HEADS, beta, gcum)
    else:
        raise ValueError(mode)


def _nmm(x, nw, w, *, tn, out_dtype=F32, mode="plain", aux=None, norm_tiles=None, scale=1.0,
         seq=1, t_valid=1, tm=512):
    M, K = x.shape
    N = w.shape[1]
    tm = min(tm, M)
    assert M % tm == 0 and N % tn == 0
    if aux is None:
        aux = jnp.zeros((N // tn, 1, tn), F32)
    kern = functools.partial(_nmm_kernel, mode=mode, norm_tiles=norm_tiles, scale=scale, seq=seq,
                             t_valid=t_valid)
    return pl.pallas_call(
        kern,
        grid=(M // tm, N // tn),
        in_specs=[
            pl.BlockSpec((tm, K), lambda i, j: (i, 0)),
            pl.BlockSpec((1, K), lambda i, j: (0, 0)),
            pl.BlockSpec((K, tn), lambda i, j: (0, j)),
            pl.BlockSpec((1,) + aux.shape[1:], lambda i, j: (j, 0, 0)),
        ],
        out_specs=pl.BlockSpec((tm, tn), lambda i, j: (i, j)),
        out_shape=jax.ShapeDtypeStruct((M, N), out_dtype),
        scratch_shapes=[pltpu.VMEM((tm, K), BF16)],
        compiler_params=_cparams(("parallel", "arbitrary")),
        name="nmm_" + mode,
    )(x, nw.reshape(1, K), w, aux)


def _mmres_kernel(x_ref, w_ref, r_ref, o_ref):
    o_ref[...] = r_ref[...] + jnp.dot(x_ref[...], w_ref[...], preferred_element_type=F32)


def _mm_res(x, w, res, *, tm=512, tn=512):
    M, K = x.shape
    N = w.shape[1]
    tm = min(tm, M)
    return pl.pallas_call(
        _mmres_kernel,
        grid=(M // tm, N // tn),
        in_specs=[
            pl.BlockSpec((tm, K), lambda i, j: (i, 0)),
            pl.BlockSpec((K, tn), lambda i, j: (0, j)),
            pl.BlockSpec((tm, tn), lambda i, j: (i, j)),
        ],
        out_specs=pl.BlockSpec((tm, tn), lambda i, j: (i, j)),
        out_shape=jax.ShapeDtypeStruct((M, N), F32),
        compiler_params=_cparams(("parallel", "arbitrary")),
        name="mm_res",
    )(x, w, res)


def _mlp_kernel(x_ref, nw_ref, w1_ref, w2_ref, o_ref, h_ref, acc_ref):
    f = pl.program_id(1)

    @pl.when(f == 0)
    def _():
        x = x_ref[...]
        h = x * lax.rsqrt(jnp.mean(x * x, axis=-1, keepdims=True) + NORM_EPS) * nw_ref[...]
        h_ref[...] = h.astype(BF16)
        acc_ref[...] = x

    a = jnp.maximum(jnp.dot(h_ref[...], w1_ref[...], preferred_element_type=F32), 0.0)
    acc_ref[...] += jnp.dot((a * a).astype(BF16), w2_ref[...], preferred_element_type=F32)

    @pl.when(f == pl.num_programs(1) - 1)
    def _():
        o_ref[...] = acc_ref[...]


def _mlp(x, nw, w1, w2, *, tm=512, tf=1024):
    M, D = x.shape
    Fdim = w1.shape[1]
    tm = min(tm, M)
    return pl.pallas_call(
        _mlp_kernel,
        grid=(M // tm, Fdim // tf),
        in_specs=[
            pl.BlockSpec((tm, D), lambda i, f: (i, 0)),
            pl.BlockSpec((1, D), lambda i, f: (0, 0)),
            pl.BlockSpec((D, tf), lambda i, f: (0, f)),
            pl.BlockSpec((tf, D), lambda i, f: (f, 0)),
        ],
        out_specs=pl.BlockSpec((tm, D), lambda i, f: (i, 0)),
        out_shape=jax.ShapeDtypeStruct((M, D), F32),
        scratch_shapes=[pltpu.VMEM((tm, D), BF16), pltpu.VMEM((tm, D), F32)],
        compiler_params=_cparams(("parallel", "arbitrary")),
        name="mlp",
    )(x, nw.reshape(1, D), w1, w2)


def _unit_lower_inverse(mats, r, c):
    eye = jnp.where(r == c, 1.0, 0.0)
    in8 = (r // 8) == (c // 8)
    d0 = [jnp.where(in8, a, 0.0) for a in mats]
    d2 = [_dot(d, d) for d in d0]
    d4 = [_dot(d, d) for d in d2]
    x = [_dot(eye - a, eye + b) for a, b in zip(d0, d2)]
    x = [_dot(a, eye + b) for a, b in zip(x, d4)]
    s = 8
    while s < GDN_CHUNK:
        off = ((r // (2 * s)) == (c // (2 * s))) & ((r // s) != (c // s))
        bx = [_dot(jnp.where(off, a, 0.0), xi) for a, xi in zip(mats, x)]
        xbx = [_dot(xi, b) for xi, b in zip(x, bx)]
        x = [xi - b for xi, b in zip(x, xbx)]
        s *= 2
    return x


GDN_PAIRS = 8


def _gdn_chunk(qn, kn, vc, zb, beta, gc, st, onw):
    C = GDN_CHUNK
    R = 2 * C
    n = len(qn)
    rcol = lax.broadcasted_iota(jnp.int32, (R, 1), 0)
    top = rcol < C
    r = lax.broadcasted_iota(jnp.int32, (R, R), 0)
    c = lax.broadcasted_iota(jnp.int32, (R, R), 1)
    same = (r // C) == (c // C)
    low = same & (c <= r)
    slow = same & (c < r)
    srow = lax.broadcasted_iota(jnp.int32, (2 * HD, 1), 0)

    def blocked(a):
        return jnp.concatenate([jnp.where(top, a, 0.0), jnp.where(top, 0.0, a)], axis=1)

    beta2 = [jnp.concatenate(b, axis=0) for b in beta]
    gc2 = [jnp.concatenate(g, axis=0) for g in gc]
    gl2 = [jnp.where(top, g[0][C - 1:C, :], g[1][C - 1:C, :]) for g in gc]
    gls = [jnp.exp(jnp.where(srow < HD, g[0][C - 1:C, :], g[1][C - 1:C, :])) for g in gc]
    dec = []
    for g2 in gc2:
        colm = jnp.broadcast_to(g2, (R, R))
        dec.append(jnp.where(low, jnp.exp(jnp.where(low, colm - colm.T, 0.0)), 0.0))
    k2 = [jnp.concatenate([k, k], axis=0) for k in kn]
    q2 = [jnp.concatenate([q, q], axis=0) for q in qn]
    v2 = [jnp.concatenate([v[:, :HD], v[:, HD:]], axis=0) for v in vc]
    kk = [_dot_nt(k, k) for k in k2]
    qk = [_dot_nt(q, k) for q, k in zip(q2, k2)]
    amat = [jnp.where(slow, kk[i] * beta2[i] * dec[i], 0.0) for i in range(n)]
    attn = [qk[i] * dec[i] for i in range(n)]
    tinv = _unit_lower_inverse(amat, r, c)

    e2 = [jnp.exp(g) for g in gc2]
    rhs = [jnp.concatenate([v2[i] * beta2[i], k2[i] * beta2[i] * e2[i]], axis=1) for i in range(n)]
    sol = [_dot(tinv[i], rhs[i]) for i in range(n)]
    lhs = [jnp.concatenate([blocked(sol[i][:, HD:]), blocked(q2[i] * e2[i])], axis=0) for i in range(n)]
    ws = [_dot(lhs[i], st[i]) for i in range(n)]
    vnew = [sol[i][:, :HD] - ws[i][:R] for i in range(n)]
    av = [_dot(attn[i], vnew[i]) for i in range(n)]
    kd = [blocked(k2[i] * jnp.exp(gl2[i] - gc2[i])) for i in range(n)]
    kv = [_dot_tn(kd[i], vnew[i]) for i in range(n)]
    st_new = [st[i] * gls[i] + kv[i] for i in range(n)]

    outs = []
    for i in range(n):
        o2 = ws[i][R:] + av[i]
        z2 = jnp.concatenate([zb[i][:, :HD], zb[i][:, HD:]], axis=0)
        on = o2 * lax.rsqrt(jnp.mean(o2 * o2, axis=-1, keepdims=True) + NORM_EPS) * onw
        out2 = on * (z2 * _sigmoid(z2))
        outs.append(jnp.concatenate([out2[:C], out2[C:]], axis=1))
    return outs, st_new


def _gdn_kernel(q_ref, k_ref, v_ref, z_ref, bg_ref, wq_ref, wk_ref, wv_ref, cq_ref, ck_ref, cv_ref,
                s0_ref, onw_ref, o_ref, sout_ref, st_ref, bq_ref, bk_ref, bv_ref, *, single_chunk):
    C = GDN_CHUNK
    G = GDN_PAIRS
    jg = pl.program_id(1)
    ch = pl.program_id(2)

    def load_state():
        for p in range(G):
            st_ref[p, 0:HD, :] = s0_ref[0, 2 * p]
            st_ref[p, HD:2 * HD, :] = s0_ref[0, 2 * p + 1]
        bq_ref[5:8, :] = cq_ref[0]
        bk_ref[5:8, :] = ck_ref[0]
        bv_ref[5:8, :] = cv_ref[0]

    if single_chunk:
        load_state()
    else:
        pl.when(ch == 0)(load_state)

    def conv_silu(x_ref, buf_ref, w_ref):
        buf_ref[8:8 + C, :] = x_ref[...]
        w = w_ref[...]
        y = w[0:1, :] * buf_ref[5:5 + C, :]
        for t in range(1, GDN_CONV):
            y = y + w[t:t + 1, :] * buf_ref[5 + t:5 + t + C, :]
        buf_ref[5:8, :] = buf_ref[5 + C:8 + C, :]
        return y * _sigmoid(y)

    qc = conv_silu(q_ref, bq_ref, wq_ref)
    kc = conv_silu(k_ref, bk_ref, wk_ref)
    vc = conv_silu(v_ref, bv_ref, wv_ref)
    zb = z_ref[...]
    bg = bg_ref[...]
    lane = lax.broadcasted_iota(jnp.int32, bg.shape, 1)

    def col(idx):
        return jnp.sum(jnp.where(lane == idx, bg, 0.0), axis=-1, keepdims=True)

    qn, kn, beta, gc = [], [], [], []
    for p in range(G):
        qp = qc[:, p * HD:(p + 1) * HD]
        kp = kc[:, p * HD:(p + 1) * HD]
        qn.append(qp * lax.rsqrt(jnp.sum(qp * qp, axis=-1, keepdims=True) + L2_EPS) * (HD ** -0.5))
        kn.append(kp * lax.rsqrt(jnp.sum(kp * kp, axis=-1, keepdims=True) + L2_EPS))
        head = 2 * (jg * G + p)
        beta.append((col(head), col(head + 1)))
        gc.append((col(GDN_V_HEADS + head), col(GDN_V_HEADS + head + 1)))
    outs, new_states = _gdn_chunk(
        qn, kn, [vc[:, 2 * p * HD:2 * (p + 1) * HD] for p in range(G)],
        [zb[:, 2 * p * HD:2 * (p + 1) * HD] for p in range(G)], beta, gc,
        [st_ref[p] for p in range(G)], onw_ref[...])
    for p in range(G):
        st_ref[p] = new_states[p]
    o_ref[...] = jnp.concatenate(outs, axis=1).astype(o_ref.dtype)

    def write_state():
        for p in range(G):
            sout_ref[0, 2 * p] = new_states[p][:HD]
            sout_ref[0, 2 * p + 1] = new_states[p][HD:]

    if single_chunk:
        write_state()
    else:
        pl.when(ch == pl.num_programs(2) - 1)(write_state)


def _gdn(proj, bg, conv_w, conv0, ssm0, out_norm, *, batch, seq):
    C = GDN_CHUNK
    G = GDN_PAIRS
    nch = seq // C
    ng = GDN_QK_HEADS // G
    row = lambda b, j, c: b * nch + c
    return pl.pallas_call(
        functools.partial(_gdn_kernel, single_chunk=nch == 1),
        grid=(batch, ng, nch),
        in_specs=[
            pl.BlockSpec((C, G * HD), lambda b, j, c: (row(b, j, c), j)),
            pl.BlockSpec((C, G * HD), lambda b, j, c: (row(b, j, c), ng + j)),
            pl.BlockSpec((C, 2 * G * HD), lambda b, j, c: (row(b, j, c), ng + j)),
            pl.BlockSpec((C, 2 * G * HD), lambda b, j, c: (row(b, j, c), 2 * ng + j)),
            pl.BlockSpec((C, HD), lambda b, j, c: (row(b, j, c), 0)),
            pl.BlockSpec((GDN_CONV, G * HD), lambda b, j, c: (0, j)),
            pl.BlockSpec((GDN_CONV, G * HD), lambda b, j, c: (0, ng + j)),
            pl.BlockSpec((GDN_CONV, 2 * G * HD), lambda b, j, c: (0, ng + j)),
            pl.BlockSpec((1, GDN_CONV - 1, G * HD), lambda b, j, c: (b, 0, j)),
            pl.BlockSpec((1, GDN_CONV - 1, G * HD), lambda b, j, c: (b, 0, ng + j)),
            pl.BlockSpec((1, GDN_CONV - 1, 2 * G * HD), lambda b, j, c: (b, 0, ng + j)),
            pl.BlockSpec((1, 2 * G, HD, HD), lambda b, j, c: (b, j, 0, 0)),
            pl.BlockSpec((1, HD), lambda b, j, c: (0, 0)),
        ],
        out_specs=[
            pl.BlockSpec((C, 2 * G * HD), lambda b, j, c: (row(b, j, c), j)),
            pl.BlockSpec((1, 2 * G, HD, HD), lambda b, j, c: (b, j, 0, 0)),
        ],
        out_shape=[
            jax.ShapeDtypeStruct((batch * seq, GDN_VAL_DIM), BF16),
            jax.ShapeDtypeStruct((batch, GDN_V_HEADS, HD, HD), F32),
        ],
        scratch_shapes=[
            pltpu.VMEM((G, 2 * HD, HD), F32),
            pltpu.VMEM((8 + C, G * HD), F32),
            pltpu.VMEM((8 + C, G * HD), F32),
            pltpu.VMEM((8 + C, 2 * G * HD), F32),
        ],
        compiler_params=_cparams(("parallel", "parallel", "arbitrary")),
        name="gdn",
    )(proj, proj, proj, proj, bg, conv_w, conv_w, conv_w, conv0, conv0, conv0, ssm0,
      out_norm.reshape(1, HD))


CMP_PPS = 8


def _cmp1_kernel(pt_ref, *refs, row_packed):
    page_refs = refs[:CMP_PPS]
    w_ref = refs[CMP_PPS]
    o_ref = refs[CMP_PPS + 1]
    nseg = PAGE // CMP_STRIDE
    nch = 2 * NSA_KV
    pr = lax.broadcasted_iota(jnp.int32, (PAGE, PAGE), 0)
    pc = lax.broadcasted_iota(jnp.int32, (PAGE, PAGE), 1)
    perm = jnp.where(pc == (pr % nseg) * CMP_STRIDE + pr // nseg, 1.0, 0.0).astype(BF16)

    def slab(p, ch):
        if row_packed:
            return p[pl.ds(ch, PAGE, stride=nch), :]
        return p[0, :, ch * HD:(ch + 1) * HD]

    perm_slabs = [[jnp.dot(perm, slab(p, ch).astype(BF16), preferred_element_type=F32) for ch in range(nch)]
                  for p in page_refs]
    for cc in range(2):
        acc = jnp.zeros((NSA_KV * CMP_PPS * nseg, 2 * HD), F32)
        for rp in range(CMP_STRIDE // 2):
            lhs = jnp.concatenate(
                [jnp.concatenate([ps[cc * NSA_KV + h][(2 * rp) * nseg:(2 * rp + 1) * nseg],
                                  ps[cc * NSA_KV + h][(2 * rp + 1) * nseg:(2 * rp + 2) * nseg]], axis=1)
                 for h in range(NSA_KV) for ps in perm_slabs], axis=0)
            acc = acc + jnp.dot(lhs.astype(BF16), w_ref[cc, rp], preferred_element_type=F32)
        o_ref[0, cc] = acc


def _cmp_stage1(pages, ptab, w1cat, *, row_packed):
    n = ptab.shape[0]
    nst = n // CMP_PPS
    nseg = PAGE // CMP_STRIDE
    if row_packed:
        specs = [pl.BlockSpec((PAGE * 2 * NSA_KV, HD), lambda s, pt, p=p: (pt[s * CMP_PPS + p], 0))
                 for p in range(CMP_PPS)]
    else:
        specs = [pl.BlockSpec((1, PAGE, 2 * NSA_KV * HD), lambda s, pt, p=p: (pt[s * CMP_PPS + p], 0, 0))
                 for p in range(CMP_PPS)]
    grid_spec = pltpu.PrefetchScalarGridSpec(
        num_scalar_prefetch=1,
        grid=(nst,),
        in_specs=specs + [pl.BlockSpec((2, CMP_STRIDE // 2, 2 * HD, 2 * HD), lambda s, pt: (0, 0, 0, 0))],
        out_specs=pl.BlockSpec((1, 2, NSA_KV * CMP_PPS * nseg, 2 * HD), lambda s, pt: (s, 0, 0, 0)),
    )
    return pl.pallas_call(
        functools.partial(_cmp1_kernel, row_packed=row_packed),
        grid_spec=grid_spec,
        out_shape=jax.ShapeDtypeStruct((nst, 2, NSA_KV * CMP_PPS * nseg, 2 * HD), F32),
        compiler_params=_cparams(("arbitrary",)),
        name="cmp_stage1",
    )(ptab, *([pages] * CMP_PPS), w1cat.reshape(2, CMP_STRIDE // 2, 2 * HD, 2 * HD))


def _cmp2_kernel(a_ref, b_ref, pe_ref, w1_ref, b1_ref, w2_ref, b2_ref, nw_ref, o_ref):
    cc = pl.program_id(0)
    pe = pe_ref[0]
    pec = jnp.dot(pe.astype(BF16), w1_ref[0], preferred_element_type=F32)[0:1, :]
    hid = a_ref[0] + b_ref[0] + pec + b1_ref[0]
    hid = hid * _sigmoid(hid)
    out = jnp.dot(hid.astype(BF16), w2_ref[0], preferred_element_type=F32) + b2_ref[0]

    @pl.when(cc == 0)
    def _():
        o_ref[0] = out * lax.rsqrt(jnp.mean(out * out, axis=-1, keepdims=True) + NORM_EPS) * nw_ref[...]

    @pl.when(cc != 0)
    def _():
        o_ref[0] = out


def _cmp_stage2(a, b, pe8, w1flat, b1, w2, b2, nw, *, tr):
    R = a.shape[1]
    return pl.pallas_call(
        _cmp2_kernel,
        grid=(2, R // tr),
        in_specs=[
            pl.BlockSpec((1, tr, HD), lambda c, i: (c, i, 0)),
            pl.BlockSpec((1, tr, HD), lambda c, i: (c, i, 0)),
            pl.BlockSpec((1, 8, CMP_BLOCK * HD), lambda c, i: (c, 0, 0)),
            pl.BlockSpec((1, CMP_BLOCK * HD, HD), lambda c, i: (c, 0, 0)),
            pl.BlockSpec((1, 1, HD), lambda c, i: (c, 0, 0)),
            pl.BlockSpec((1, HD, HD), lambda c, i: (c, 0, 0)),
            pl.BlockSpec((1, 1, HD), lambda c, i: (c, 0, 0)),
            pl.BlockSpec((1, HD), lambda c, i: (0, 0)),
        ],
        out_specs=pl.BlockSpec((1, tr, HD), lambda c, i: (c, i, 0)),
        out_shape=jax.ShapeDtypeStruct((2, R, HD), F32),
        compiler_params=_cparams(("arbitrary", "arbitrary")),
        name="cmp_stage2",
    )(a, b, pe8, w1flat, b1.reshape(2, 1, HD), w2, b2.reshape(2, 1, HD), nw.reshape(1, HD))


def _stack_heads(qb):
    return jnp.concatenate([qb[:, g * HD:(g + 1) * HD] for g in range(NSA_G)], axis=0)


def _unstack_heads(o, tq):
    return jnp.concatenate([o[g * tq:(g + 1) * tq] for g in range(NSA_G)], axis=1)


def _gate_rows(gt):
    return jnp.concatenate([gt[:, g:g + 1] for g in range(NSA_G)], axis=0)


def _attn_cmp_kernel(q_ref, kc_ref, vc_ref, bias_ref, gate_ref, o_ref, sel_ref, *idx_ref, tq, q_pos0, nc, ns, nsp,
                     n_idx, sel_t):
    i = pl.program_id(2)
    ncp = kc_ref.shape[2]
    qs = _stack_heads(q_ref[...])
    logits = _dot_nt(qs, kc_ref[0, 0])
    logits = logits + jnp.concatenate([bias_ref[0, g] for g in range(NSA_G)], axis=0)
    rows = NSA_G * tq
    t4 = q_pos0 + i * tq + lax.broadcasted_iota(jnp.int32, (rows, ncp), 0) % tq
    cidx = lax.broadcasted_iota(jnp.int32, (rows, ncp), 1)
    mask = (cidx * CMP_STRIDE + (CMP_BLOCK - 1) <= t4) & (cidx < nc)
    lg = jnp.where(mask, logits, NEG)
    mx = jnp.max(lg, axis=-1, keepdims=True)
    ex = jnp.exp(lg - mx)
    p = ex / jnp.sum(ex, axis=-1, keepdims=True) * jnp.where(mask, 1.0, 0.0)
    oc = _dot(p, vc_ref[0, 0])
    o_ref[...] = _unstack_heads(oc * _gate_rows(gate_ref[0, 0, 0]), tq)

    psum = p[0:tq]
    for g in range(1, NSA_G):
        psum = psum + p[g * tq:(g + 1) * tq]
    top_k = min(SEL_TOPK, ns)
    if sel_t:
        nsr = sel_ref.shape[2]
        sr = lax.broadcasted_iota(jnp.int32, (nsr, ncp), 0)
        cc = lax.broadcasted_iota(jnp.int32, (nsr, ncp), 1)
        hit = (cc * CMP_STRIDE < sr * SEL_BLOCK + SEL_BLOCK) & (cc * CMP_STRIDE + CMP_BLOCK > sr * SEL_BLOCK)
        c2s_t = jnp.where(hit & (cc < nc) & (sr < ns), 1.0, 0.0)
        imp_t = jnp.dot(c2s_t, psum.T, preferred_element_type=F32, precision=lax.Precision.HIGHEST)
        tt = q_pos0 + i * tq + lax.broadcasted_iota(jnp.int32, (nsr, tq), 1)
        st = lax.broadcasted_iota(jnp.int32, (nsr, tq), 0)
        cur_t = tt // SEL_BLOCK
        forced_t = (st == 0) | (st == cur_t) | (st == cur_t - 1)
        score_t = jnp.where(forced_t, NSA_G + 1.0, jnp.where(st * SEL_BLOCK <= tt, imp_t, -1.0))
        score_t = jnp.where(st < ns, score_t, -2.0)
        rank_t = jnp.zeros((nsr, tq), F32)
        for sp in range(ns):
            other = score_t[sp:sp + 1, :]
            rank_t = rank_t + jnp.where(other > score_t, 1.0, jnp.where((other == score_t) & (sp < st), 1.0, 0.0))
        sel_ref[0, 0] = jnp.where((rank_t < top_k) & (st < ns), 1.0, 0.0)
        return
    cr = lax.broadcasted_iota(jnp.int32, (ncp, nsp), 0)
    sc = lax.broadcasted_iota(jnp.int32, (ncp, nsp), 1)
    c2s = (cr * CMP_STRIDE < sc * SEL_BLOCK + SEL_BLOCK) & (cr * CMP_STRIDE + CMP_BLOCK > sc * SEL_BLOCK)
    c2s = jnp.where(c2s & (cr < nc) & (sc < ns), 1.0, 0.0)
    imp = jnp.dot(psum, c2s, preferred_element_type=F32, precision=lax.Precision.HIGHEST)

    t = q_pos0 + i * tq + lax.broadcasted_iota(jnp.int32, (tq, nsp), 0)
    s = lax.broadcasted_iota(jnp.int32, (tq, nsp), 1)
    cur = t // SEL_BLOCK
    forced = (s == 0) | (s == cur) | (s == cur - 1)
    valid = s * SEL_BLOCK <= t
    score = jnp.where(forced, NSA_G + 1.0, jnp.where(valid, imp, -1.0))
    score = jnp.where(s < ns, score, -2.0)
    rank = jnp.zeros((tq, nsp), F32)
    for sp in range(ns):
        other = score[:, sp:sp + 1]
        ahead = (other > score) | ((other == score) & (sp < s))
        rank = rank + jnp.where(ahead, 1.0, 0.0)
    picked = (rank < top_k) & (s < ns)
    sel_ref[0, 0] = jnp.where(picked, 1.0, 0.0)
    if n_idx:
        listed = jnp.where(picked & (s < ns - 1), 1.0, 0.0)
        before = _dot(listed, jnp.where(lax.broadcasted_iota(jnp.int32, (nsp, nsp), 0)
                                        < lax.broadcasted_iota(jnp.int32, (nsp, nsp), 1), 1.0, 0.0))
        lane = lax.broadcasted_iota(jnp.int32, (tq, HD), 1)
        sf = s.astype(F32)
        out = jnp.zeros((tq, HD), F32)
        for kk in range(n_idx):
            hit = (listed > 0.5) & (before == float(kk))
            out = out + jnp.where(lane == kk, jnp.sum(jnp.where(hit, sf, 0.0), axis=-1, keepdims=True), 0.0)
        idx_ref[0][0, 0] = out.astype(jnp.int32)


def _attn_cmp(q, kcvc, bias_c, gate, *, batch, seq, tq, q_pos0, nc, ns, nsp, n_idx=0, sel_t=False):
    nqt = seq // tq
    ncp = kcvc.shape[3]
    kern = functools.partial(_attn_cmp_kernel, tq=tq, q_pos0=q_pos0, nc=nc, ns=ns, nsp=nsp, n_idx=n_idx,
                             sel_t=sel_t)
    rows_per_b = seq // tq
    nsr = -(-ns // 8) * 8
    if sel_t:
        sel_spec = pl.BlockSpec((1, 1, nsr, tq), lambda b, h, i: (b, h, 0, i))
        sel_shape = jax.ShapeDtypeStruct((batch, NSA_KV, nsr, seq), F32)
    else:
        sel_spec = pl.BlockSpec((1, 1, tq, nsp), lambda b, h, i: (b, h, i, 0))
        sel_shape = jax.ShapeDtypeStruct((batch, NSA_KV, seq, nsp), F32)
    extra_specs = [pl.BlockSpec((1, 1, tq, HD), lambda b, h, i: (b, h, i, 0))] if n_idx else []
    extra_shapes = [jax.ShapeDtypeStruct((batch, NSA_KV, seq, HD), jnp.int32)] if n_idx else []
    return pl.pallas_call(
        kern,
        grid=(batch, NSA_KV, nqt),
        in_specs=[
            pl.BlockSpec((tq, NSA_G * HD), lambda b, h, i: (b * rows_per_b + i, h)),
            pl.BlockSpec((None, 1, 1, ncp, HD), lambda b, h, i: (0, b, h, 0, 0)),
            pl.BlockSpec((None, 1, 1, ncp, HD), lambda b, h, i: (1, b, h, 0, 0)),
            pl.BlockSpec((1, NSA_G, tq, ncp), lambda b, h, i: (h, 0, i, 0)),
            pl.BlockSpec((1, 1, 1, tq, NSA_G), lambda b, h, i: (0, b, h, i, 0)),
        ],
        out_specs=[
            pl.BlockSpec((tq, NSA_G * HD), lambda b, h, i: (b * rows_per_b + i, h)),
            sel_spec,
        ] + extra_specs,
        out_shape=[jax.ShapeDtypeStruct((batch * seq, NSA_HEADS * HD), F32), sel_shape] + extra_shapes,
        compiler_params=_cparams(("parallel", "parallel", "arbitrary")),
        name="attn_cmp",
    )(q, kcvc, kcvc, bias_c, gate)


def _flash_kernel(pt_ref, *refs, cfg):
    pps, tq, has_tail, use_sel = cfg["pps"], cfg["tq"], cfg["has_tail"], cfg["use_sel"]
    it = iter(refs)
    q_ref = next(it)
    k_refs = [next(it) for _ in range(pps)]
    v_refs = [next(it) for _ in range(pps)]
    b_refs = [next(it) for _ in range(pps)]
    if has_tail:
        kt_ref, vt_ref, bt_ref = next(it), next(it), next(it)
    sel_ref = next(it) if use_sel else None
    gate_ref = next(it)
    prev_ref = next(it)
    o_ref = next(it)
    m_ref, l_ref, acc_ref = next(it), next(it), next(it)

    i = pl.program_id(2)
    st = pl.program_id(3)
    rows = NSA_G * tq

    @pl.when(st == 0)
    def _():
        m_ref[...] = jnp.full((rows, 1), NEG, F32)
        l_ref[...] = jnp.zeros((rows, 1), F32)
        acc_ref[...] = jnp.zeros((rows, HD), F32)

    t0 = cfg["q_pos0"] + i * tq

    def tile(k, v, bias4, kt, p0):
        qs = _stack_heads(q_ref[...])
        s = _dot_nt(qs, k) + jnp.concatenate([bias4[g] for g in range(NSA_G)], axis=0)
        tt = t0 + lax.broadcasted_iota(jnp.int32, (tq, TK), 0)
        pos = p0 + lax.broadcasted_iota(jnp.int32, (tq, TK), 1)
        dist = tt - pos
        ok = dist >= 0
        if use_sel:
            nsp = sel_ref.shape[3]
            sr = lax.broadcasted_iota(jnp.int32, (nsp, TK), 0)
            sc = lax.broadcasted_iota(jnp.int32, (nsp, TK), 1)
            expand = jnp.where(sr == kt * (TK // SEL_BLOCK) + sc // SEL_BLOCK, 1.0, 0.0)
            picked = _dot(sel_ref[0, 0], expand)
            ok = ok & (picked > 0.5)
        else:
            ok = ok & (dist < WINDOW) & (pos >= cfg["w_pos0"])
        okf = jnp.where(ok, 1.0, 0.0)
        ok4 = jnp.concatenate([okf] * NSA_G, axis=0)
        s = jnp.where(ok4 > 0.5, s, NEG)
        m_old = m_ref[...]
        m_new = jnp.maximum(m_old, jnp.max(s, axis=-1, keepdims=True))
        alpha = jnp.exp(m_old - m_new)
        p = jnp.exp(s - m_new) * ok4
        l_ref[...] = alpha * l_ref[...] + jnp.sum(p, axis=-1, keepdims=True)
        acc_ref[...] = alpha * acc_ref[...] + _dot(p, v)
        m_ref[...] = m_new

    for pp in range(pps):
        kt = cfg["tile_of"](i, st, pp)
        active = cfg["active"](i, st, pp)
        p0 = cfg["kbase"] + kt * TK
        if active is True:
            tile(k_refs[pp][0], v_refs[pp][0], b_refs[pp][0, 0], kt, p0)
        else:
            @pl.when(active)
            def _(pp=pp, kt=kt, p0=p0):
                tile(k_refs[pp][0], v_refs[pp][0], b_refs[pp][0, 0], kt, p0)

    if has_tail:
        @pl.when(st == pl.num_programs(3) - 1)
        def _():
            tile(kt_ref[0], vt_ref[0], bt_ref[0, 0], cfg["tail_tile"], cfg["kbase"] + cfg["tail_tile"] * TK)

    @pl.when(st == pl.num_programs(3) - 1)
    def _():
        o = acc_ref[...] / l_ref[...] * _gate_rows(gate_ref[0, 0, 0])
        o_ref[...] = (prev_ref[...] + _unstack_heads(o, tq)).astype(o_ref.dtype)


def _flash(q, pages, ptab, kcol, vcol, bias_tiles, tails, sel, gate, branch, prev, *, batch, seq, tq, q_pos0,
           pps, nsteps, tile_of, active, kbase, w_pos0, npt, tail_tile, out_dtype):
    nqt = seq // tq
    has_tail = tails is not None
    use_sel = sel is not None
    n_delta = bias_tiles.shape[1]
    cfg = dict(pps=pps, tq=tq, has_tail=has_tail, use_sel=use_sel, q_pos0=q_pos0, tile_of=tile_of,
               active=active, kbase=kbase, w_pos0=w_pos0, tail_tile=tail_tile)

    def page_idx(b, i, s, pp, pt):
        kt = jnp.clip(tile_of(i, s, pp), 0, npt - 1)
        return pt[b * npt + kt]

    def didx(i, s, pp):
        kt = tile_of(i, s, pp)
        return jnp.clip((q_pos0 + i * tq - kbase - kt * TK) // TK, 0, n_delta - 1)

    in_specs = [pl.BlockSpec((tq, NSA_G * HD), lambda b, h, i, s, pt: (b * nqt + i, h))]
    args = [q]
    for col in (kcol, vcol):
        for pp in range(pps):
            in_specs.append(pl.BlockSpec(
                (1, TK, HD), lambda b, h, i, s, pt, pp=pp, col=col: (page_idx(b, i, s, pp, pt), 0, col + h)))
            args.append(pages)
    for pp in range(pps):
        in_specs.append(pl.BlockSpec(
            (1, 1, NSA_G, tq, TK), lambda b, h, i, s, pt, pp=pp: (h, didx(i, s, pp), 0, 0, 0)))
        args.append(bias_tiles)
    if has_tail:
        tail_pages, tkcol, tvcol = tails
        tdelta = min(max((q_pos0 - kbase - tail_tile * TK) // TK, 0), n_delta - 1)
        in_specs.append(pl.BlockSpec((1, TK, HD), lambda b, h, i, s, pt: (b, 0, tkcol + h)))
        in_specs.append(pl.BlockSpec((1, TK, HD), lambda b, h, i, s, pt: (b, 0, tvcol + h)))
        in_specs.append(pl.BlockSpec((1, 1, NSA_G, tq, TK), lambda b, h, i, s, pt: (h, tdelta, 0, 0, 0)))
        args += [tail_pages, tail_pages, bias_tiles]
    if use_sel:
        nsp = sel.shape[3]
        in_specs.append(pl.BlockSpec((1, 1, tq, nsp), lambda b, h, i, s, pt: (b, h, i, 0)))
        args.append(sel)
    in_specs.append(pl.BlockSpec((1, 1, 1, tq, NSA_G), lambda b, h, i, s, pt: (branch, b, h, i, 0)))
    args.append(gate)
    in_specs.append(pl.BlockSpec((tq, NSA_G * HD), lambda b, h, i, s, pt: (b * nqt + i, h)))
    args.append(prev)

    rows = NSA_G * tq
    grid_spec = pltpu.PrefetchScalarGridSpec(
        num_scalar_prefetch=1,
        grid=(batch, NSA_KV, nqt, nsteps),
        in_specs=in_specs,
        out_specs=pl.BlockSpec((tq, NSA_G * HD), lambda b, h, i, s, pt: (b * nqt + i, h)),
        scratch_shapes=[pltpu.VMEM((rows, 1), F32), pltpu.VMEM((rows, 1), F32), pltpu.VMEM((rows, HD), F32)],
    )
    return pl.pallas_call(
        functools.partial(_flash_kernel, cfg=cfg),
        grid_spec=grid_spec,
        out_shape=jax.ShapeDtypeStruct((batch * seq, NSA_HEADS * HD), out_dtype),
        compiler_params=_cparams(("parallel", "parallel", "arbitrary", "arbitrary")),
        name="flash_sel" if use_sel else "flash_win",
    )(ptab, *args)


def _flasht_kernel(pt_ref, qi_ref, si_ref, lf_ref, kt_ref, ktc_ref, dd_ref, *refs, cfg):
    pps, tq, use_sel = cfg["pps"], cfg["tq"], cfg["use_sel"]
    it = iter(refs)
    q_ref = next(it)
    k_refs = [next(it) for _ in range(pps)]
    v_refs = [next(it) for _ in range(pps)]
    b_refs = [next(it) for _ in range(pps)]
    sel_ref = next(it) if use_sel else None
    gate_ref = next(it)
    prev_ref = next(it)
    o_ref = next(it)
    qt_ref, m_ref, l_ref, acc_ref = next(it), next(it), next(it), next(it)

    n = pl.program_id(2)
    i = qi_ref[n]
    st = si_ref[n]
    cols = NSA_G * tq

    @pl.when(st == 0)
    def _():
        qb = q_ref[...].astype(F32)
        qt_ref[...] = jnp.concatenate([qb[:, g * HD:(g + 1) * HD].T for g in range(NSA_G)], axis=1).astype(BF16)
        m_ref[...] = jnp.full((1, cols), NEG, F32)
        l_ref[...] = jnp.zeros((1, cols), F32)
        acc_ref[...] = jnp.zeros((HD, cols), F32)

    t0 = cfg["q_pos0"] + i * tq
    qt = qt_ref[...]
    tt = t0 + lax.broadcasted_iota(jnp.int32, (TK, tq), 1)
    krow = lax.broadcasted_iota(jnp.int32, (TK, tq), 0)
    scores, oks = [], []
    for pp in range(pps):
        kt = kt_ref[n * pps + pp]
        pos = cfg["kbase"] + kt * TK + krow
        dist = tt - pos
        ok = dist >= 0
        if use_sel:
            nsr = sel_ref.shape[2]
            kr = lax.broadcasted_iota(jnp.int32, (TK, nsr), 0)
            sc = lax.broadcasted_iota(jnp.int32, (TK, nsr), 1)
            expand = jnp.where(sc == kt * (TK // SEL_BLOCK) + kr // SEL_BLOCK, 1.0, 0.0)
            ok = ok & (_dot(expand, sel_ref[0, 0]) > 0.5)
        else:
            ok = ok & (dist < WINDOW) & (pos >= cfg["w_pos0"])
        ok4 = jnp.concatenate([ok] * NSA_G, axis=1)
        bias = jnp.concatenate([b_refs[pp][0, g, 0] for g in range(NSA_G)], axis=1)
        s = jnp.dot(k_refs[pp][0].astype(BF16), qt, preferred_element_type=F32) + bias
        scores.append(jnp.where(ok4, s, NEG))
        oks.append(ok4)
    m_old = m_ref[...]
    m_new = functools.reduce(jnp.maximum, [jnp.max(s, axis=0, keepdims=True) for s in scores] + [m_old])
    alpha = jnp.exp(m_old - m_new)
    ps = [jnp.where(ok4, jnp.exp(s - m_new), 0.0) for s, ok4 in zip(scores, oks)]
    l_new = alpha * l_ref[...]
    acc = alpha * acc_ref[...]
    for pp in range(pps):
        l_new = l_new + jnp.sum(ps[pp], axis=0, keepdims=True)
        acc = acc + _dot_tn(v_refs[pp][0], ps[pp])
    l_ref[...] = l_new
    acc_ref[...] = acc
    m_ref[...] = m_new

    @pl.when(lf_ref[n] == 1)
    def _():
        gt = gate_ref[0, 0, 0]
        grow = jnp.concatenate([gt[g:g + 1, :] for g in range(NSA_G)], axis=1)
        ot = acc / l_new * grow
        o = jnp.concatenate([ot[:, g * tq:(g + 1) * tq].T for g in range(NSA_G)], axis=1)
        o_ref[...] = (prev_ref[...] + o).astype(o_ref.dtype)


def _flasht(q, pages, ptab, kcol, vcol, bias_t, sel, gate_t, branch, prev, *, batch, seq, tq, q_pos0, pps,
            steps_of, tile_of, kbase, w_pos0, npt, out_dtype):
    nqt = seq // tq
    use_sel = sel is not None
    n_delta = bias_t.shape[2]
    cols = NSA_G * tq
    pairs = [(i, s) for i in range(nqt) for s in range(steps_of(i))]
    qi = jnp.asarray([p[0] for p in pairs], jnp.int32)
    si = jnp.asarray([p[1] for p in pairs], jnp.int32)
    lf = jnp.asarray([int(s == steps_of(i) - 1) for (i, s) in pairs], jnp.int32)
    cfg = dict(pps=pps, tq=tq, use_sel=use_sel, q_pos0=q_pos0, kbase=kbase, w_pos0=w_pos0)
    kt_raw = [tile_of(i, s, pp) for (i, s) in pairs for pp in range(pps)]
    kt = jnp.asarray(kt_raw, jnp.int32)
    ktc = jnp.asarray([min(max(k, 0), npt - 1) for k in kt_raw], jnp.int32)
    dd = jnp.asarray([min(max((q_pos0 + i * tq - kbase - tile_of(i, s, pp) * TK) // TK, 0), n_delta - 1)
                      for (i, s) in pairs for pp in range(pps)], jnp.int32)

    qmap = lambda b, h, n, pt, qi_, *_: (b * nqt + qi_[n], h)
    in_specs = [pl.BlockSpec((tq, NSA_G * HD), qmap)]
    args = [q]
    for col in (kcol, vcol):
        for pp in range(pps):
            in_specs.append(pl.BlockSpec(
                (1, TK, HD),
                lambda b, h, n, pt, qi_, si_, lf_, kt_, ktc_, dd_, pp=pp, col=col:
                (pt[b * npt + ktc_[n * pps + pp]], 0, col + h)))
            args.append(pages)
    for pp in range(pps):
        in_specs.append(pl.BlockSpec(
            (1, NSA_G, 1, TK, tq),
            lambda b, h, n, pt, qi_, si_, lf_, kt_, ktc_, dd_, pp=pp: (h, 0, dd_[n * pps + pp], 0, 0)))
        args.append(bias_t)
    if use_sel:
        nsr = sel.shape[2]
        in_specs.append(pl.BlockSpec((1, 1, nsr, tq), lambda b, h, n, pt, qi_, *_: (b, h, 0, qi_[n])))
        args.append(sel)
    in_specs.append(pl.BlockSpec((1, 1, 1, NSA_G, tq), lambda b, h, n, pt, qi_, *_: (branch, b, h, 0, qi_[n])))
    args.append(gate_t)
    in_specs.append(pl.BlockSpec((tq, NSA_G * HD), qmap))
    args.append(prev)

    grid_spec = pltpu.PrefetchScalarGridSpec(
        num_scalar_prefetch=7,
        grid=(batch, NSA_KV, len(pairs)),
        in_specs=in_specs,
        out_specs=pl.BlockSpec((tq, NSA_G * HD), qmap),
        scratch_shapes=[pltpu.VMEM((HD, cols), BF16), pltpu.VMEM((1, cols), F32), pltpu.VMEM((1, cols), F32),
                        pltpu.VMEM((HD, cols), F32)],
    )
    return pl.pallas_call(
        functools.partial(_flasht_kernel, cfg=cfg),
        grid_spec=grid_spec,
        out_shape=jax.ShapeDtypeStruct((batch * seq, NSA_HEADS * HD), out_dtype),
        compiler_params=_cparams(("parallel", "parallel", "arbitrary")),
        name="flasht_sel" if use_sel else "flasht_win",
    )(ptab, qi, si, lf, kt, ktc, dd, *args)


def _selg_kernel(idx_ref, pt_ref, q_ref, *refs, nblk, tq, q_pos0, tail_pos0):
    kv_refs = refs[:nblk]
    b_refs = refs[nblk:2 * nblk]
    tk_ref, tv_ref, tb_ref, gate_ref, prev_ref, o_ref, osc_ref = refs[2 * nblk:]
    b = pl.program_id(0)
    h = pl.program_id(1)
    qi = pl.program_id(2)
    nq = pl.num_programs(2)
    rows = NSA_G * tq
    nch = 2 * NSA_KV

    @pl.when(qi == 0)
    def _():
        osc_ref[...] = jnp.zeros((rows, HD), F32)

    qs = _stack_heads(q_ref[...])
    t = q_pos0 + qi
    base = ((b * NSA_KV + h) * nq + qi) * nblk
    jj = lax.broadcasted_iota(jnp.int32, (rows, SEL_BLOCK), 1)

    def bias_rows(bref):
        bb = bref[0, 0, 0]
        return jnp.concatenate([jnp.broadcast_to(bb[g:g + 1, :], (tq, SEL_BLOCK)) for g in range(NSA_G)], axis=0)

    scores, vals = [], []
    for n in range(nblk):
        k = kv_refs[n][pl.ds(h, SEL_BLOCK, stride=nch), :]
        v = kv_refs[n][pl.ds(NSA_KV + h, SEL_BLOCK, stride=nch), :]
        s = _dot_nt(qs, k) + bias_rows(b_refs[n])
        pos = idx_ref[base + n] * SEL_BLOCK + jj
        scores.append(jnp.where(pos <= t, s, NEG))
        vals.append(v)
    s = _dot_nt(qs, tk_ref[0]) + bias_rows(tb_ref)
    scores.append(jnp.where(tail_pos0 + jj <= t, s, NEG))
    vals.append(tv_ref[0])

    m = functools.reduce(jnp.maximum, [jnp.max(s, axis=-1, keepdims=True) for s in scores])
    l = jnp.zeros((rows, 1), F32)
    acc = jnp.zeros((rows, HD), F32)
    for s, v in zip(scores, vals):
        p = jnp.exp(s - m)
        l = l + jnp.sum(p, axis=-1, keepdims=True)
        acc = acc + _dot(p, v)
    rowq = lax.broadcasted_iota(jnp.int32, (rows, 1), 0) % tq
    osc = jnp.where(rowq == qi, acc / l, osc_ref[...])
    osc_ref[...] = osc

    @pl.when(qi == nq - 1)
    def _():
        o_ref[...] = prev_ref[...] + _unstack_heads(osc * _gate_rows(gate_ref[0, 0, 0]), tq)


def _sel_gather(q, cache_rows, ptab, idx, bias_blk, tail, gate, prev, *, batch, tq, nq, q_pos0, npt, ns):
    nblk = idx.shape[0] // (batch * NSA_KV * nq)
    half = SEL_BLOCK * 2 * NSA_KV
    per_page = PAGE // SEL_BLOCK

    def blk(b, h, qi, n, idx_ref):
        return idx_ref[((b * NSA_KV + h) * nq + qi) * nblk + n]

    def kv_map(n):
        def f(b, h, qi, idx_ref, pt_ref):
            s = blk(b, h, qi, n, idx_ref)
            return (pt_ref[b * npt + s // per_page] * per_page + s % per_page, 0)
        return f

    in_specs = [pl.BlockSpec((tq, NSA_G * HD), lambda b, h, qi, i_, p_: (b, h))]
    in_specs += [pl.BlockSpec((half, HD), kv_map(n)) for n in range(nblk)]
    in_specs += [pl.BlockSpec((1, 1, 1, NSA_G, SEL_BLOCK),
                              lambda b, h, qi, i_, p_, n=n: (qi, h, blk(b, h, qi, n, i_), 0, 0)) for n in range(nblk)]
    in_specs += [
        pl.BlockSpec((1, SEL_BLOCK, HD), lambda b, h, qi, i_, p_: (b, 0, h)),
        pl.BlockSpec((1, SEL_BLOCK, HD), lambda b, h, qi, i_, p_: (b, 0, NSA_KV + h)),
        pl.BlockSpec((1, 1, 1, NSA_G, SEL_BLOCK), lambda b, h, qi, i_, p_: (qi, h, ns - 1, 0, 0)),
        pl.BlockSpec((1, 1, 1, tq, NSA_G), lambda b, h, qi, i_, p_: (1, b, h, 0, 0)),
        pl.BlockSpec((tq, NSA_G * HD), lambda b, h, qi, i_, p_: (b, h)),
    ]
    grid_spec = pltpu.PrefetchScalarGridSpec(
        num_scalar_prefetch=2,
        grid=(batch, NSA_KV, nq),
        in_specs=in_specs,
        out_specs=pl.BlockSpec((tq, NSA_G * HD), lambda b, h, qi, i_, p_: (b, h)),
        scratch_shapes=[pltpu.VMEM((NSA_G * tq, HD), F32)],
    )
    kern = functools.partial(_selg_kernel, nblk=nblk, tq=tq, q_pos0=q_pos0, tail_pos0=(ns - 1) * SEL_BLOCK)
    return pl.pallas_call(
        kern,
        grid_spec=grid_spec,
        out_shape=jax.ShapeDtypeStruct((batch * tq, NSA_HEADS * HD), F32),
        compiler_params=_cparams(("parallel", "parallel", "arbitrary")),
        name="sel_gather",
    )(idx, ptab, q, *([cache_rows] * nblk), *([bias_blk] * nblk), tail, tail, bias_blk, gate, prev)


def _rel_bucket(dist):
    n = jnp.maximum(dist, 0)
    max_exact = REL_BUCKETS // 2
    nf = jnp.maximum(n, 1).astype(F32)
    large = max_exact + (jnp.log(nf / max_exact) / math.log(REL_MAX_DIST / max_exact)
                         * (REL_BUCKETS - max_exact)).astype(jnp.int32)
    return jnp.where(n < max_exact, n, jnp.minimum(large, REL_BUCKETS - 1))


def _bias_by_distance(rel_bias):
    return rel_bias.astype(F32)[_rel_bucket(jnp.arange(REL_MAX_DIST))]


def _toeplitz(r, nrows, ncols):
    p = r.shape[-1]
    flat = jnp.tile(r, (1,) * (r.ndim - 1) + (nrows,))[..., :nrows * (p - 1)]
    return flat.reshape(r.shape[:-1] + (nrows, p - 1))[..., :ncols]


def _bias_tiles(rel_bias, tq, n_delta, transposed=False):
    fd = _bias_by_distance(rel_bias).T
    hi = TK * (n_delta + 1) - REL_MAX_DIST
    padded = jnp.concatenate([jnp.tile(fd[:, :1], (1, TK)), fd[:, :TK * (n_delta + 1)]]
                             + ([jnp.tile(fd[:, -1:], (1, hi))] if hi > 0 else []), axis=1)
    blocks = padded[:, 1:1 + TK * (n_delta + 1)].reshape(NSA_HEADS, n_delta + 1, TK)
    z = jnp.concatenate([blocks[:, :n_delta], blocks[:, 1:, :tq - 1]], axis=2)
    if transposed:
        r = jnp.roll(z, -(TK - 1), axis=2)
        return _toeplitz(r, TK, tq).reshape(NSA_KV, NSA_G, n_delta, TK, tq)
    r = jnp.roll(z[:, :, ::-1], -(tq - 1), axis=2)
    t = _toeplitz(r, tq, TK)
    return t.reshape(NSA_KV, NSA_G, n_delta, tq, TK).transpose(0, 2, 1, 3, 4)


def _bias_cmp(rel_bias, q_pos0, tqs, ncp):
    fd = _bias_by_distance(rel_bias)
    last = CMP_BLOCK - 1
    if tqs <= SEL_BLOCK:
        dist = (q_pos0 + jnp.arange(tqs))[:, None] - (jnp.arange(ncp) * CMP_STRIDE + last)[None, :]
        b = fd[jnp.clip(dist, 0, REL_MAX_DIST - 1)]
        return b.reshape(tqs, ncp, NSA_KV, NSA_G).transpose(2, 3, 0, 1)
    assert q_pos0 == 0 and tqs % CMP_STRIDE == 0
    ntau = tqs // CMP_STRIDE
    period = ntau + ncp
    kappa = period - jnp.arange(period)
    dist = CMP_STRIDE * kappa[None, :] + jnp.arange(CMP_STRIDE)[:, None] - last
    dist = jnp.where(kappa[None, :] < ntau, dist, 0)
    r = fd[jnp.clip(dist, 0, REL_MAX_DIST - 1)]
    t = _toeplitz(r.transpose(2, 0, 1), ntau, ncp)
    return t.transpose(0, 2, 1, 3).reshape(NSA_KV, NSA_G, tqs, ncp)


def _bias_blocks(rel_bias, q_pos0, nq, ns):
    fd = _bias_by_distance(rel_bias)
    n = ns * SEL_BLOCK
    rows = []
    for qq in range(nq):
        t = q_pos0 + qq
        far = max(min(t - (REL_MAX_DIST - 1), n), 0)
        mid_hi = min(t + 1, n)
        parts = [jnp.tile(fd[-1:], (far, 1))] if far else []
        if mid_hi > far:
            parts.append(fd[t - mid_hi + 1:t - far + 1][::-1])
        if n > mid_hi:
            parts.append(jnp.tile(fd[:1], (n - mid_hi, 1)))
        rows.append(jnp.concatenate(parts, axis=0))
    t = jnp.stack(rows)
    return t.reshape(nq, ns, SEL_BLOCK, NSA_KV, NSA_G).transpose(0, 3, 1, 4, 2)


def _forward_group(x, conv0, ssm0, past, P, *, batch, seq, t_valid, q_pos0, tq):
    M = batch * seq
    conv_out, ssm_out = [], []
    for l in range(2):
        proj = _nmm(x, P["mix_norm"][l], P["gdn_w_main"][l], tn=1024, tm=1024)
        bg = _nmm(x, P["mix_norm"][l], P["gdn_w_gate"][l], tn=128, mode="gdn_gate", aux=P["gdn_gate_aux"][l],
                  seq=seq, t_valid=t_valid)
        o, s_new = _gdn(proj, bg, P["gdn_conv_w"][l], conv0[l], ssm0[l], P["gdn_out_norm"][l],
                        batch=batch, seq=seq)
        conv_out.append(proj.reshape(batch, seq, -1)[:, t_valid - (GDN_CONV - 1):t_valid, :GDN_CONV_DIM])
        ssm_out.append(s_new)
        x = _mm_res(o, P["gdn_w_out"][l], x, tn=1024)
        x = _mlp(x, P["mlp_norm"][l], P["mlp_w1"][l], P["mlp_w2"][l])
    x, cmp_rows, sel_rows, win_state = _nsa_layers(x, past, P, batch=batch, seq=seq, t_valid=t_valid,
                                                   q_pos0=q_pos0, tq=tq)
    return x, jnp.stack(conv_out), jnp.stack(ssm_out), cmp_rows, sel_rows, win_state


def _nsa_layers(x, past, P, *, batch, seq, t_valid, q_pos0, tq):
    M = batch * seq
    kv = _nmm(x, P["kv_norm"], P["nsa_w_kv"], tn=512, tm=1024, mode="headnorm", aux=P["kv_aux"],
              norm_tiles=(2, 4))
    kv3 = kv.reshape(batch, seq, 6 * NSA_KV * HD)
    new_rows = kv3[:, :t_valid]
    cmp_rows = new_rows[..., 0:1024].reshape(batch, t_valid, 2, NSA_KV, HD)
    sel_rows = new_rows[..., 1024:2048].reshape(batch, t_valid, 2, NSA_KV, HD)
    win_new = new_rows[..., 2048:3072]

    ident = jnp.arange(M // PAGE, dtype=jnp.int32) if seq % PAGE == 0 else None
    if past is None:
        n_tot = t_valid
        npages = seq // PAGE
        kv_pages = kv.reshape(M // PAGE, PAGE, 6 * NSA_KV * HD)
        first = _cmp_stage1(kv_pages, ident, P["cmp_w1cat"], row_packed=False)
        nsb = npages // CMP_PPS
        nseg_tot = npages * (PAGE // CMP_STRIDE)
        f6 = first.reshape(batch, nsb, 2, NSA_KV, CMP_PPS * 8, 2, HD).transpose(2, 0, 3, 1, 4, 5, 6)
        f6 = f6.reshape(2, batch, NSA_KV, nseg_tot, 2, HD)
        win_seq = win_new
        w_pos0 = 0
    else:
        n_past = past["page_table"].shape[1] * PAGE
        n_tot = n_past + t_valid
        npages = n_past // PAGE
        ptab = past["page_table"].reshape(-1)
        first = _cmp_stage1(past["cmp_rows"], ptab, P["cmp_w1cat"], row_packed=True)
        nsb = npages // CMP_PPS
        f6 = first.reshape(batch, nsb, 2, NSA_KV, CMP_PPS * 8, 2, HD).transpose(2, 0, 3, 1, 4, 5, 6)
        f6 = f6.reshape(2, batch, NSA_KV, npages * 8, 2, HD)
        tail_cmp = jnp.pad(new_rows[..., 0:1024], ((0, 0), (0, PAGE - t_valid), (0, 0)))
        tail_cmp = jnp.pad(tail_cmp, ((0, (-batch) % CMP_PPS), (0, 0), (0, 0)))
        tfirst = _cmp_stage1(tail_cmp, jnp.arange(tail_cmp.shape[0], dtype=jnp.int32), P["cmp_w1cat"],
                             row_packed=False)
        t6 = tfirst.reshape(-1, 2, NSA_KV, CMP_PPS, 8, 2, HD).transpose(1, 0, 3, 2, 4, 5, 6)
        t6 = t6.reshape(2, -1, NSA_KV, 8, 2, HD)[:, :batch, :, :(-(-t_valid // CMP_STRIDE))]
        f6 = jnp.concatenate([f6, t6], axis=3)
        nseg_tot = f6.shape[3]
        win_seq = jnp.concatenate([past["win"], win_new], axis=1)
        w_pos0 = q_pos0 + t_valid - win_seq.shape[1]
    nc = -(-n_tot // CMP_STRIDE) - 1
    ns = -(-n_tot // SEL_BLOCK)
    ncp = -(-nc // 128) * 128
    nsp = -(-ns // 128) * 128
    a = f6[:, :, :, 0:nc, 0, :]
    b = f6[:, :, :, 1:nc + 1, 1, :]
    if b.shape[3] < nc:
        b = jnp.pad(b, ((0, 0), (0, 0), (0, 0), (0, nc - b.shape[3]), (0, 0)))
    a = jnp.pad(a, ((0, 0), (0, 0), (0, 0), (0, ncp - nc), (0, 0))).reshape(2, batch * NSA_KV * ncp, HD)
    b = jnp.pad(b, ((0, 0), (0, 0), (0, 0), (0, ncp - nc), (0, 0))).reshape(2, batch * NSA_KV * ncp, HD)
    R = batch * NSA_KV * ncp
    kcvc = _cmp_stage2(a, b, P["cmp_pe8"], P["cmp_w1flat"], P["cmp_b1"], P["cmp_w2"], P["cmp_b2"],
                       P["k_cmp_norm"], tr=min(R, 2048))
    kcvc = kcvc.reshape(2, batch, NSA_KV, ncp, HD)

    n_keep = min(WINDOW, win_seq.shape[1])
    win_state = win_seq[:, win_seq.shape[1] - n_keep:].reshape(batch, n_keep, 2, NSA_KV, HD)

    seq_q = seq if past is None else tq
    bias_c = _bias_cmp(P["rel_bias"], q_pos0, seq_q, ncp)
    n_delta = min(N_DELTA, (q_pos0 + seq_q) // TK + 1)
    if past is None:
        btiles_t = _bias_tiles(P["rel_bias"], tq, n_delta, transposed=True)
    else:
        btiles = _bias_tiles(P["rel_bias"], tq, n_delta)
    if past is None:
        sel_pages, sel_ptab, sel_npt = kv_pages, ident, seq // PAGE
        sel_kcol, sel_vcol = 8, 12
        sel_tails = None
        sel_pps = 4
        sel_steps = -(-sel_npt // sel_pps)
        sel_tile_of = lambda i, s, pp: s * sel_pps + pp
        sel_active = lambda i, s, pp: (s * sel_pps + pp) * TK <= i * tq + tq - 1
        win_pages, win_ptab, win_npt = kv_pages, ident, seq // PAGE
        win_kcol, win_vcol = 16, 20
        win_pps = WINDOW // TK + tq // TK
        win_tile_of = lambda i, s, pp: (i * tq) // TK - WINDOW // TK + pp
        win_active = lambda i, s, pp: (i * tq) // TK - WINDOW // TK + pp >= 0
        win_kbase = 0
    else:
        assert n_past % SEL_BLOCK == 0 and t_valid <= SEL_BLOCK and ns - 1 > SEL_TOPK
        tail_sel = jnp.pad(new_rows[..., 1024:2048], ((0, 0), (0, SEL_BLOCK - t_valid), (0, 0)))
        bias_blk = _bias_blocks(P["rel_bias"], q_pos0, t_valid, ns)
        nwt = -(-win_seq.shape[1] // TK)
        win_pages = jnp.pad(win_seq, ((0, 0), (0, nwt * TK - win_seq.shape[1]), (0, 0)))
        win_pages = win_pages.reshape(batch * nwt, TK, 2 * NSA_KV * HD)
        win_ptab, win_npt = jnp.arange(batch * nwt, dtype=jnp.int32), nwt
        win_kcol, win_vcol = 0, 4
        win_pps = nwt
        win_tile_of = lambda i, s, pp: pp
        win_active = lambda i, s, pp: True
        win_kbase = w_pos0

    for jj in range(2):
        l = 2 + jj
        q = _nmm(x, P["mix_norm"][l], P["nsa_w_q"][jj], tn=512, tm=1024, out_dtype=BF16, mode="headnorm",
                 aux=P["nsa_q_aux"][jj], scale=HD ** -0.5)
        gates = _nmm(x, P["mix_norm"][l], P["nsa_w_g"][jj], tn=128, mode="sigmoid")
        gate = gates[:, :NSA_HEADS * 3].reshape(batch, seq, NSA_KV, NSA_G, 3).transpose(4, 0, 2, 1, 3)
        if seq_q != seq:
            q = q.reshape(batch, seq, -1)[:, :seq_q].reshape(batch * seq_q, -1)
            gate = gate[:, :, :, :seq_q]
        if past is None:
            gate_t = gates[:, :NSA_HEADS * 3].reshape(batch, seq, NSA_KV, NSA_G, 3).transpose(4, 0, 2, 3, 1)
            o_c, sel = _attn_cmp(q, kcvc, bias_c, gate, batch=batch, seq=seq_q, tq=tq, q_pos0=q_pos0, nc=nc,
                                 ns=ns, nsp=nsp, sel_t=True)
            o_s = _flasht(q, sel_pages, sel_ptab, sel_kcol, sel_vcol, btiles_t, sel, gate_t, 1, o_c,
                          batch=batch, seq=seq_q, tq=tq, q_pos0=q_pos0, pps=sel_pps,
                          steps_of=lambda i: (i * tq + tq - 1) // (TK * sel_pps) + 1,
                          tile_of=sel_tile_of, kbase=0, w_pos0=0, npt=sel_npt, out_dtype=F32)
            o_w = _flasht(q, win_pages, win_ptab, win_kcol, win_vcol, btiles_t, None, gate_t, 2, o_s,
                          batch=batch, seq=seq_q, tq=tq, q_pos0=q_pos0, pps=win_pps, steps_of=lambda i: 1,
                          tile_of=win_tile_of, kbase=win_kbase, w_pos0=w_pos0, npt=win_npt, out_dtype=BF16)
        else:
            o_c, sel, idx = _attn_cmp(q, kcvc, bias_c, gate, batch=batch, seq=seq_q, tq=tq, q_pos0=q_pos0,
                                      nc=nc, ns=ns, nsp=nsp, n_idx=SEL_TOPK - 1)
            o_s = _sel_gather(q, past["sel_rows"], past["page_table"].reshape(-1),
                              idx[:, :, :t_valid, :SEL_TOPK - 1].reshape(-1), bias_blk, tail_sel, gate, o_c,
                              batch=batch, tq=tq, nq=t_valid, q_pos0=q_pos0, npt=npages, ns=ns)
            o_w = _flash(q, win_pages, win_ptab, win_kcol, win_vcol, btiles, None, None, gate, 2, o_s,
                         batch=batch, seq=seq_q, tq=tq, q_pos0=q_pos0, pps=win_pps,
                         nsteps=1, tile_of=win_tile_of, active=win_active, kbase=win_kbase, w_pos0=w_pos0,
                         npt=win_npt, tail_tile=0, out_dtype=F32)
        if seq_q != seq:
            o_w = jnp.pad(o_w.reshape(batch, seq_q, -1), ((0, 0), (0, seq - seq_q), (0, 0))).reshape(M, -1)
        x = _mm_res(o_w.astype(BF16), P["nsa_w_out"][jj], x, tm=1024, tn=1024)
        x = _mlp(x, P["mlp_norm"][l], P["mlp_w1"][l], P["mlp_w2"][l])
    return x, cmp_rows, sel_rows, win_state


def _prepare_params(mix_norm, mlp_norm, mlp_w1, mlp_w2, gdn_w_in, gdn_conv_w, gdn_a_log, gdn_dt_bias,
                    gdn_out_norm, gdn_w_out, kv_norm, nsa_w_kv, k_sel_norm, k_win_norm, k_cmp_norm, cmp_pe,
                    cmp_w1, cmp_b1, cmp_w2, cmp_b2, nsa_w_in, nsa_q_norm, nsa_w_out, rel_bias):
    n_lay = gdn_w_in.shape[0]
    main = GDN_CONV_DIM + GDN_VAL_DIM
    zpad = lambda n: jnp.zeros((1, n), F32)
    gate_aux = jnp.stack([
        jnp.concatenate([
            jnp.concatenate([zpad(GDN_V_HEADS), gdn_a_log[l][None].astype(F32), zpad(HD - 2 * GDN_V_HEADS)], 1),
            jnp.concatenate([zpad(GDN_V_HEADS), gdn_dt_bias[l][None].astype(F32), zpad(HD - 2 * GDN_V_HEADS)], 1),
        ], 0)[None] for l in range(n_lay)])
    tile4 = lambda w: jnp.tile(w.astype(F32), NSA_KV)[None, None]
    kv_aux = jnp.concatenate([jnp.ones((2, 1, 512), F32), tile4(k_sel_norm), jnp.ones((1, 1, 512), F32),
                              tile4(k_win_norm), jnp.ones((1, 1, 512), F32)], 0)
    nq = NSA_HEADS * HD
    w1r = cmp_w1.reshape(2, 2, CMP_STRIDE, HD, HD)
    P = dict(
        mix_norm=mix_norm, mlp_norm=mlp_norm,
        mlp_w1=mlp_w1.astype(BF16), mlp_w2=mlp_w2.astype(BF16),
        gdn_w_main=gdn_w_in[:, :, :main].astype(BF16),
        gdn_w_gate=jnp.pad(gdn_w_in[:, :, main:], ((0, 0), (0, 0), (0, HD - 2 * GDN_V_HEADS))).astype(BF16),
        gdn_gate_aux=gate_aux, gdn_conv_w=gdn_conv_w, gdn_out_norm=gdn_out_norm,
        gdn_w_out=gdn_w_out.astype(BF16),
        kv_norm=kv_norm, nsa_w_kv=nsa_w_kv.astype(BF16), kv_aux=kv_aux, k_cmp_norm=k_cmp_norm,
        cmp_w1cat=jnp.concatenate([w1r[:, 0], w1r[:, 1]], axis=-1).astype(BF16),
        cmp_w1flat=cmp_w1.reshape(2, CMP_BLOCK * HD, HD).astype(BF16),
        cmp_pe8=jnp.pad(cmp_pe.reshape(2, 1, CMP_BLOCK * HD), ((0, 0), (0, 7), (0, 0))),
        cmp_b1=cmp_b1, cmp_w2=cmp_w2.astype(BF16), cmp_b2=cmp_b2,
        nsa_w_q=nsa_w_in[:, :, :nq].astype(BF16),
        nsa_w_g=jnp.pad(nsa_w_in[:, :, nq:], ((0, 0), (0, 0), (0, HD - 3 * NSA_HEADS))).astype(BF16),
        nsa_q_aux=jnp.stack([jnp.tile(tile4(nsa_q_norm[jj]), (nq // 512, 1, 1)) for jj in range(2)]),
        nsa_w_out=nsa_w_out.astype(BF16), rel_bias=rel_bias,
    )
    return P


def kernel(x_prompt, x_sample, state_conv, state_ssm, cache_cmp, cache_sel, cache_win, page_table, mix_norm,
           mlp_norm, mlp_w1, mlp_w2, gdn_w_in, gdn_conv_w, gdn_a_log, gdn_dt_bias, gdn_out_norm, gdn_w_out,
           kv_norm, nsa_w_kv, k_sel_norm, k_win_norm, k_cmp_norm, cmp_pe, cmp_w1, cmp_b1, cmp_w2, cmp_b2,
           nsa_w_in, nsa_q_norm, nsa_w_out, rel_bias):
    bp, tp, _ = x_prompt.shape
    bs, ts, _ = x_sample.shape
    n_lay = gdn_w_in.shape[0]
    P = _prepare_params(mix_norm, mlp_norm, mlp_w1, mlp_w2, gdn_w_in, gdn_conv_w, gdn_a_log, gdn_dt_bias,
                        gdn_out_norm, gdn_w_out, kv_norm, nsa_w_kv, k_sel_norm, k_win_norm, k_cmp_norm, cmp_pe,
                        cmp_w1, cmp_b1, cmp_w2, cmp_b2, nsa_w_in, nsa_q_norm, nsa_w_out, rel_bias)

    conv0 =jnp.zeros((n_lay, bp, GDN_CONV - 1, GDN_CONV_DIM), F32)
    ssm0 = jnp.zeros((n_lay, bp, GDN_V_HEADS, HD, HD), F32)
    yp, conv_p, ssm_p, cmp_p, sel_p, win_p = _forward_group(
        x_prompt.reshape(bp * tp, D_MODEL), conv0, ssm0, None, P,
        batch=bp, seq=tp, t_valid=tp, q_pos0=0, tq=128)

    seq_s = GDN_CHUNK
    xs = jnp.pad(x_sample, ((0, 0), (0, seq_s - ts), (0, 0))).reshape(bs * seq_s, D_MODEL)
    n_pool = cache_cmp.shape[0]
    past = dict(cmp_rows=cache_cmp.reshape(n_pool * PAGE * 2 * NSA_KV, HD),
                sel_rows=cache_sel.reshape(n_pool * PAGE * 2 * NSA_KV, HD),
                page_table=page_table.astype(jnp.int32),
                win=cache_win.reshape(bs, cache_win.shape[1], 2 * NSA_KV * HD))
    n_past = page_table.shape[1] * PAGE
    ys, conv_s, ssm_s, cmp_s, sel_s, win_s = _forward_group(
        xs, state_conv, state_ssm, past, P, batch=bs, seq=seq_s, t_valid=ts, q_pos0=n_past, tq=8)
    y_sample = ys.reshape(bs, seq_s, D_MODEL)[:, :ts]
    return (yp.reshape(bp, tp, D_MODEL), y_sample, conv_p, ssm_p, cmp_p, sel_p, win_p,
            conv_s, ssm_s, cmp_s, sel_s, win_s)
```

```python
import functools
import math

import jax
import jax.numpy as jnp
from jax import lax
from jax.experimental import pallas as pl
from jax.experimental.pallas import tpu as pltpu

F32 = jnp.float32
BF16 = jnp.bfloat16

D_MODEL = 2048
D_FF = 4 * D_MODEL
NORM_EPS = 1e-6
L2_EPS = 1e-6
PAGE = 128

HD = 128
GDN_QK_HEADS = 16
GDN_V_HEADS = 32
GDN_KEY_DIM = GDN_QK_HEADS * HD
GDN_VAL_DIM = GDN_V_HEADS * HD
GDN_CONV = 4
GDN_CHUNK = 64
GDN_CONV_DIM = 2 * GDN_KEY_DIM + GDN_VAL_DIM

NSA_HEADS = 16
NSA_KV = 4
NSA_G = NSA_HEADS // NSA_KV
CMP_BLOCK = 32
CMP_STRIDE = 16
SEL_BLOCK = 64
SEL_TOPK = 16
WINDOW = 512
REL_BUCKETS = 32
REL_MAX_DIST = 4096
NEG = -1e30

TK = 128
FLASH_TQ = 256
N_DELTA = REL_MAX_DIST // TK + 2

VMEM_LIMIT = 56 * 1024 * 1024


def _cparams(sem):
    return pltpu.CompilerParams(dimension_semantics=sem, vmem_limit_bytes=VMEM_LIMIT)


def _sigmoid(x):
    return 1.0 / (1.0 + jnp.exp(-x))


def _softplus(x):
    return jnp.maximum(x, 0.0) + jnp.log(1.0 + jnp.exp(-jnp.abs(x)))


def _dot(a, b):
    return jnp.dot(a.astype(BF16), b.astype(BF16), preferred_element_type=F32)


def _dot_nt(a, b):
    return lax.dot_general(a.astype(BF16), b.astype(BF16), (((1,), (1,)), ((), ())),
                           preferred_element_type=F32)


def _dot_tn(a, b):
    return lax.dot_general(a.astype(BF16), b.astype(BF16), (((0,), (0,)), ((), ())),
                           preferred_element_type=F32)


def _headnorm(acc, gw):
    parts = []
    for g in range(acc.shape[1] // HD):
        a = acc[:, g * HD:(g + 1) * HD]
        parts.append(a * lax.rsqrt(jnp.mean(a * a, axis=-1, keepdims=True) + NORM_EPS))
    return jnp.concatenate(parts, axis=1) * gw


def _nmm_kernel(x_ref, nw_ref, w_ref, aux_ref, o_ref, h_ref, *, mode, norm_tiles, scale, seq, t_valid):
    i = pl.program_id(0)
    j = pl.program_id(1)

    @pl.when(j == 0)
    def _():
        x = x_ref[...]
        h = x * lax.rsqrt(jnp.mean(x * x, axis=-1, keepdims=True) + NORM_EPS) * nw_ref[...]
        h_ref[...] = h.astype(BF16)

    acc = jnp.dot(h_ref[...], w_ref[...], preferred_element_type=F32)
    if mode == "plain":
        o_ref[...] = acc.astype(o_ref.dtype)
    elif mode == "headnorm":
        if norm_tiles is None:
            o_ref[...] = (_headnorm(acc, aux_ref[0]) * scale).astype(o_ref.dtype)
        else:
            is_n = functools.reduce(jnp.logical_or, [j == t for t in norm_tiles])

            @pl.when(is_n)
            def _():
                o_ref[...] = (_headnorm(acc, aux_ref[0]) * scale).astype(o_ref.dtype)

            @pl.when(jnp.logical_not(is_n))
            def _():
                o_ref[...] = acc.astype(o_ref.dtype)
    elif mode == "sigmoid":
        o_ref[...] = _sigmoid(acc)
    elif mode == "gdn_gate":
        tm = acc.shape[0]
        aux = aux_ref[0]
        lane = lax.broadcasted_iota(jnp.int32, acc.shape, 1)
        row = lax.broadcasted_iota(jnp.int32, acc.shape, 0) + i * tm
        live = (row % seq) < t_valid
        beta = jnp.where(live, _sigmoid(acc), 0.0)
        g = jnp.where(live, -jnp.exp(aux[0:1, :]) * _softplus(acc + aux[1:2, :]), 0.0)
        g = jnp.where((lane >= GDN_V_HEADS) & (lane < 2 * GDN_V_HEADS), g, 0.0)
        r = lax.broadcasted_iota(jnp.int32, (tm, tm), 0)
        c = lax.broadcasted_iota(jnp.int32, (tm, tm), 1)
        tri = ((r // GDN_CHUNK) == (c // GDN_CHUNK)) & (c <= r)
        gcum = jnp.dot(jnp.where(tri, 1.0, 0.0), g, preferred_element_type=F32,
                       precision=lax.Precision.HIGHEST)
        o_ref[...] = jnp.where(lane < GDN_V_HEADS, beta, gcum)
    else:
        raise ValueError(mode)


def _nmm(x, nw, w, *, tn, out_dtype=F32, mode="plain", aux=None, norm_tiles=None, scale=1.0,
         seq=1, t_valid=1, tm=512, n_split=1):
    M, K = x.shape
    N = w.shape[1]
    tm = min(tm, M)
    assert M % tm == 0 and N % (tn * n_split) == 0
    if aux is None:
        aux = jnp.zeros((N // tn, 1, tn), F32)
    if n_split == 1:
        out_spec = pl.BlockSpec((tm, tn), lambda i, j: (i, j))
        out_shape = jax.ShapeDtypeStruct((M, N), out_dtype)
    else:
        per = N // n_split // tn
        out_spec = pl.BlockSpec((None, tm, tn), lambda i, j: (j // per, i, j % per))
        out_shape = jax.ShapeDtypeStruct((n_split, M, N // n_split), out_dtype)
    kern = functools.partial(_nmm_kernel, mode=mode, norm_tiles=norm_tiles, scale=scale, seq=seq,
                             t_valid=t_valid)
    return pl.pallas_call(
        kern,
        grid=(M // tm, N // tn),
        in_specs=[
            pl.BlockSpec((tm, K), lambda i, j: (i, 0)),
            pl.BlockSpec((1, K), lambda i, j: (0, 0)),
            pl.BlockSpec((K, tn), lambda i, j: (0, j)),
            pl.BlockSpec((1,) + aux.shape[1:], lambda i, j: (j, 0, 0)),
        ],
        out_specs=out_spec,
        out_shape=out_shape,
        scratch_shapes=[pltpu.VMEM((tm, K), BF16)],
        compiler_params=_cparams(("parallel", "arbitrary")),
        name="nmm_" + mode,
    )(x, nw.reshape(1, K), w, aux)


def _mmres_kernel(x_ref, w_ref, r_ref, o_ref):
    o_ref[...] = r_ref[...] + jnp.dot(x_ref[...], w_ref[...], preferred_element_type=F32)


def _mm_res(x, w, res, *, tm=512, tn=512):
    M, K = x.shape
    N = w.shape[1]
    tm = min(tm, M)
    return pl.pallas_call(
        _mmres_kernel,
        grid=(M // tm, N // tn),
        in_specs=[
            pl.BlockSpec((tm, K), lambda i, j: (i, 0)),
            pl.BlockSpec((K, tn), lambda i, j: (0, j)),
            pl.BlockSpec((tm, tn), lambda i, j: (i, j)),
        ],
        out_specs=pl.BlockSpec((tm, tn), lambda i, j: (i, j)),
        out_shape=jax.ShapeDtypeStruct((M, N), F32),
        compiler_params=_cparams(("parallel", "arbitrary")),
        name="mm_res",
    )(x, w, res)


def _mlp_kernel(x_ref, nw_ref, w1_ref, w2_ref, o_ref, h_ref, acc_ref):
    f = pl.program_id(1)

    @pl.when(f == 0)
    def _():
        x = x_ref[...]
        h = x * lax.rsqrt(jnp.mean(x * x, axis=-1, keepdims=True) + NORM_EPS) * nw_ref[...]
        h_ref[...] = h.astype(BF16)
        acc_ref[...] = x

    a = jnp.maximum(jnp.dot(h_ref[...], w1_ref[...], preferred_element_type=F32), 0.0)
    acc_ref[...] += jnp.dot((a * a).astype(BF16), w2_ref[...], preferred_element_type=F32)

    @pl.when(f == pl.num_programs(1) - 1)
    def _():
        o_ref[...] = acc_ref[...]


def _mlp(x, nw, w1, w2, *, tm=512, tf=1024):
    M, D = x.shape
    Fdim = w1.shape[1]
    tm = min(tm, M)
    return pl.pallas_call(
        _mlp_kernel,
        grid=(M // tm, Fdim // tf),
        in_specs=[
            pl.BlockSpec((tm, D), lambda i, f: (i, 0)),
            pl.BlockSpec((1, D), lambda i, f: (0, 0)),
            pl.BlockSpec((D, tf), lambda i, f: (0, f)),
            pl.BlockSpec((tf, D), lambda i, f: (f, 0)),
        ],
        out_specs=pl.BlockSpec((tm, D), lambda i, f: (i, 0)),
        out_shape=jax.ShapeDtypeStruct((M, D), F32),
        scratch_shapes=[pltpu.VMEM((tm, D), BF16), pltpu.VMEM((tm, D), F32)],
        compiler_params=_cparams(("parallel", "arbitrary")),
        name="mlp",
    )(x, nw.reshape(1, D), w1, w2)


def _unit_lower_inverse(mats, r, c):
    eye = jnp.where(r == c, 1.0, 0.0)
    in8 = (r // 8) == (c // 8)
    d0 = [jnp.where(in8, a, 0.0) for a in mats]
    d2 = [_dot(d, d) for d in d0]
    d4 = [_dot(d, d) for d in d2]
    x = [_dot(eye - a, eye + b) for a, b in zip(d0, d2)]
    x = [_dot(a, eye + b) for a, b in zip(x, d4)]
    s = 8
    while s < GDN_CHUNK:
        off = ((r // (2 * s)) == (c // (2 * s))) & ((r // s) != (c // s))
        bx = [_dot(jnp.where(off, a, 0.0), xi) for a, xi in zip(mats, x)]
        xbx = [_dot(xi, b) for xi, b in zip(x, bx)]
        x = [xi - b for xi, b in zip(x, xbx)]
        s *= 2
    return x


GDN_PAIRS = 8


def _gdn_chunk(qn, kn, vc, zb, beta, gc, st, onw):
    C = GDN_CHUNK
    R = 2 * C
    n = len(qn)
    rcol = lax.broadcasted_iota(jnp.int32, (R, 1), 0)
    top = rcol < C
    r = lax.broadcasted_iota(jnp.int32, (R, R), 0)
    c = lax.broadcasted_iota(jnp.int32, (R, R), 1)
    same = (r // C) == (c // C)
    low = same & (c <= r)
    slow = same & (c < r)
    srow = lax.broadcasted_iota(jnp.int32, (2 * HD, 1), 0)

    def blocked(a):
        return jnp.concatenate([jnp.where(top, a, 0.0), jnp.where(top, 0.0, a)], axis=1)

    beta2 = [jnp.concatenate(b, axis=0) for b in beta]
    gc2 = [jnp.concatenate(g, axis=0) for g in gc]
    gl2 = [jnp.where(top, g[0][C - 1:C, :], g[1][C - 1:C, :]) for g in gc]
    gls = [jnp.exp(jnp.where(srow < HD, g[0][C - 1:C, :], g[1][C - 1:C, :])) for g in gc]
    dec = []
    for g2 in gc2:
        colm = jnp.broadcast_to(g2, (R, R))
        dec.append(jnp.where(low, jnp.exp(jnp.where(low, colm - colm.T, 0.0)), 0.0))
    k2 = [jnp.concatenate([k, k], axis=0) for k in kn]
    q2 = [jnp.concatenate([q, q], axis=0) for q in qn]
    v2 = [jnp.concatenate([v[:, :HD], v[:, HD:]], axis=0) for v in vc]
    kk = [_dot_nt(k, k) for k in k2]
    qk = [_dot_nt(q, k) for q, k in zip(q2, k2)]
    amat = [jnp.where(slow, kk[i] * beta2[i] * dec[i], 0.0) for i in range(n)]
    attn = [qk[i] * dec[i] for i in range(n)]
    tinv = _unit_lower_inverse(amat, r, c)

    e2 = [jnp.exp(g) for g in gc2]
    rhs = [jnp.concatenate([v2[i] * beta2[i], k2[i] * beta2[i] * e2[i]], axis=1) for i in range(n)]
    sol = [_dot(tinv[i], rhs[i]) for i in range(n)]
    lhs = [jnp.concatenate([blocked(sol[i][:, HD:]), blocked(q2[i] * e2[i])], axis=0) for i in range(n)]
    ws = [_dot(lhs[i], st[i]) for i in range(n)]
    vnew = [sol[i][:, :HD] - ws[i][:R] for i in range(n)]
    av = [_dot(attn[i], vnew[i]) for i in range(n)]
    kd = [blocked(k2[i] * jnp.exp(gl2[i] - gc2[i])) for i in range(n)]
    kv = [_dot_tn(kd[i], vnew[i]) for i in range(n)]
    st_new = [st[i] * gls[i] + kv[i] for i in range(n)]

    outs = []
    for i in range(n):
        o2 = ws[i][R:] + av[i]
        z2 = jnp.concatenate([zb[i][:, :HD], zb[i][:, HD:]], axis=0)
        on = o2 * lax.rsqrt(jnp.mean(o2 * o2, axis=-1, keepdims=True) + NORM_EPS) * onw
        out2 = on * (z2 * _sigmoid(z2))
        outs.append(jnp.concatenate([out2[:C], out2[C:]], axis=1))
    return outs, st_new


def _gdn_kernel(q_ref, k_ref, v_ref, z_ref, bg_ref, wq_ref, wk_ref, wv_ref, cq_ref, ck_ref, cv_ref,
                s0_ref, onw_ref, o_ref, sout_ref, st_ref, bq_ref, bk_ref, bv_ref, *, single_chunk):
    C = GDN_CHUNK
    G = GDN_PAIRS
    jg = pl.program_id(1)
    ch = pl.program_id(2)

    def load_state():
        for p in range(G):
            st_ref[p, 0:HD, :] = s0_ref[0, 2 * p]
            st_ref[p, HD:2 * HD, :] = s0_ref[0, 2 * p + 1]
        bq_ref[5:8, :] = cq_ref[0]
        bk_ref[5:8, :] = ck_ref[0]
        bv_ref[5:8, :] = cv_ref[0]

    if single_chunk:
        load_state()
    else:
        pl.when(ch == 0)(load_state)

    def conv_silu(x_ref, buf_ref, w_ref):
        buf_ref[8:8 + C, :] = x_ref[...]
        w = w_ref[...]
        y = w[0:1, :] * buf_ref[5:5 + C, :]
        for t in range(1, GDN_CONV):
            y = y + w[t:t + 1, :] * buf_ref[5 + t:5 + t + C, :]
        buf_ref[5:8, :] = buf_ref[5 + C:8 + C, :]
        return y * _sigmoid(y)

    qc = conv_silu(q_ref, bq_ref, wq_ref)
    kc = conv_silu(k_ref, bk_ref, wk_ref)
    vc = conv_silu(v_ref, bv_ref, wv_ref)
    zb = z_ref[...]
    bg = bg_ref[...]
    lane = lax.broadcasted_iota(jnp.int32, bg.shape, 1)

    def col(idx):
        return jnp.sum(jnp.where(lane == idx, bg, 0.0), axis=-1, keepdims=True)

    qn, kn, beta, gc = [], [], [], []
    for p in range(G):
        qp = qc[:, p * HD:(p + 1) * HD]
        kp = kc[:, p * HD:(p + 1) * HD]
        qn.append(qp * lax.rsqrt(jnp.sum(qp * qp, axis=-1, keepdims=True) + L2_EPS) * (HD ** -0.5))
        kn.append(kp * lax.rsqrt(jnp.sum(kp * kp, axis=-1, keepdims=True) + L2_EPS))
        head = 2 * (jg * G + p)
        beta.append((col(head), col(head + 1)))
        gc.append((col(GDN_V_HEADS + head), col(GDN_V_HEADS + head + 1)))
    outs, new_states = _gdn_chunk(
        qn, kn, [vc[:, 2 * p * HD:2 * (p + 1) * HD] for p in range(G)],
        [zb[:, 2 * p * HD:2 * (p + 1) * HD] for p in range(G)], beta, gc,
        [st_ref[p] for p in range(G)], onw_ref[...])
    for p in range(G):
        st_ref[p] = new_states[p]
    o_ref[...] = jnp.concatenate(outs, axis=1).astype(o_ref.dtype)

    def write_state():
        for p in range(G):
            sout_ref[0, 2 * p] = new_states[p][:HD]
            sout_ref[0, 2 * p + 1] = new_states[p][HD:]

    if single_chunk:
        write_state()
    else:
        pl.when(ch == pl.num_programs(2) - 1)(write_state)


def _gdn(proj, bg, conv_w, conv0, ssm0, out_norm, *, batch, seq):
    C = GDN_CHUNK
    G = GDN_PAIRS
    nch = seq // C
    ng = GDN_QK_HEADS // G
    row = lambda b, j, c: b * nch + c
    return pl.pallas_call(
        functools.partial(_gdn_kernel, single_chunk=nch == 1),
        grid=(batch, ng, nch),
        in_specs=[
            pl.BlockSpec((C, G * HD), lambda b, j, c: (row(b, j, c), j)),
            pl.BlockSpec((C, G * HD), lambda b, j, c: (row(b, j, c), ng + j)),
            pl.BlockSpec((C, 2 * G * HD), lambda b, j, c: (row(b, j, c), ng + j)),
            pl.BlockSpec((C, 2 * G * HD), lambda b, j, c: (row(b, j, c), 2 * ng + j)),
            pl.BlockSpec((C, HD), lambda b, j, c: (row(b, j, c), 0)),
            pl.BlockSpec((GDN_CONV, G * HD), lambda b, j, c: (0, j)),
            pl.BlockSpec((GDN_CONV, G * HD), lambda b, j, c: (0, ng + j)),
            pl.BlockSpec((GDN_CONV, 2 * G * HD), lambda b, j, c: (0, ng + j)),
            pl.BlockSpec((1, GDN_CONV - 1, G * HD), lambda b, j, c: (b, 0, j)),
            pl.BlockSpec((1, GDN_CONV - 1, G * HD), lambda b, j, c: (b, 0, ng + j)),
            pl.BlockSpec((1, GDN_CONV - 1, 2 * G * HD), lambda b, j, c: (b, 0, ng + j)),
            pl.BlockSpec((1, 2 * G, HD, HD), lambda b, j, c: (b, j, 0, 0)),
            pl.BlockSpec((1, HD), lambda b, j, c: (0, 0)),
        ],
        out_specs=[
            pl.BlockSpec((C, 2 * G * HD), lambda b, j, c: (row(b, j, c), j)),
            pl.BlockSpec((1, 2 * G, HD, HD), lambda b, j, c: (b, j, 0, 0)),
        ],
        out_shape=[
            jax.ShapeDtypeStruct((batch * seq, GDN_VAL_DIM), BF16),
            jax.ShapeDtypeStruct((batch, GDN_V_HEADS, HD, HD), F32),
        ],
        scratch_shapes=[
            pltpu.VMEM((G, 2 * HD, HD), F32),
            pltpu.VMEM((8 + C, G * HD), F32),
            pltpu.VMEM((8 + C, G * HD), F32),
            pltpu.VMEM((8 + C, 2 * G * HD), F32),
        ],
        compiler_params=_cparams(("parallel", "parallel", "arbitrary")),
        name="gdn",
    )(proj, proj, proj, proj, bg, conv_w, conv_w, conv_w, conv0, conv0, conv0, ssm0,
      out_norm.reshape(1, HD))


CMP_PPS = 8


def _cmp1_kernel(pt_ref, *refs, row_packed):
    page_refs = refs[:CMP_PPS]
    w_ref = refs[CMP_PPS]
    o_ref = refs[CMP_PPS + 1]
    nseg = PAGE // CMP_STRIDE
    nch = 2 * NSA_KV
    pr = lax.broadcasted_iota(jnp.int32, (PAGE, PAGE), 0)
    pc = lax.broadcasted_iota(jnp.int32, (PAGE, PAGE), 1)
    perm = jnp.where(pc == (pr % nseg) * CMP_STRIDE + pr // nseg, 1.0, 0.0).astype(BF16)

    def slab(p, ch):
        if row_packed:
            return p[pl.ds(ch, PAGE, stride=nch), :]
        return p[0, :, ch * HD:(ch + 1) * HD]

    perm_slabs = [[jnp.dot(perm, slab(p, ch).astype(BF16), preferred_element_type=F32) for ch in range(nch)]
                  for p in page_refs]
    for cc in range(2):
        acc = jnp.zeros((NSA_KV * CMP_PPS * nseg, 2 * HD), F32)
        for rp in range(CMP_STRIDE // 2):
            lhs = jnp.concatenate(
                [jnp.concatenate([ps[cc * NSA_KV + h][(2 * rp) * nseg:(2 * rp + 1) * nseg],
                                  ps[cc * NSA_KV + h][(2 * rp + 1) * nseg:(2 * rp + 2) * nseg]], axis=1)
                 for h in range(NSA_KV) for ps in perm_slabs], axis=0)
            acc = acc + jnp.dot(lhs.astype(BF16), w_ref[cc, rp], preferred_element_type=F32)
        o_ref[0, cc] = acc


def _cmp_stage1(pages, ptab, w1cat, *, row_packed):
    n = ptab.shape[0]
    nst = n // CMP_PPS
    nseg = PAGE // CMP_STRIDE
    if row_packed:
        specs = [pl.BlockSpec((PAGE * 2 * NSA_KV, HD), lambda s, pt, p=p: (pt[s * CMP_PPS + p], 0))
                 for p in range(CMP_PPS)]
    else:
        specs = [pl.BlockSpec((1, PAGE, 2 * NSA_KV * HD), lambda s, pt, p=p: (pt[s * CMP_PPS + p], 0, 0))
                 for p in range(CMP_PPS)]
    grid_spec = pltpu.PrefetchScalarGridSpec(
        num_scalar_prefetch=1,
        grid=(nst,),
        in_specs=specs + [pl.BlockSpec((2, CMP_STRIDE // 2, 2 * HD, 2 * HD), lambda s, pt: (0, 0, 0, 0))],
        out_specs=pl.BlockSpec((1, 2, NSA_KV * CMP_PPS * nseg, 2 * HD), lambda s, pt: (s, 0, 0, 0)),
    )
    return pl.pallas_call(
        functools.partial(_cmp1_kernel, row_packed=row_packed),
        grid_spec=grid_spec,
        out_shape=jax.ShapeDtypeStruct((nst, 2, NSA_KV * CMP_PPS * nseg, 2 * HD), F32),
        compiler_params=_cparams(("arbitrary",)),
        name="cmp_stage1",
    )(ptab, *([pages] * CMP_PPS), w1cat.reshape(2, CMP_STRIDE // 2, 2 * HD, 2 * HD))


def _cmp2_kernel(a_ref, b_ref, pe_ref, w1_ref, b1_ref, w2_ref, b2_ref, nw_ref, o_ref):
    cc = pl.program_id(0)
    pe = pe_ref[0]
    pec = jnp.dot(pe.astype(BF16), w1_ref[0], preferred_element_type=F32)[0:1, :]
    hid = a_ref[0] + b_ref[0] + pec + b1_ref[0]
    hid = hid * _sigmoid(hid)
    out = jnp.dot(hid.astype(BF16), w2_ref[0], preferred_element_type=F32) + b2_ref[0]

    @pl.when(cc == 0)
    def _():
        o_ref[0] = out * lax.rsqrt(jnp.mean(out * out, axis=-1, keepdims=True) + NORM_EPS) * nw_ref[...]

    @pl.when(cc != 0)
    def _():
        o_ref[0] = out


def _cmp_stage2(a, b, pe8, w1flat, b1, w2, b2, nw, *, tr):
    R = a.shape[1]
    return pl.pallas_call(
        _cmp2_kernel,
        grid=(2, R // tr),
        in_specs=[
            pl.BlockSpec((1, tr, HD), lambda c, i: (c, i, 0)),
            pl.BlockSpec((1, tr, HD), lambda c, i: (c, i, 0)),
            pl.BlockSpec((1, 8, CMP_BLOCK * HD), lambda c, i: (c, 0, 0)),
            pl.BlockSpec((1, CMP_BLOCK * HD, HD), lambda c, i: (c, 0, 0)),
            pl.BlockSpec((1, 1, HD), lambda c, i: (c, 0, 0)),
            pl.BlockSpec((1, HD, HD), lambda c, i: (c, 0, 0)),
            pl.BlockSpec((1, 1, HD), lambda c, i: (c, 0, 0)),
            pl.BlockSpec((1, HD), lambda c, i: (0, 0)),
        ],
        out_specs=pl.BlockSpec((1, tr, HD), lambda c, i: (c, i, 0)),
        out_shape=jax.ShapeDtypeStruct((2, R, HD), F32),
        compiler_params=_cparams(("arbitrary", "arbitrary")),
        name="cmp_stage2",
    )(a, b, pe8, w1flat, b1.reshape(2, 1, HD), w2, b2.reshape(2, 1, HD), nw.reshape(1, HD))


def _stack_heads(qb):
    return jnp.concatenate([qb[:, g * HD:(g + 1) * HD] for g in range(NSA_G)], axis=0)


def _unstack_heads(o, tq):
    return jnp.concatenate([o[g * tq:(g + 1) * tq] for g in range(NSA_G)], axis=1)


def _gate_rows(gt):
    return jnp.concatenate([gt[:, g:g + 1] for g in range(NSA_G)], axis=0)


def _attn_cmp_kernel(q_ref, kc_ref, vc_ref, bias_ref, gate_ref, o_ref, sel_ref, *idx_ref, tq, q_pos0, nc, ns, nsp,
                     n_idx, sel_t):
    i = pl.program_id(2)
    ncp = kc_ref.shape[2]
    qs = _stack_heads(q_ref[...])
    logits = _dot_nt(qs, kc_ref[0, 0])
    logits = logits + jnp.concatenate([bias_ref[0, g] for g in range(NSA_G)], axis=0)
    rows = NSA_G * tq
    t4 = q_pos0 + i * tq + lax.broadcasted_iota(jnp.int32, (rows, ncp), 0) % tq
    cidx = lax.broadcasted_iota(jnp.int32, (rows, ncp), 1)
    mask = (cidx * CMP_STRIDE + (CMP_BLOCK - 1) <= t4) & (cidx < nc)
    lg = jnp.where(mask, logits, NEG)
    mx = jnp.max(lg, axis=-1, keepdims=True)
    ex = jnp.exp(lg - mx)
    p = ex / jnp.sum(ex, axis=-1, keepdims=True) * jnp.where(mask, 1.0, 0.0)
    oc = _dot(p, vc_ref[0, 0])
    o_ref[...] = _unstack_heads(oc * _gate_rows(gate_ref[0, 0, 0]), tq)

    psum = p[0:tq]
    for g in range(1, NSA_G):
        psum = psum + p[g * tq:(g + 1) * tq]
    top_k = min(SEL_TOPK, ns)
    if sel_t:
        nsr = sel_ref.shape[2]
        sr = lax.broadcasted_iota(jnp.int32, (nsr, ncp), 0)
        cc = lax.broadcasted_iota(jnp.int32, (nsr, ncp), 1)
        hit = (cc * CMP_STRIDE < sr * SEL_BLOCK + SEL_BLOCK) & (cc * CMP_STRIDE + CMP_BLOCK > sr * SEL_BLOCK)
        c2s_t = jnp.where(hit & (cc < nc) & (sr < ns), 1.0, 0.0)
        imp_t = jnp.dot(c2s_t, psum.T, preferred_element_type=F32, precision=lax.Precision.HIGHEST)
        tt = q_pos0 + i * tq + lax.broadcasted_iota(jnp.int32, (nsr, tq), 1)
        st = lax.broadcasted_iota(jnp.int32, (nsr, tq), 0)
        cur_t = tt // SEL_BLOCK
        forced_t = (st == 0) | (st == cur_t) | (st == cur_t - 1)
        score_t = jnp.where(forced_t, NSA_G + 1.0, jnp.where(st * SEL_BLOCK <= tt, imp_t, -1.0))
        score_t = jnp.where(st < ns, score_t, -2.0)
        rank_t = jnp.zeros((nsr, tq), F32)
        for sp in range(ns):
            other = score_t[sp:sp + 1, :]
            rank_t = rank_t + jnp.where(other > score_t, 1.0, jnp.where((other == score_t) & (sp < st), 1.0, 0.0))
        sel_ref[0, 0] = jnp.where((rank_t < top_k) & (st < ns), 1.0, 0.0)
        return
    cr = lax.broadcasted_iota(jnp.int32, (ncp, nsp), 0)
    sc = lax.broadcasted_iota(jnp.int32, (ncp, nsp), 1)
    c2s = (cr * CMP_STRIDE < sc * SEL_BLOCK + SEL_BLOCK) & (cr * CMP_STRIDE + CMP_BLOCK > sc * SEL_BLOCK)
    c2s = jnp.where(c2s & (cr < nc) & (sc < ns), 1.0, 0.0)
    imp = jnp.dot(psum, c2s, preferred_element_type=F32, precision=lax.Precision.HIGHEST)

    t = q_pos0 + i * tq + lax.broadcasted_iota(jnp.int32, (tq, nsp), 0)
    s = lax.broadcasted_iota(jnp.int32, (tq, nsp), 1)
    cur = t // SEL_BLOCK
    forced = (s == 0) | (s == cur) | (s == cur - 1)
    valid = s * SEL_BLOCK <= t
    score = jnp.where(forced, NSA_G + 1.0, jnp.where(valid, imp, -1.0))
    score = jnp.where(s < ns, score, -2.0)
    rank = jnp.zeros((tq, nsp), F32)
    for sp in range(ns):
        other = score[:, sp:sp + 1]
        ahead = (other > score) | ((other == score) & (sp < s))
        rank = rank + jnp.where(ahead, 1.0, 0.0)
    picked = (rank < top_k) & (s < ns)
    sel_ref[0, 0] = jnp.where(picked, 1.0, 0.0)
    if n_idx:
        listed = jnp.where(picked & (s < ns - 1), 1.0, 0.0)
        before = _dot(listed, jnp.where(lax.broadcasted_iota(jnp.int32, (nsp, nsp), 0)
                                        < lax.broadcasted_iota(jnp.int32, (nsp, nsp), 1), 1.0, 0.0))
        lane = lax.broadcasted_iota(jnp.int32, (tq, HD), 1)
        sf = s.astype(F32)
        out = jnp.zeros((tq, HD), F32)
        for kk in range(n_idx):
            hit = (listed > 0.5) & (before == float(kk))
            out = out + jnp.where(lane == kk, jnp.sum(jnp.where(hit, sf, 0.0), axis=-1, keepdims=True), 0.0)
        idx_ref[0][0, 0] = out.astype(jnp.int32)


def _attn_cmp(q, kcvc, bias_c, gate, *, batch, seq, tq, q_pos0, nc, ns, nsp, n_idx=0, sel_t=False):
    nqt = seq // tq
    ncp = kcvc.shape[3]
    kern = functools.partial(_attn_cmp_kernel, tq=tq, q_pos0=q_pos0, nc=nc, ns=ns, nsp=nsp, n_idx=n_idx,
                             sel_t=sel_t)
    rows_per_b = seq // tq
    nsr = -(-ns // 8) * 8
    if sel_t:
        sel_spec = pl.BlockSpec((1, 1, nsr, tq), lambda b, h, i: (b, h, 0, i))
        sel_shape = jax.ShapeDtypeStruct((batch, NSA_KV, nsr, seq), F32)
    else:
        sel_spec = pl.BlockSpec((1, 1, tq, nsp), lambda b, h, i: (b, h, i, 0))
        sel_shape = jax.ShapeDtypeStruct((batch, NSA_KV, seq, nsp), F32)
    extra_specs = [pl.BlockSpec((1, 1, tq, HD), lambda b, h, i: (b, h, i, 0))] if n_idx else []
    extra_shapes = [jax.ShapeDtypeStruct((batch, NSA_KV, seq, HD), jnp.int32)] if n_idx else []
    return pl.pallas_call(
        kern,
        grid=(batch, NSA_KV, nqt),
        in_specs=[
            pl.BlockSpec((tq, NSA_G * HD), lambda b, h, i: (b * rows_per_b + i, h)),
            pl.BlockSpec((None, 1, 1, ncp, HD), lambda b, h, i: (0, b, h, 0, 0)),
            pl.BlockSpec((None, 1, 1, ncp, HD), lambda b, h, i: (1, b, h, 0, 0)),
            pl.BlockSpec((1, NSA_G, tq, ncp), lambda b, h, i: (h, 0, i, 0)),
            pl.BlockSpec((1, 1, 1, tq, NSA_G), lambda b, h, i: (0, b, h, i, 0)),
        ],
        out_specs=[
            pl.BlockSpec((tq, NSA_G * HD), lambda b, h, i: (b * rows_per_b + i, h)),
            sel_spec,
        ] + extra_specs,
        out_shape=[jax.ShapeDtypeStruct((batch * seq, NSA_HEADS * HD), F32), sel_shape] + extra_shapes,
        compiler_params=_cparams(("parallel", "parallel", "arbitrary")),
        name="attn_cmp",
    )(q, kcvc, kcvc, bias_c, gate)


def _flash_kernel(pt_ref, *refs, cfg):
    pps, tq, has_tail, use_sel = cfg["pps"], cfg["tq"], cfg["has_tail"], cfg["use_sel"]
    it = iter(refs)
    q_ref = next(it)
    k_refs = [next(it) for _ in range(pps)]
    v_refs = [next(it) for _ in range(pps)]
    b_refs = [next(it) for _ in range(pps)]
    if has_tail:
        kt_ref, vt_ref, bt_ref = next(it), next(it), next(it)
    sel_ref = next(it) if use_sel else None
    gate_ref = next(it)
    prev_ref = next(it)
    o_ref = next(it)
    m_ref, l_ref, acc_ref = next(it), next(it), next(it)

    i = pl.program_id(2)
    st = pl.program_id(3)
    rows = NSA_G * tq

    @pl.when(st == 0)
    def _():
        m_ref[...] = jnp.full((rows, 1), NEG, F32)
        l_ref[...] = jnp.zeros((rows, 1), F32)
        acc_ref[...] = jnp.zeros((rows, HD), F32)

    t0 = cfg["q_pos0"] + i * tq

    def tile(k, v, bias4, kt, p0):
        qs = _stack_heads(q_ref[...])
        s = _dot_nt(qs, k) + jnp.concatenate([bias4[g] for g in range(NSA_G)], axis=0)
        tt = t0 + lax.broadcasted_iota(jnp.int32, (tq, TK), 0)
        pos = p0 + lax.broadcasted_iota(jnp.int32, (tq, TK), 1)
        dist = tt - pos
        ok = dist >= 0
        if use_sel:
            nsp = sel_ref.shape[3]
            sr = lax.broadcasted_iota(jnp.int32, (nsp, TK), 0)
            sc = lax.broadcasted_iota(jnp.int32, (nsp, TK), 1)
            expand = jnp.where(sr == kt * (TK // SEL_BLOCK) + sc // SEL_BLOCK, 1.0, 0.0)
            picked = _dot(sel_ref[0, 0], expand)
            ok = ok & (picked > 0.5)
        else:
            ok = ok & (dist < WINDOW) & (pos >= cfg["w_pos0"])
        okf = jnp.where(ok, 1.0, 0.0)
        ok4 = jnp.concatenate([okf] * NSA_G, axis=0)
        s = jnp.where(ok4 > 0.5, s, NEG)
        m_old = m_ref[...]
        m_new = jnp.maximum(m_old, jnp.max(s, axis=-1, keepdims=True))
        alpha = jnp.exp(m_old - m_new)
        p = jnp.exp(s - m_new) * ok4
        l_ref[...] = alpha * l_ref[...] + jnp.sum(p, axis=-1, keepdims=True)
        acc_ref[...] = alpha * acc_ref[...] + _dot(p, v)
        m_ref[...] = m_new

    for pp in range(pps):
        kt = cfg["tile_of"](i, st, pp)
        active = cfg["active"](i, st, pp)
        p0 = cfg["kbase"] + kt * TK
        if active is True:
            tile(k_refs[pp][0], v_refs[pp][0], b_refs[pp][0, 0], kt, p0)
        else:
            @pl.when(active)
            def _(pp=pp, kt=kt, p0=p0):
                tile(k_refs[pp][0], v_refs[pp][0], b_refs[pp][0, 0], kt, p0)

    if has_tail:
        @pl.when(st == pl.num_programs(3) - 1)
        def _():
            tile(kt_ref[0], vt_ref[0], bt_ref[0, 0], cfg["tail_tile"], cfg["kbase"] + cfg["tail_tile"] * TK)

    @pl.when(st == pl.num_programs(3) - 1)
    def _():
        o = acc_ref[...] / l_ref[...] * _gate_rows(gate_ref[0, 0, 0])
        o_ref[...] = (prev_ref[...] + _unstack_heads(o, tq)).astype(o_ref.dtype)


def _flash(q, pages, ptab, kcol, vcol, bias_tiles, tails, sel, gate, branch, prev, *, batch, seq, tq, q_pos0,
           pps, nsteps, tile_of, active, kbase, w_pos0, npt, tail_tile, out_dtype):
    nqt = seq // tq
    has_tail = tails is not None
    use_sel = sel is not None
    n_delta = bias_tiles.shape[1]
    cfg = dict(pps=pps, tq=tq, has_tail=has_tail, use_sel=use_sel, q_pos0=q_pos0, tile_of=tile_of,
               active=active, kbase=kbase, w_pos0=w_pos0, tail_tile=tail_tile)

    def page_idx(b, i, s, pp, pt):
        kt = jnp.clip(tile_of(i, s, pp), 0, npt - 1)
        return pt[b * npt + kt]

    def didx(i, s, pp):
        kt = tile_of(i, s, pp)
        return jnp.clip((q_pos0 + i * tq - kbase - kt * TK) // TK, 0, n_delta - 1)

    in_specs = [pl.BlockSpec((tq, NSA_G * HD), lambda b, h, i, s, pt: (b * nqt + i, h))]
    args = [q]
    for col in (kcol, vcol):
        for pp in range(pps):
            in_specs.append(pl.BlockSpec(
                (1, TK, HD), lambda b, h, i, s, pt, pp=pp, col=col: (page_idx(b, i, s, pp, pt), 0, col + h)))
            args.append(pages)
    for pp in range(pps):
        in_specs.append(pl.BlockSpec(
            (1, 1, NSA_G, tq, TK), lambda b, h, i, s, pt, pp=pp: (h, didx(i, s, pp), 0, 0, 0)))
        args.append(bias_tiles)
    if has_tail:
        tail_pages, tkcol, tvcol = tails
        tdelta = min(max((q_pos0 - kbase - tail_tile * TK) // TK, 0), n_delta - 1)
        in_specs.append(pl.BlockSpec((1, TK, HD), lambda b, h, i, s, pt: (b, 0, tkcol + h)))
        in_specs.append(pl.BlockSpec((1, TK, HD), lambda b, h, i, s, pt: (b, 0, tvcol + h)))
        in_specs.append(pl.BlockSpec((1, 1, NSA_G, tq, TK), lambda b, h, i, s, pt: (h, tdelta, 0, 0, 0)))
        args += [tail_pages, tail_pages, bias_tiles]
    if use_sel:
        nsp = sel.shape[3]
        in_specs.append(pl.BlockSpec((1, 1, tq, nsp), lambda b, h, i, s, pt: (b, h, i, 0)))
        args.append(sel)
    in_specs.append(pl.BlockSpec((1, 1, 1, tq, NSA_G), lambda b, h, i, s, pt: (branch, b, h, i, 0)))
    args.append(gate)
    in_specs.append(pl.BlockSpec((tq, NSA_G * HD), lambda b, h, i, s, pt: (b * nqt + i, h)))
    args.append(prev)

    rows = NSA_G * tq
    grid_spec = pltpu.PrefetchScalarGridSpec(
        num_scalar_prefetch=1,
        grid=(batch, NSA_KV, nqt, nsteps),
        in_specs=in_specs,
        out_specs=pl.BlockSpec((tq, NSA_G * HD), lambda b, h, i, s, pt: (b * nqt + i, h)),
        scratch_shapes=[pltpu.VMEM((rows, 1), F32), pltpu.VMEM((rows, 1), F32), pltpu.VMEM((rows, HD), F32)],
    )
    return pl.pallas_call(
        functools.partial(_flash_kernel, cfg=cfg),
        grid_spec=grid_spec,
        out_shape=jax.ShapeDtypeStruct((batch * seq, NSA_HEADS * HD), out_dtype),
        compiler_params=_cparams(("parallel", "parallel", "arbitrary", "arbitrary")),
        name="flash_sel" if use_sel else "flash_win",
    )(ptab, *args)


def _flasht_kernel(pt_ref, qi_ref, si_ref, lf_ref, kt_ref, ktc_ref, dd_ref, *refs, cfg):
    pps, tq, use_sel = cfg["pps"], cfg["tq"], cfg["use_sel"]
    it = iter(refs)
    q_ref = next(it)
    k_refs = [next(it) for _ in range(pps)]
    v_refs = [next(it) for _ in range(pps)]
    b_refs = [next(it) for _ in range(pps)]
    sel_ref = next(it) if use_sel else None
    gate_ref = next(it)
    prev_ref = next(it)
    o_ref = next(it)
    qt_ref, m_ref, l_ref, acc_ref = next(it), next(it), next(it), next(it)

    n = pl.program_id(2)
    i = qi_ref[n]
    st = si_ref[n]
    cols = NSA_G * tq

    @pl.when(st == 0)
    def _():
        qb = q_ref[...].astype(F32)
        qt_ref[...] = jnp.concatenate([qb[:, g * HD:(g + 1) * HD].T for g in range(NSA_G)], axis=1).astype(BF16)
        m_ref[...] = jnp.full((1, cols), NEG, F32)
        l_ref[...] = jnp.zeros((1, cols), F32)
        acc_ref[...] = jnp.zeros((HD, cols), F32)

    t0 = cfg["q_pos0"] + i * tq
    qt = qt_ref[...]
    tt = t0 + lax.broadcasted_iota(jnp.int32, (TK, tq), 1)
    krow = lax.broadcasted_iota(jnp.int32, (TK, tq), 0)
    scores, oks = [], []
    for pp in range(pps):
        kt = kt_ref[n * pps + pp]
        pos = cfg["kbase"] + kt * TK + krow
        dist = tt - pos
        ok = dist >= 0
        if use_sel:
            nsr = sel_ref.shape[2]
            kr = lax.broadcasted_iota(jnp.int32, (TK, nsr), 0)
            sc = lax.broadcasted_iota(jnp.int32, (TK, nsr), 1)
            expand = jnp.where(sc == kt * (TK // SEL_BLOCK) + kr // SEL_BLOCK, 1.0, 0.0)
            ok = ok & (_dot(expand, sel_ref[0, 0]) > 0.5)
        else:
            ok = ok & (dist < WINDOW) & (pos >= cfg["w_pos0"])
        ok4 = jnp.concatenate([ok] * NSA_G, axis=1)
        bias = jnp.concatenate([b_refs[pp][0, g, 0] for g in range(NSA_G)], axis=1)
        s = jnp.dot(k_refs[pp][0].astype(BF16), qt, preferred_element_type=F32) + bias
        scores.append(jnp.where(ok4, s, NEG))
        oks.append(ok4)
    m_old = m_ref[...]
    m_new = functools.reduce(jnp.maximum, [jnp.max(s, axis=0, keepdims=True) for s in scores] + [m_old])
    alpha = jnp.exp(m_old - m_new)
    ps = [jnp.where(ok4, jnp.exp(s - m_new), 0.0) for s, ok4 in zip(scores, oks)]
    l_new = alpha * l_ref[...]
    acc = alpha * acc_ref[...]
    for pp in range(pps):
        l_new = l_new + jnp.sum(ps[pp], axis=0, keepdims=True)
        acc = acc + _dot_tn(v_refs[pp][0], ps[pp])
    l_ref[...] = l_new
    acc_ref[...] = acc
    m_ref[...] = m_new

    @pl.when(lf_ref[n] == 1)
    def _():
        gt = gate_ref[0, 0, 0]
        grow = jnp.concatenate([gt[g:g + 1, :] for g in range(NSA_G)], axis=1)
        ot = acc / l_new * grow
        o = jnp.concatenate([ot[:, g * tq:(g + 1) * tq].T for g in range(NSA_G)], axis=1)
        o_ref[...] = (prev_ref[...] + o).astype(o_ref.dtype)


def _flasht(q, pages, ptab, kcol, vcol, bias_t, sel, gate_t, branch, prev, *, batch, seq, tq, q_pos0, pps,
            steps_of, tile_of, kbase, w_pos0, npt, out_dtype):
    nqt = seq // tq
    use_sel = sel is not None
    n_delta = bias_t.shape[2]
    cols = NSA_G * tq
    pairs = [(i, s) for i in range(nqt) for s in range(steps_of(i))]
    qi = jnp.asarray([p[0] for p in pairs], jnp.int32)
    si = jnp.asarray([p[1] for p in pairs], jnp.int32)
    lf = jnp.asarray([int(s == steps_of(i) - 1) for (i, s) in pairs], jnp.int32)
    cfg = dict(pps=pps, tq=tq, use_sel=use_sel, q_pos0=q_pos0, kbase=kbase, w_pos0=w_pos0)
    kt_raw = [tile_of(i, s, pp) for (i, s) in pairs for pp in range(pps)]
    kt = jnp.asarray(kt_raw, jnp.int32)
    ktc = jnp.asarray([min(max(k, 0), npt - 1) for k in kt_raw], jnp.int32)
    d_neg = tq // TK - 1
    dd = jnp.asarray([min(max((q_pos0 + i * tq - kbase - tile_of(i, s, pp) * TK) // TK + d_neg, 0), n_delta - 1)
                      for (i, s) in pairs for pp in range(pps)], jnp.int32)

    qmap = lambda b, h, n, pt, qi_, *_: (b * nqt + qi_[n], h)
    in_specs = [pl.BlockSpec((tq, NSA_G * HD), qmap)]
    args = [q]
    for col in (kcol, vcol):
        for pp in range(pps):
            in_specs.append(pl.BlockSpec(
                (1, TK, HD),
                lambda b, h, n, pt, qi_, si_, lf_, kt_, ktc_, dd_, pp=pp, col=col:
                (pt[b * npt + ktc_[n * pps + pp]], 0, col + h)))
            args.append(pages)
    for pp in range(pps):
        in_specs.append(pl.BlockSpec(
            (1, NSA_G, 1, TK, tq),
            lambda b, h, n, pt, qi_, si_, lf_, kt_, ktc_, dd_, pp=pp: (h, 0, dd_[n * pps + pp], 0, 0)))
        args.append(bias_t)
    if use_sel:
        nsr = sel.shape[2]
        in_specs.append(pl.BlockSpec((1, 1, nsr, tq), lambda b, h, n, pt, qi_, *_: (b, h, 0, qi_[n])))
        args.append(sel)
    in_specs.append(pl.BlockSpec((1, 1, 1, NSA_G, tq), lambda b, h, n, pt, qi_, *_: (branch, b, h, 0, qi_[n])))
    args.append(gate_t)
    in_specs.append(pl.BlockSpec((tq, NSA_G * HD), qmap))
    args.append(prev)

    grid_spec = pltpu.PrefetchScalarGridSpec(
        num_scalar_prefetch=7,
        grid=(batch, NSA_KV, len(pairs)),
        in_specs=in_specs,
        out_specs=pl.BlockSpec((tq, NSA_G * HD), qmap),
        scratch_shapes=[pltpu.VMEM((HD, cols), BF16), pltpu.VMEM((1, cols), F32), pltpu.VMEM((1, cols), F32),
                        pltpu.VMEM((HD, cols), F32)],
    )
    return pl.pallas_call(
        functools.partial(_flasht_kernel, cfg=cfg),
        grid_spec=grid_spec,
        out_shape=jax.ShapeDtypeStruct((batch * seq, NSA_HEADS * HD), out_dtype),
        compiler_params=_cparams(("parallel", "parallel", "arbitrary")),
        name="flasht_sel" if use_sel else "flasht_win",
    )(ptab, qi, si, lf, kt, ktc, dd, *args)


def _selg_kernel(idx_ref, pt_ref, q_ref, *refs, nblk, tq, q_pos0, tail_pos0):
    kv_refs = refs[:nblk]
    tk_ref, tv_ref, bt_ref, gate_ref, prev_ref, o_ref, osc_ref = refs[nblk:]
    ns = bt_ref.shape[2]
    b = pl.program_id(0)
    h = pl.program_id(1)
    qi = pl.program_id(2)
    nq = pl.num_programs(2)
    rows = NSA_G * tq
    nch = 2 * NSA_KV

    @pl.when(qi == 0)
    def _():
        osc_ref[...] = jnp.zeros((rows, HD), F32)

    qs = _stack_heads(q_ref[...])
    t = q_pos0 + qi
    base = ((b * NSA_KV + h) * nq + qi) * nblk
    jj = lax.broadcasted_iota(jnp.int32, (rows, SEL_BLOCK), 1)

    def bias_rows(blk):
        bb = bt_ref[0, 0, blk]
        return jnp.concatenate([jnp.broadcast_to(bb[g:g + 1, :], (tq, SEL_BLOCK)) for g in range(NSA_G)], axis=0)

    scores, vals = [], []
    for n in range(nblk):
        k = kv_refs[n][pl.ds(h, SEL_BLOCK, stride=nch), :]
        v = kv_refs[n][pl.ds(NSA_KV + h, SEL_BLOCK, stride=nch), :]
        blk = idx_ref[base + n]
        s = _dot_nt(qs, k) + bias_rows(blk)
        pos = blk * SEL_BLOCK + jj
        scores.append(jnp.where(pos <= t, s, NEG))
        vals.append(v)
    s = _dot_nt(qs, tk_ref[0]) + bias_rows(ns - 1)
    scores.append(jnp.where(tail_pos0 + jj <= t, s, NEG))
    vals.append(tv_ref[0])

    m = functools.reduce(jnp.maximum, [jnp.max(s, axis=-1, keepdims=True) for s in scores])
    l = jnp.zeros((rows, 1), F32)
    acc = jnp.zeros((rows, HD), F32)
    for s, v in zip(scores, vals):
        p = jnp.exp(s - m)
        l = l + jnp.sum(p, axis=-1, keepdims=True)
        acc = acc + _dot(p, v)
    rowq = lax.broadcasted_iota(jnp.int32, (rows, 1), 0) % tq
    osc = jnp.where(rowq == qi, acc / l, osc_ref[...])
    osc_ref[...] = osc

    @pl.when(qi == nq - 1)
    def _():
        o_ref[...] = prev_ref[...] + _unstack_heads(osc * _gate_rows(gate_ref[0, 0, 0]), tq)


def _sel_gather(q, cache_rows, ptab, idx, bias_blk, tail, gate, prev, *, batch, tq, nq, q_pos0, npt, ns):
    nblk = idx.shape[0] // (batch * NSA_KV * nq)
    half = SEL_BLOCK * 2 * NSA_KV
    per_page = PAGE // SEL_BLOCK

    def blk(b, h, qi, n, idx_ref):
        return idx_ref[((b * NSA_KV + h) * nq + qi) * nblk + n]

    def kv_map(n):
        def f(b, h, qi, idx_ref, pt_ref):
            s = blk(b, h, qi, n, idx_ref)
            return (pt_ref[b * npt + s // per_page] * per_page + s % per_page, 0)
        return f

    in_specs = [pl.BlockSpec((tq, NSA_G * HD), lambda b, h, qi, i_, p_: (b, h))]
    in_specs += [pl.BlockSpec((half, HD), kv_map(n)) for n in range(nblk)]
    in_specs += [
        pl.BlockSpec((1, SEL_BLOCK, HD), lambda b, h, qi, i_, p_: (b, 0, h)),
        pl.BlockSpec((1, SEL_BLOCK, HD), lambda b, h, qi, i_, p_: (b, 0, NSA_KV + h)),
        pl.BlockSpec((1, 1, ns, NSA_G, SEL_BLOCK), lambda b, h, qi, i_, p_: (qi, h, 0, 0, 0)),
        pl.BlockSpec((1, 1, 1, tq, NSA_G), lambda b, h, qi, i_, p_: (1, b, h, 0, 0)),
        pl.BlockSpec((tq, NSA_G * HD), lambda b, h, qi, i_, p_: (b, h)),
    ]
    grid_spec = pltpu.PrefetchScalarGridSpec(
        num_scalar_prefetch=2,
        grid=(batch, NSA_KV, nq),
        in_specs=in_specs,
        out_specs=pl.BlockSpec((tq, NSA_G * HD), lambda b, h, qi, i_, p_: (b, h)),
        scratch_shapes=[pltpu.VMEM((NSA_G * tq, HD), F32)],
    )
    kern = functools.partial(_selg_kernel, nblk=nblk, tq=tq, q_pos0=q_pos0, tail_pos0=(ns - 1) * SEL_BLOCK)
    return pl.pallas_call(
        kern,
        grid_spec=grid_spec,
        out_shape=jax.ShapeDtypeStruct((batch * tq, NSA_HEADS * HD), F32),
        compiler_params=_cparams(("parallel", "parallel", "arbitrary")),
        name="sel_gather",
    )(idx, ptab, q, *([cache_rows] * nblk), tail, tail, bias_blk, gate, prev)


def _rel_bucket(dist):
    n = jnp.maximum(dist, 0)
    max_exact = REL_BUCKETS // 2
    nf = jnp.maximum(n, 1).astype(F32)
    large = max_exact + (jnp.log(nf / max_exact) / math.log(REL_MAX_DIST / max_exact)
                         * (REL_BUCKETS - max_exact)).astype(jnp.int32)
    return jnp.where(n < max_exact, n, jnp.minimum(large, REL_BUCKETS - 1))


def _bias_by_distance(rel_bias):
    return rel_bias.astype(F32)[_rel_bucket(jnp.arange(REL_MAX_DIST))]


def _toeplitz(r, nrows, ncols):
    p = r.shape[-1]
    flat = jnp.tile(r, (1,) * (r.ndim - 1) + (nrows,))[..., :nrows * (p - 1)]
    return flat.reshape(r.shape[:-1] + (nrows, p - 1))[..., :ncols]


def _bias_tiles(rel_bias, tq, n_delta, transposed=False, d_neg=0):
    fd = _bias_by_distance(rel_bias).T
    lo = TK * (1 + d_neg)
    span = TK * n_delta + tq
    hi = span - REL_MAX_DIST
    padded = jnp.concatenate([jnp.tile(fd[:, :1], (1, lo)), fd[:, :span]]
                             + ([jnp.tile(fd[:, -1:], (1, hi))] if hi > 0 else []), axis=1)
    z = jnp.stack([padded[:, TK * d + 1:TK * d + TK + tq] for d in range(n_delta + d_neg)], axis=1)
    if transposed:
        r = jnp.roll(z, -(TK - 1), axis=2)
        return _toeplitz(r, TK, tq).reshape(NSA_KV, NSA_G, n_delta + d_neg, TK, tq)
    r = jnp.roll(z[:, :, ::-1], -(tq - 1), axis=2)
    t = _toeplitz(r, tq, TK)
    return t.reshape(NSA_KV, NSA_G, n_delta, tq, TK).transpose(0, 2, 1, 3, 4)


def _bias_cmp(rel_bias, q_pos0, tqs, ncp):
    fd = _bias_by_distance(rel_bias)
    last = CMP_BLOCK - 1
    if tqs <= SEL_BLOCK:
        dist = (q_pos0 + jnp.arange(tqs))[:, None] - (jnp.arange(ncp) * CMP_STRIDE + last)[None, :]
        b = fd[jnp.clip(dist, 0, REL_MAX_DIST - 1)]
        return b.reshape(tqs, ncp, NSA_KV, NSA_G).transpose(2, 3, 0, 1)
    assert q_pos0 == 0 and tqs % CMP_STRIDE == 0
    ntau = tqs // CMP_STRIDE
    period = ntau + ncp
    kappa = period - jnp.arange(period)
    dist = CMP_STRIDE * kappa[None, :] + jnp.arange(CMP_STRIDE)[:, None] - last
    dist = jnp.where(kappa[None, :] < ntau, dist, 0)
    r = fd[jnp.clip(dist, 0, REL_MAX_DIST - 1)]
    t = _toeplitz(r.transpose(2, 0, 1), ntau, ncp)
    return t.transpose(0, 2, 1, 3).reshape(NSA_KV, NSA_G, tqs, ncp)


def _bias_blocks(rel_bias, q_pos0, nq, ns):
    fd = _bias_by_distance(rel_bias)
    n = ns * SEL_BLOCK
    rows = []
    for qq in range(nq):
        t = q_pos0 + qq
        far = max(min(t - (REL_MAX_DIST - 1), n), 0)
        mid_hi = min(t + 1, n)
        parts = [jnp.tile(fd[-1:], (far, 1))] if far else []
        if mid_hi > far:
            parts.append(fd[t - mid_hi + 1:t - far + 1][::-1])
        if n > mid_hi:
            parts.append(jnp.tile(fd[:1], (n - mid_hi, 1)))
        rows.append(jnp.concatenate(parts, axis=0))
    t = jnp.stack(rows)
    return t.reshape(nq, ns, SEL_BLOCK, NSA_KV, NSA_G).transpose(0, 3, 1, 4, 2)


def _forward_group(x, conv0, ssm0, past, P, *, batch, seq, t_valid, q_pos0, tq):
    M = batch * seq
    conv_out, ssm_out = [], []
    for l in range(2):
        proj = _nmm(x, P["mix_norm"][l], P["gdn_w_main"][l], tn=1024, tm=1024)
        bg = _nmm(x, P["mix_norm"][l], P["gdn_w_gate"][l], tn=128, mode="gdn_gate", aux=P["gdn_gate_aux"][l],
                  seq=seq, t_valid=t_valid)
        o, s_new = _gdn(proj, bg, P["gdn_conv_w"][l], conv0[l], ssm0[l], P["gdn_out_norm"][l],
                        batch=batch, seq=seq)
        conv_out.append(proj.reshape(batch, seq, -1)[:, t_valid - (GDN_CONV - 1):t_valid, :GDN_CONV_DIM])
        ssm_out.append(s_new)
        x = _mm_res(o, P["gdn_w_out"][l], x, tn=1024)
        x = _mlp(x, P["mlp_norm"][l], P["mlp_w1"][l], P["mlp_w2"][l])
    x, cmp_rows, sel_rows, win_state = _nsa_layers(x, past, P, batch=batch, seq=seq, t_valid=t_valid,
                                                   q_pos0=q_pos0, tq=tq)
    return x, jnp.stack(conv_out), jnp.stack(ssm_out), cmp_rows, sel_rows, win_state


def _nsa_layers(x, past, P, *, batch, seq, t_valid, q_pos0, tq):
    M = batch * seq
    kv = _nmm(x, P["kv_norm"], P["nsa_w_kv"], tn=512, tm=1024, mode="headnorm", aux=P["kv_aux"],
              norm_tiles=(2, 4), n_split=3)
    kv4 = kv.reshape(3, batch, seq, 2 * NSA_KV * HD)
    new_cmp, new_sel, win_new = (kv4[br][:, :t_valid] for br in range(3))
    cmp_rows = new_cmp.reshape(batch, t_valid, 2, NSA_KV, HD)
    sel_rows = new_sel.reshape(batch, t_valid, 2, NSA_KV, HD)

    ident = jnp.arange(M // PAGE, dtype=jnp.int32) if seq % PAGE == 0 else None
    if past is None:
        n_tot = t_valid
        npages = seq // PAGE
        kv_pages = kv.reshape(3 * M // PAGE, PAGE, 2 * NSA_KV * HD)
        first = _cmp_stage1(kv_pages, ident, P["cmp_w1cat"], row_packed=False)
        nsb = npages // CMP_PPS
        nseg_tot = npages * (PAGE // CMP_STRIDE)
        f6 = first.reshape(batch, nsb, 2, NSA_KV, CMP_PPS * 8, 2, HD).transpose(2, 0, 3, 1, 4, 5, 6)
        f6 = f6.reshape(2, batch, NSA_KV, nseg_tot, 2, HD)
        win_seq = win_new
        w_pos0 = 0
    else:
        n_past = past["page_table"].shape[1] * PAGE
        n_tot = n_past + t_valid
        npages = n_past // PAGE
        ptab = past["page_table"].reshape(-1)
        first = _cmp_stage1(past["cmp_rows"], ptab, P["cmp_w1cat"], row_packed=True)
        nsb = npages // CMP_PPS
        f6 = first.reshape(batch, nsb, 2, NSA_KV, CMP_PPS * 8, 2, HD).transpose(2, 0, 3, 1, 4, 5, 6)
        f6 = f6.reshape(2, batch, NSA_KV, npages * 8, 2, HD)
        tail_cmp = jnp.pad(new_cmp, ((0, 0), (0, PAGE - t_valid), (0, 0)))
        tail_cmp = jnp.pad(tail_cmp, ((0, (-batch) % CMP_PPS), (0, 0), (0, 0)))
        tfirst = _cmp_stage1(tail_cmp, jnp.arange(tail_cmp.shape[0], dtype=jnp.int32), P["cmp_w1cat"],
                             row_packed=False)
        t6 = tfirst.reshape(-1, 2, NSA_KV, CMP_PPS, 8, 2, HD).transpose(1, 0, 3, 2, 4, 5, 6)
        t6 = t6.reshape(2, -1, NSA_KV, 8, 2, HD)[:, :batch, :, :(-(-t_valid // CMP_STRIDE))]
        f6 = jnp.concatenate([f6, t6], axis=3)
        nseg_tot = f6.shape[3]
        win_seq = jnp.concatenate([past["win"], win_new], axis=1)
        w_pos0 = q_pos0 + t_valid - win_seq.shape[1]
    nc = -(-n_tot // CMP_STRIDE) - 1
    ns = -(-n_tot // SEL_BLOCK)
    ncp = -(-nc // 128) * 128
    nsp = -(-ns // 128) * 128
    a = f6[:, :, :, 0:nc, 0, :]
    b = f6[:, :, :, 1:nc + 1, 1, :]
    if b.shape[3] < nc:
        b = jnp.pad(b, ((0, 0), (0, 0), (0, 0), (0, nc - b.shape[3]), (0, 0)))
    a = jnp.pad(a, ((0, 0), (0, 0), (0, 0), (0, ncp - nc), (0, 0))).reshape(2, batch * NSA_KV * ncp, HD)
    b = jnp.pad(b, ((0, 0), (0, 0), (0, 0), (0, ncp - nc), (0, 0))).reshape(2, batch * NSA_KV * ncp, HD)
    R = batch * NSA_KV * ncp
    kcvc = _cmp_stage2(a, b, P["cmp_pe8"], P["cmp_w1flat"], P["cmp_b1"], P["cmp_w2"], P["cmp_b2"],
                       P["k_cmp_norm"], tr=min(R, 2048))
    kcvc = kcvc.reshape(2, batch, NSA_KV, ncp, HD)

    n_keep = min(WINDOW, win_seq.shape[1])
    win_state = win_seq[:, win_seq.shape[1] - n_keep:].reshape(batch, n_keep, 2, NSA_KV, HD)

    seq_q = seq if past is None else tq
    bias_c = _bias_cmp(P["rel_bias"], q_pos0, seq_q, ncp)
    n_delta = min(N_DELTA, (q_pos0 + seq_q) // TK + 1)
    if past is None:
        tqf = FLASH_TQ
        btiles_t = _bias_tiles(P["rel_bias"], tqf, n_delta, transposed=True, d_neg=tqf // TK - 1)
        sel_pages, sel_ptab, sel_npt = kv_pages, ident + M // PAGE, seq // PAGE
        sel_kcol, sel_vcol = 0, NSA_KV
        sel_pps = 4
        sel_tile_of = lambda i, s, pp: s * sel_pps + pp
        win_pages, win_ptab, win_npt = kv_pages, ident + 2 * (M // PAGE), seq // PAGE
        win_kcol, win_vcol = 0, NSA_KV
        win_pps = WINDOW // TK + tqf // TK
        win_tile_of = lambda i, s, pp: (i * tqf) // TK - WINDOW // TK + pp
        win_kbase = 0
    else:
        btiles = _bias_tiles(P["rel_bias"], tq, n_delta)
        assert n_past % SEL_BLOCK == 0 and t_valid <= SEL_BLOCK and ns - 1 > SEL_TOPK
        tail_sel = jnp.pad(new_sel, ((0, 0), (0, SEL_BLOCK - t_valid), (0, 0)))
        bias_blk = _bias_blocks(P["rel_bias"], q_pos0, t_valid, ns)
        nwt = -(-win_seq.shape[1] // TK)
        win_pages = jnp.pad(win_seq, ((0, 0), (0, nwt * TK - win_seq.shape[1]), (0, 0)))
        win_pages = win_pages.reshape(batch * nwt, TK, 2 * NSA_KV * HD)
        win_ptab, win_npt = jnp.arange(batch * nwt, dtype=jnp.int32), nwt
        win_kcol, win_vcol = 0, 4
        win_pps = nwt
        win_tile_of = lambda i, s, pp: pp
        win_active = lambda i, s, pp: True
        win_kbase = w_pos0

    for jj in range(2):
        l = 2 + jj
        q = _nmm(x, P["mix_norm"][l], P["nsa_w_q"][jj], tn=512, tm=1024, out_dtype=BF16, mode="headnorm",
                 aux=P["nsa_q_aux"][jj], scale=HD ** -0.5)
        gates = _nmm(x, P["mix_norm"][l], P["nsa_w_g"][jj], tn=128, mode="sigmoid")
        gate = gates[:, :NSA_HEADS * 3].reshape(batch, seq, NSA_KV, NSA_G, 3).transpose(4, 0, 2, 1, 3)
        if seq_q != seq:
            q = q.reshape(batch, seq, -1)[:, :seq_q].reshape(batch * seq_q, -1)
            gate = gate[:, :, :, :seq_q]
        if past is None:
            gate_t = gates[:, :NSA_HEADS * 3].reshape(batch, seq, NSA_KV, NSA_G, 3).transpose(4, 0, 2, 3, 1)
            o_c, sel = _attn_cmp(q, kcvc, bias_c, gate, batch=batch, seq=seq_q, tq=tq, q_pos0=q_pos0, nc=nc,
                                 ns=ns, nsp=nsp, sel_t=True)
            o_s = _flasht(q, sel_pages, sel_ptab, sel_kcol, sel_vcol, btiles_t, sel, gate_t, 1, o_c,
                          batch=batch, seq=seq_q, tq=tqf, q_pos0=q_pos0, pps=sel_pps,
                          steps_of=lambda i: (i * tqf + tqf - 1) // (TK * sel_pps) + 1,
                          tile_of=sel_tile_of, kbase=0, w_pos0=0, npt=sel_npt, out_dtype=F32)
            o_w = _flasht(q, win_pages, win_ptab, win_kcol, win_vcol, btiles_t, None, gate_t, 2, o_s,
                          batch=batch, seq=seq_q, tq=tqf, q_pos0=q_pos0, pps=win_pps, steps_of=lambda i: 1,
                          tile_of=win_tile_of, kbase=win_kbase, w_pos0=w_pos0, npt=win_npt, out_dtype=BF16)
        else:
            o_c, sel, idx = _attn_cmp(q, kcvc, bias_c, gate, batch=batch, seq=seq_q, tq=tq, q_pos0=q_pos0,
                                      nc=nc, ns=ns, nsp=nsp, n_idx=SEL_TOPK - 1)
            o_s = _sel_gather(q, past["sel_rows"], past["page_table"].reshape(-1),
                              idx[:, :, :t_valid, :SEL_TOPK - 1].reshape(-1), bias_blk, tail_sel, gate, o_c,
                              batch=batch, tq=tq, nq=t_valid, q_pos0=q_pos0, npt=npages, ns=ns)
            o_w = _flash(q, win_pages, win_ptab, win_kcol, win_vcol, btiles, None, None, gate, 2, o_s,
                         batch=batch, seq=seq_q, tq=tq, q_pos0=q_pos0, pps=win_pps,
                         nsteps=1, tile_of=win_tile_of, active=win_active, kbase=win_kbase, w_pos0=w_pos0,
                         npt=win_npt, tail_tile=0, out_dtype=F32)
        if seq_q != seq:
            o_w = jnp.pad(o_w.reshape(batch, seq_q, -1), ((0, 0), (0, seq - seq_q), (0, 0))).reshape(M, -1)
        x = _mm_res(o_w.astype(BF16), P["nsa_w_out"][jj], x, tm=1024, tn=1024)
        x = _mlp(x, P["mlp_norm"][l], P["mlp_w1"][l], P["mlp_w2"][l])
    return x, cmp_rows, sel_rows, win_state


def _prepare_params(mix_norm, mlp_norm, mlp_w1, mlp_w2, gdn_w_in, gdn_conv_w, gdn_a_log, gdn_dt_bias,
                    gdn_out_norm, gdn_w_out, kv_norm, nsa_w_kv, k_sel_norm, k_win_norm, k_cmp_norm, cmp_pe,
                    cmp_w1, cmp_b1, cmp_w2, cmp_b2, nsa_w_in, nsa_q_norm, nsa_w_out, rel_bias):
    n_lay = gdn_w_in.shape[0]
    main = GDN_CONV_DIM + GDN_VAL_DIM
    zpad = lambda n: jnp.zeros((1, n), F32)
    gate_aux = jnp.stack([
        jnp.concatenate([
            jnp.concatenate([zpad(GDN_V_HEADS), gdn_a_log[l][None].astype(F32), zpad(HD - 2 * GDN_V_HEADS)], 1),
            jnp.concatenate([zpad(GDN_V_HEADS), gdn_dt_bias[l][None].astype(F32), zpad(HD - 2 * GDN_V_HEADS)], 1),
        ], 0)[None] for l in range(n_lay)])
    tile4 = lambda w: jnp.tile(w.astype(F32), NSA_KV)[None, None]
    kv_aux = jnp.concatenate([jnp.ones((2, 1, 512), F32), tile4(k_sel_norm), jnp.ones((1, 1, 512), F32),
                              tile4(k_win_norm), jnp.ones((1, 1, 512), F32)], 0)
    nq = NSA_HEADS * HD
    w1r = cmp_w1.reshape(2, 2, CMP_STRIDE, HD, HD)
    P = dict(
        mix_norm=mix_norm, mlp_norm=mlp_norm,
        mlp_w1=mlp_w1.astype(BF16), mlp_w2=mlp_w2.astype(BF16),
        gdn_w_main=gdn_w_in[:, :, :main].astype(BF16),
        gdn_w_gate=jnp.pad(gdn_w_in[:, :, main:], ((0, 0), (0, 0), (0, HD - 2 * GDN_V_HEADS))).astype(BF16),
        gdn_gate_aux=gate_aux, gdn_conv_w=gdn_conv_w, gdn_out_norm=gdn_out_norm,
        gdn_w_out=gdn_w_out.astype(BF16),
        kv_norm=kv_norm, nsa_w_kv=nsa_w_kv.astype(BF16), kv_aux=kv_aux, k_cmp_norm=k_cmp_norm,
        cmp_w1cat=jnp.concatenate([w1r[:, 0], w1r[:, 1]], axis=-1).astype(BF16),
        cmp_w1flat=cmp_w1.reshape(2, CMP_BLOCK * HD, HD).astype(BF16),
        cmp_pe8=jnp.pad(cmp_pe.reshape(2, 1, CMP_BLOCK * HD), ((0, 0), (0, 7), (0, 0))),
        cmp_b1=cmp_b1, cmp_w2=cmp_w2.astype(BF16), cmp_b2=cmp_b2,
        nsa_w_q=nsa_w_in[:, :, :nq].astype(BF16),
        nsa_w_g=jnp.pad(nsa_w_in[:, :, nq:], ((0, 0), (0, 0), (0, HD - 3 * NSA_HEADS))).astype(BF16),
        nsa_q_aux=jnp.stack([jnp.tile(tile4(nsa_q_norm[jj]), (nq // 512, 1, 1)) for jj in range(2)]),
        nsa_w_out=nsa_w_out.astype(BF16), rel_bias=rel_bias,
    )
    return P


def kernel(x_prompt, x_sample, state_conv, state_ssm, cache_cmp, cache_sel, cache_win, page_table, mix_norm,
           mlp_norm, mlp_w1, mlp_w2, gdn_w_in, gdn_conv_w, gdn_a_log, gdn_dt_bias, gdn_out_norm, gdn_w_out,
           kv_norm, nsa_w_kv, k_sel_norm, k_win_norm, k_cmp_norm, cmp_pe, cmp_w1, cmp_b1, cmp_w2, cmp_b2,
           nsa_w_in, nsa_q_norm, nsa_w_out, rel_bias):
    bp, tp, _ = x_prompt.shape
    bs, ts, _ = x_sample.shape
    n_lay = gdn_w_in.shape[0]
    P = _prepare_params(mix_norm, mlp_norm, mlp_w1, mlp_w2, gdn_w_in, gdn_conv_w, gdn_a_log, gdn_dt_bias,
                        gdn_out_norm, gdn_w_out, kv_norm, nsa_w_kv, k_sel_norm, k_win_norm, k_cmp_norm, cmp_pe,
                        cmp_w1, cmp_b1, cmp_w2, cmp_b2, nsa_w_in, nsa_q_norm, nsa_w_out, rel_bias)

    conv0 =jnp.zeros((n_lay, bp, GDN_CONV - 1, GDN_CONV_DIM), F32)
    ssm0 = jnp.zeros((n_lay, bp, GDN_V_HEADS, HD, HD), F32)
    yp, conv_p, ssm_p, cmp_p, sel_p, win_p = _forward_group(
        x_prompt.reshape(bp * tp, D_MODEL), conv0, ssm0, None, P,
        batch=bp, seq=tp, t_valid=tp, q_pos0=0, tq=128)

    seq_s = GDN_CHUNK
    xs = jnp.pad(x_sample, ((0, 0), (0, seq_s - ts), (0, 0))).reshape(bs * seq_s, D_MODEL)
    n_pool = cache_cmp.shape[0]
    past = dict(cmp_rows=cache_cmp.reshape(n_pool * PAGE * 2 * NSA_KV, HD),
                sel_rows=cache_sel.reshape(n_pool * PAGE * 2 * NSA_KV, HD),
                page_table=page_table.astype(jnp.int32),
                win=cache_win.reshape(bs, cache_win.shape[1], 2 * NSA_KV * HD))
    n_past = page_table.shape[1] * PAGE
    ys, conv_s, ssm_s, cmp_s, sel_s, win_s = _forward_group(
        xs, state_conv, state_ssm, past, P, batch=bs, seq=seq_s, t_valid=ts, q_pos0=n_past, tq=8)
    y_sample = ys.reshape(bs, seq_s, D_MODEL)[:, :ts]
    return (yp.reshape(bp, tp, D_MODEL), y_sample, conv_p, ssm_p, cmp_p, sel_p, win_p,
            conv_s, ssm_s, cmp_s, sel_s, win_s)
```

```python
import functools
import math

import jax
import jax.numpy as jnp
from jax import lax
from jax.experimental import pallas as pl
from jax.experimental.pallas import tpu as pltpu

F32 = jnp.float32
BF16 = jnp.bfloat16

D_MODEL = 2048
D_FF = 4 * D_MODEL
NORM_EPS = 1e-6
L2_EPS = 1e-6
PAGE = 128

HD = 128
GDN_QK_HEADS = 16
GDN_V_HEADS = 32
GDN_KEY_DIM = GDN_QK_HEADS * HD
GDN_VAL_DIM = GDN_V_HEADS * HD
GDN_CONV = 4
GDN_CHUNK = 64
GDN_CONV_DIM = 2 * GDN_KEY_DIM + GDN_VAL_DIM

NSA_HEADS = 16
NSA_KV = 4
NSA_G = NSA_HEADS // NSA_KV
CMP_BLOCK = 32
CMP_STRIDE = 16
SEL_BLOCK = 64
SEL_TOPK = 16
WINDOW = 512
REL_BUCKETS = 32
REL_MAX_DIST = 4096
NEG = -1e30

TK = 128
FLASH_TQ = 256
N_DELTA = REL_MAX_DIST // TK + 2

VMEM_LIMIT = 56 * 1024 * 1024


def _cparams(sem):
    return pltpu.CompilerParams(dimension_semantics=sem, vmem_limit_bytes=VMEM_LIMIT)


def _sigmoid(x):
    return 1.0 / (1.0 + jnp.exp(-x))


def _softplus(x):
    return jnp.maximum(x, 0.0) + jnp.log(1.0 + jnp.exp(-jnp.abs(x)))


def _dot(a, b):
    return jnp.dot(a.astype(BF16), b.astype(BF16), preferred_element_type=F32)


def _dot_nt(a, b):
    return lax.dot_general(a.astype(BF16), b.astype(BF16), (((1,), (1,)), ((), ())),
                           preferred_element_type=F32)


def _dot_tn(a, b):
    return lax.dot_general(a.astype(BF16), b.astype(BF16), (((0,), (0,)), ((), ())),
                           preferred_element_type=F32)


def _headnorm(acc, gw):
    parts = []
    for g in range(acc.shape[1] // HD):
        a = acc[:, g * HD:(g + 1) * HD]
        parts.append(a * lax.rsqrt(jnp.mean(a * a, axis=-1, keepdims=True) + NORM_EPS))
    return jnp.concatenate(parts, axis=1) * gw


def _nmm_kernel(x_ref, nw_ref, w_ref, aux_ref, o_ref, h_ref, *, mode, norm_tiles, scale, seq, t_valid):
    i = pl.program_id(0)
    j = pl.program_id(1)

    @pl.when(j == 0)
    def _():
        x = x_ref[...]
        h = x * lax.rsqrt(jnp.mean(x * x, axis=-1, keepdims=True) + NORM_EPS) * nw_ref[...]
        h_ref[...] = h.astype(BF16)

    acc = jnp.dot(h_ref[...], w_ref[...], preferred_element_type=F32)
    if mode == "plain":
        o_ref[...] = acc.astype(o_ref.dtype)
    elif mode == "headnorm":
        if norm_tiles is None:
            o_ref[...] = (_headnorm(acc, aux_ref[0]) * scale).astype(o_ref.dtype)
        else:
            is_n = functools.reduce(jnp.logical_or, [j == t for t in norm_tiles])

            @pl.when(is_n)
            def _():
                o_ref[...] = (_headnorm(acc, aux_ref[0]) * scale).astype(o_ref.dtype)

            @pl.when(jnp.logical_not(is_n))
            def _():
                o_ref[...] = acc.astype(o_ref.dtype)
    elif mode == "sigmoid":
        o_ref[...] = _sigmoid(acc)
    elif mode == "gdn_gate":
        tm = acc.shape[0]
        aux = aux_ref[0]
        lane = lax.broadcasted_iota(jnp.int32, acc.shape, 1)
        row = lax.broadcasted_iota(jnp.int32, acc.shape, 0) + i * tm
        live = (row % seq) < t_valid
        beta = jnp.where(live, _sigmoid(acc), 0.0)
        g = jnp.where(live, -jnp.exp(aux[0:1, :]) * _softplus(acc + aux[1:2, :]), 0.0)
        g = jnp.where((lane >= GDN_V_HEADS) & (lane < 2 * GDN_V_HEADS), g, 0.0)
        r = lax.broadcasted_iota(jnp.int32, (tm, tm), 0)
        c = lax.broadcasted_iota(jnp.int32, (tm, tm), 1)
        tri = ((r // GDN_CHUNK) == (c // GDN_CHUNK)) & (c <= r)
        gcum = jnp.dot(jnp.where(tri, 1.0, 0.0), g, preferred_element_type=F32,
                       precision=lax.Precision.HIGHEST)
        o_ref[...] = jnp.where(lane < GDN_V_HEADS, beta, gcum)
    else:
        raise ValueError(mode)


def _nmm(x, nw, w, *, tn, out_dtype=F32, mode="plain", aux=None, norm_tiles=None, scale=1.0,
         seq=1, t_valid=1, tm=512, n_split=1):
    M, K = x.shape
    N = w.shape[1]
    tm = min(tm, M)
    assert M % tm == 0 and N % (tn * n_split) == 0
    if aux is None:
        aux = jnp.zeros((N // tn, 1, tn), F32)
    if n_split == 1:
        out_spec = pl.BlockSpec((tm, tn), lambda i, j: (i, j))
        out_shape = jax.ShapeDtypeStruct((M, N), out_dtype)
    else:
        per = N // n_split // tn
        out_spec = pl.BlockSpec((None, tm, tn), lambda i, j: (j // per, i, j % per))
        out_shape = jax.ShapeDtypeStruct((n_split, M, N // n_split), out_dtype)
    kern = functools.partial(_nmm_kernel, mode=mode, norm_tiles=norm_tiles, scale=scale, seq=seq,
                             t_valid=t_valid)
    return pl.pallas_call(
        kern,
        grid=(M // tm, N // tn),
        in_specs=[
            pl.BlockSpec((tm, K), lambda i, j: (i, 0)),
            pl.BlockSpec((1, K), lambda i, j: (0, 0)),
            pl.BlockSpec((K, tn), lambda i, j: (0, j)),
            pl.BlockSpec((1,) + aux.shape[1:], lambda i, j: (j, 0, 0)),
        ],
        out_specs=out_spec,
        out_shape=out_shape,
        scratch_shapes=[pltpu.VMEM((tm, K), BF16)],
        compiler_params=_cparams(("parallel", "arbitrary")),
        name="nmm_" + mode,
    )(x, nw.reshape(1, K), w, aux)


def _mmres_kernel(x_ref, w_ref, r_ref, o_ref):
    o_ref[...] = r_ref[...] + jnp.dot(x_ref[...], w_ref[...], preferred_element_type=F32)


def _mm_res(x, w, res, *, tm=512, tn=512):
    M, K = x.shape
    N = w.shape[1]
    tm = min(tm, M)
    return pl.pallas_call(
        _mmres_kernel,
        grid=(M // tm, N // tn),
        in_specs=[
            pl.BlockSpec((tm, K), lambda i, j: (i, 0)),
            pl.BlockSpec((K, tn), lambda i, j: (0, j)),
            pl.BlockSpec((tm, tn), lambda i, j: (i, j)),
        ],
        out_specs=pl.BlockSpec((tm, tn), lambda i, j: (i, j)),
        out_shape=jax.ShapeDtypeStruct((M, N), F32),
        compiler_params=_cparams(("parallel", "arbitrary")),
        name="mm_res",
    )(x, w, res)


def _mlp_kernel(x_ref, nw_ref, w1_ref, w2_ref, o_ref, h_ref, acc_ref):
    f = pl.program_id(1)

    @pl.when(f == 0)
    def _():
        x = x_ref[...]
        h = x * lax.rsqrt(jnp.mean(x * x, axis=-1, keepdims=True) + NORM_EPS) * nw_ref[...]
        h_ref[...] = h.astype(BF16)
        acc_ref[...] = x

    a = jnp.maximum(jnp.dot(h_ref[...], w1_ref[...], preferred_element_type=F32), 0.0)
    acc_ref[...] += jnp.dot((a * a).astype(BF16), w2_ref[...], preferred_element_type=F32)

    @pl.when(f == pl.num_programs(1) - 1)
    def _():
        o_ref[...] = acc_ref[...]


def _mlp(x, nw, w1, w2, *, tm=512, tf=1024):
    M, D = x.shape
    Fdim = w1.shape[1]
    tm = min(tm, M)
    return pl.pallas_call(
        _mlp_kernel,
        grid=(M // tm, Fdim // tf),
        in_specs=[
            pl.BlockSpec((tm, D), lambda i, f: (i, 0)),
            pl.BlockSpec((1, D), lambda i, f: (0, 0)),
            pl.BlockSpec((D, tf), lambda i, f: (0, f)),
            pl.BlockSpec((tf, D), lambda i, f: (f, 0)),
        ],
        out_specs=pl.BlockSpec((tm, D), lambda i, f: (i, 0)),
        out_shape=jax.ShapeDtypeStruct((M, D), F32),
        scratch_shapes=[pltpu.VMEM((tm, D), BF16), pltpu.VMEM((tm, D), F32)],
        compiler_params=_cparams(("parallel", "arbitrary")),
        name="mlp",
    )(x, nw.reshape(1, D), w1, w2)


def _unit_lower_inverse(mats, r, c):
    eye = jnp.where(r == c, 1.0, 0.0)
    in8 = (r // 8) == (c // 8)
    d0 = [jnp.where(in8, a, 0.0) for a in mats]
    d2 = [_dot(d, d) for d in d0]
    d4 = [_dot(d, d) for d in d2]
    x = [_dot(eye - a, eye + b) for a, b in zip(d0, d2)]
    x = [_dot(a, eye + b) for a, b in zip(x, d4)]
    s = 8
    while s < GDN_CHUNK:
        off = ((r // (2 * s)) == (c // (2 * s))) & ((r // s) != (c // s))
        bx = [_dot(jnp.where(off, a, 0.0), xi) for a, xi in zip(mats, x)]
        xbx = [_dot(xi, b) for xi, b in zip(x, bx)]
        x = [xi - b for xi, b in zip(x, xbx)]
        s *= 2
    return x


GDN_PAIRS = 8


def _gdn_chunk(qn, kn, vc, zb, beta, gc, st, onw):
    C = GDN_CHUNK
    R = 2 * C
    n = len(qn)
    rcol = lax.broadcasted_iota(jnp.int32, (R, 1), 0)
    top = rcol < C
    r = lax.broadcasted_iota(jnp.int32, (R, R), 0)
    c = lax.broadcasted_iota(jnp.int32, (R, R), 1)
    same = (r // C) == (c // C)
    low = same & (c <= r)
    slow = same & (c < r)
    srow = lax.broadcasted_iota(jnp.int32, (2 * HD, 1), 0)

    def blocked(a):
        return jnp.concatenate([jnp.where(top, a, 0.0), jnp.where(top, 0.0, a)], axis=1)

    beta2 = [jnp.concatenate(b, axis=0) for b in beta]
    gc2 = [jnp.concatenate(g, axis=0) for g in gc]
    gl2 = [jnp.where(top, g[0][C - 1:C, :], g[1][C - 1:C, :]) for g in gc]
    gls = [jnp.exp(jnp.where(srow < HD, g[0][C - 1:C, :], g[1][C - 1:C, :])) for g in gc]
    dec = []
    for g2 in gc2:
        colm = jnp.broadcast_to(g2, (R, R))
        dec.append(jnp.exp(jnp.where(low, colm - colm.T, NEG)))
    k2 = [jnp.concatenate([k, k], axis=0) for k in kn]
    q2 = [jnp.concatenate([q, q], axis=0) for q in qn]
    v2 = [jnp.concatenate([v[:, :HD], v[:, HD:]], axis=0) for v in vc]
    kk = [_dot_nt(k, k) for k in k2]
    qk = [_dot_nt(q, k) for q, k in zip(q2, k2)]
    amat = [jnp.where(slow, kk[i] * beta2[i] * dec[i], 0.0) for i in range(n)]
    attn = [qk[i] * dec[i] for i in range(n)]
    tinv = _unit_lower_inverse(amat, r, c)

    e2 = [jnp.exp(g) for g in gc2]
    rhs = [jnp.concatenate([v2[i] * beta2[i], k2[i] * beta2[i] * e2[i]], axis=1) for i in range(n)]
    sol = [_dot(tinv[i], rhs[i]) for i in range(n)]
    lhs = [jnp.concatenate([blocked(sol[i][:, HD:]), blocked(q2[i] * e2[i])], axis=0) for i in range(n)]
    ws = [_dot(lhs[i], st[i]) for i in range(n)]
    vnew = [sol[i][:, :HD] - ws[i][:R] for i in range(n)]
    av = [_dot(attn[i], vnew[i]) for i in range(n)]
    kd = [blocked(k2[i] * jnp.exp(gl2[i] - gc2[i])) for i in range(n)]
    kv = [_dot_tn(kd[i], vnew[i]) for i in range(n)]
    st_new = [st[i] * gls[i] + kv[i] for i in range(n)]

    outs = []
    for i in range(n):
        o2 = ws[i][R:] + av[i]
        z2 = jnp.concatenate([zb[i][:, :HD], zb[i][:, HD:]], axis=0)
        on = o2 * lax.rsqrt(jnp.mean(o2 * o2, axis=-1, keepdims=True) + NORM_EPS) * onw
        out2 = on * (z2 * _sigmoid(z2))
        outs.append(jnp.concatenate([out2[:C], out2[C:]], axis=1))
    return outs, st_new


def _gdn_kernel(q_ref, k_ref, v_ref, z_ref, bg_ref, wq_ref, wk_ref, wv_ref, cq_ref, ck_ref, cv_ref,
                s0_ref, onw_ref, o_ref, sout_ref, st_ref, bq_ref, bk_ref, bv_ref, *, single_chunk):
    C = GDN_CHUNK
    G = GDN_PAIRS
    jg = pl.program_id(1)
    ch = pl.program_id(2)

    def load_state():
        for p in range(G):
            st_ref[p, 0:HD, :] = s0_ref[0, 2 * p]
            st_ref[p, HD:2 * HD, :] = s0_ref[0, 2 * p + 1]
        bq_ref[5:8, :] = cq_ref[0]
        bk_ref[5:8, :] = ck_ref[0]
        bv_ref[5:8, :] = cv_ref[0]

    if single_chunk:
        load_state()
    else:
        pl.when(ch == 0)(load_state)

    def conv_silu(x_ref, buf_ref, w_ref):
        buf_ref[8:8 + C, :] = x_ref[...]
        w = w_ref[...]
        y = w[0:1, :] * buf_ref[5:5 + C, :]
        for t in range(1, GDN_CONV):
            y = y + w[t:t + 1, :] * buf_ref[5 + t:5 + t + C, :]
        buf_ref[5:8, :] = buf_ref[5 + C:8 + C, :]
        return y * _sigmoid(y)

    qc = conv_silu(q_ref, bq_ref, wq_ref)
    kc = conv_silu(k_ref, bk_ref, wk_ref)
    vc = conv_silu(v_ref, bv_ref, wv_ref)
    zb = z_ref[...]
    bg = bg_ref[...]
    lane = lax.broadcasted_iota(jnp.int32, bg.shape, 1)

    def col(idx):
        return jnp.sum(jnp.where(lane == idx, bg, 0.0), axis=-1, keepdims=True)

    qn, kn, beta, gc = [], [], [], []
    for p in range(G):
        qp = qc[:, p * HD:(p + 1) * HD]
        kp = kc[:, p * HD:(p + 1) * HD]
        qn.append(qp * lax.rsqrt(jnp.sum(qp * qp, axis=-1, keepdims=True) + L2_EPS) * (HD ** -0.5))
        kn.append(kp * lax.rsqrt(jnp.sum(kp * kp, axis=-1, keepdims=True) + L2_EPS))
        head = 2 * (jg * G + p)
        beta.append((col(head), col(head + 1)))
        gc.append((col(GDN_V_HEADS + head), col(GDN_V_HEADS + head + 1)))
    outs, new_states = _gdn_chunk(
        qn, kn, [vc[:, 2 * p * HD:2 * (p + 1) * HD] for p in range(G)],
        [zb[:, 2 * p * HD:2 * (p + 1) * HD] for p in range(G)], beta, gc,
        [st_ref[p] for p in range(G)], onw_ref[...])
    for p in range(G):
        st_ref[p] = new_states[p]
    o_ref[...] = jnp.concatenate(outs, axis=1).astype(o_ref.dtype)

    def write_state():
        for p in range(G):
            sout_ref[0, 2 * p] = new_states[p][:HD]
            sout_ref[0, 2 * p + 1] = new_states[p][HD:]

    if single_chunk:
        write_state()
    else:
        pl.when(ch == pl.num_programs(2) - 1)(write_state)


def _gdn(proj, bg, conv_w, conv0, ssm0, out_norm, *, batch, seq):
    C = GDN_CHUNK
    G = GDN_PAIRS
    nch = seq // C
    ng = GDN_QK_HEADS // G
    row = lambda b, j, c: b * nch + c
    return pl.pallas_call(
        functools.partial(_gdn_kernel, single_chunk=nch == 1),
        grid=(batch, ng, nch),
        in_specs=[
            pl.BlockSpec((C, G * HD), lambda b, j, c: (row(b, j, c), j)),
            pl.BlockSpec((C, G * HD), lambda b, j, c: (row(b, j, c), ng + j)),
            pl.BlockSpec((C, 2 * G * HD), lambda b, j, c: (row(b, j, c), ng + j)),
            pl.BlockSpec((C, 2 * G * HD), lambda b, j, c: (row(b, j, c), 2 * ng + j)),
            pl.BlockSpec((C, HD), lambda b, j, c: (row(b, j, c), 0)),
            pl.BlockSpec((GDN_CONV, G * HD), lambda b, j, c: (0, j)),
            pl.BlockSpec((GDN_CONV, G * HD), lambda b, j, c: (0, ng + j)),
            pl.BlockSpec((GDN_CONV, 2 * G * HD), lambda b, j, c: (0, ng + j)),
            pl.BlockSpec((1, GDN_CONV - 1, G * HD), lambda b, j, c: (b, 0, j)),
            pl.BlockSpec((1, GDN_CONV - 1, G * HD), lambda b, j, c: (b, 0, ng + j)),
            pl.BlockSpec((1, GDN_CONV - 1, 2 * G * HD), lambda b, j, c: (b, 0, ng + j)),
            pl.BlockSpec((1, 2 * G, HD, HD), lambda b, j, c: (b, j, 0, 0)),
            pl.BlockSpec((1, HD), lambda b, j, c: (0, 0)),
        ],
        out_specs=[
            pl.BlockSpec((C, 2 * G * HD), lambda b, j, c: (row(b, j, c), j)),
            pl.BlockSpec((1, 2 * G, HD, HD), lambda b, j, c: (b, j, 0, 0)),
        ],
        out_shape=[
            jax.ShapeDtypeStruct((batch * seq, GDN_VAL_DIM), BF16),
            jax.ShapeDtypeStruct((batch, GDN_V_HEADS, HD, HD), F32),
        ],
        scratch_shapes=[
            pltpu.VMEM((G, 2 * HD, HD), F32),
            pltpu.VMEM((8 + C, G * HD), F32),
            pltpu.VMEM((8 + C, G * HD), F32),
            pltpu.VMEM((8 + C, 2 * G * HD), F32),
        ],
        compiler_params=_cparams(("parallel", "parallel", "arbitrary")),
        name="gdn",
    )(proj, proj, proj, proj, bg, conv_w, conv_w, conv_w, conv0, conv0, conv0, ssm0,
      out_norm.reshape(1, HD))


CMP_PPS = 8


def _cmp1_kernel(pt_ref, *refs, row_packed):
    page_refs = refs[:CMP_PPS]
    w_ref = refs[CMP_PPS]
    o_ref = refs[CMP_PPS + 1]
    nseg = PAGE // CMP_STRIDE
    nch = 2 * NSA_KV
    pr = lax.broadcasted_iota(jnp.int32, (PAGE, PAGE), 0)
    pc = lax.broadcasted_iota(jnp.int32, (PAGE, PAGE), 1)
    perm = jnp.where(pc == (pr % nseg) * CMP_STRIDE + pr // nseg, 1.0, 0.0).astype(BF16)

    def slab(p, ch):
        if row_packed:
            return p[pl.ds(ch, PAGE, stride=nch), :]
        return p[0, :, ch * HD:(ch + 1) * HD]

    perm_slabs = [[jnp.dot(perm, slab(p, ch).astype(BF16), preferred_element_type=F32) for ch in range(nch)]
                  for p in page_refs]
    for cc in range(2):
        acc = jnp.zeros((NSA_KV * CMP_PPS * nseg, 2 * HD), F32)
        for rp in range(CMP_STRIDE // 2):
            lhs = jnp.concatenate(
                [jnp.concatenate([ps[cc * NSA_KV + h][(2 * rp) * nseg:(2 * rp + 1) * nseg],
                                  ps[cc * NSA_KV + h][(2 * rp + 1) * nseg:(2 * rp + 2) * nseg]], axis=1)
                 for h in range(NSA_KV) for ps in perm_slabs], axis=0)
            acc = acc + jnp.dot(lhs.astype(BF16), w_ref[cc, rp], preferred_element_type=F32)
        for h in range(NSA_KV):
            o_ref[cc, h] = acc[h * CMP_PPS * nseg:(h + 1) * CMP_PPS * nseg]


def _cmp_stage1(pages, ptab, w1cat, *, row_packed):
    n = ptab.shape[0]
    nst = n // CMP_PPS
    nseg = PAGE // CMP_STRIDE
    if row_packed:
        specs = [pl.BlockSpec((PAGE * 2 * NSA_KV, HD), lambda s, pt, p=p: (pt[s * CMP_PPS + p], 0))
                 for p in range(CMP_PPS)]
    else:
        specs = [pl.BlockSpec((1, PAGE, 2 * NSA_KV * HD), lambda s, pt, p=p: (pt[s * CMP_PPS + p], 0, 0))
                 for p in range(CMP_PPS)]
    grid_spec = pltpu.PrefetchScalarGridSpec(
        num_scalar_prefetch=1,
        grid=(nst,),
        in_specs=specs + [pl.BlockSpec((2, CMP_STRIDE // 2, 2 * HD, 2 * HD), lambda s, pt: (0, 0, 0, 0))],
        out_specs=pl.BlockSpec((2, NSA_KV, CMP_PPS * nseg, 2 * HD), lambda s, pt: (0, 0, s, 0)),
    )
    return pl.pallas_call(
        functools.partial(_cmp1_kernel, row_packed=row_packed),
        grid_spec=grid_spec,
        out_shape=jax.ShapeDtypeStruct((2, NSA_KV, n * nseg, 2 * HD), F32),
        compiler_params=_cparams(("arbitrary",)),
        name="cmp_stage1",
    )(ptab, *([pages] * CMP_PPS), w1cat.reshape(2, CMP_STRIDE // 2, 2 * HD, 2 * HD))


def _cmp2_kernel(a_ref, b_ref, pe_ref, w1_ref, b1_ref, w2_ref, b2_ref, nw_ref, o_ref):
    cc = pl.program_id(0)
    pe = pe_ref[0]
    pec = jnp.dot(pe.astype(BF16), w1_ref[0], preferred_element_type=F32)[0:1, :]
    hid = a_ref[0] + b_ref[0] + pec + b1_ref[0]
    hid = hid * _sigmoid(hid)
    out = jnp.dot(hid.astype(BF16), w2_ref[0], preferred_element_type=F32) + b2_ref[0]

    @pl.when(cc == 0)
    def _():
        o_ref[0] = out * lax.rsqrt(jnp.mean(out * out, axis=-1, keepdims=True) + NORM_EPS) * nw_ref[...]

    @pl.when(cc != 0)
    def _():
        o_ref[0] = out


def _cmp_stage2(a, b, pe8, w1flat, b1, w2, b2, nw, *, tr):
    R = a.shape[1]
    return pl.pallas_call(
        _cmp2_kernel,
        grid=(2, R // tr),
        in_specs=[
            pl.BlockSpec((1, tr, HD), lambda c, i: (c, i, 0)),
            pl.BlockSpec((1, tr, HD), lambda c, i: (c, i, 0)),
            pl.BlockSpec((1, 8, CMP_BLOCK * HD), lambda c, i: (c, 0, 0)),
            pl.BlockSpec((1, CMP_BLOCK * HD, HD), lambda c, i: (c, 0, 0)),
            pl.BlockSpec((1, 1, HD), lambda c, i: (c, 0, 0)),
            pl.BlockSpec((1, HD, HD), lambda c, i: (c, 0, 0)),
            pl.BlockSpec((1, 1, HD), lambda c, i: (c, 0, 0)),
            pl.BlockSpec((1, HD), lambda c, i: (0, 0)),
        ],
        out_specs=pl.BlockSpec((1, tr, HD), lambda c, i: (c, i, 0)),
        out_shape=jax.ShapeDtypeStruct((2, R, HD), F32),
        compiler_params=_cparams(("arbitrary", "arbitrary")),
        name="cmp_stage2",
    )(a, b, pe8, w1flat, b1.reshape(2, 1, HD), w2, b2.reshape(2, 1, HD), nw.reshape(1, HD))


def _stack_heads(qb):
    return jnp.concatenate([qb[:, g * HD:(g + 1) * HD] for g in range(NSA_G)], axis=0)


def _unstack_heads(o, tq):
    return jnp.concatenate([o[g * tq:(g + 1) * tq] for g in range(NSA_G)], axis=1)


def _gate_rows(gt):
    return jnp.concatenate([gt[:, g:g + 1] for g in range(NSA_G)], axis=0)


def _attn_cmp_kernel(q_ref, kc_ref, vc_ref, bias_ref, gate_ref, o_ref, sel_ref, *idx_ref, tq, q_pos0, nc, ns, nsp,
                     n_idx, sel_t):
    i = pl.program_id(2)
    ncp = kc_ref.shape[2]
    qs = _stack_heads(q_ref[...])
    logits = _dot_nt(qs, kc_ref[0, 0])
    logits = logits + jnp.concatenate([bias_ref[0, g] for g in range(NSA_G)], axis=0)
    rows = NSA_G * tq
    t4 = q_pos0 + i * tq + lax.broadcasted_iota(jnp.int32, (rows, ncp), 0) % tq
    cidx = lax.broadcasted_iota(jnp.int32, (rows, ncp), 1)
    mask = (cidx * CMP_STRIDE + (CMP_BLOCK - 1) <= t4) & (cidx < nc)
    lg = jnp.where(mask, logits, NEG)
    mx = jnp.max(lg, axis=-1, keepdims=True)
    ex = jnp.exp(lg - mx)
    p = ex / jnp.sum(ex, axis=-1, keepdims=True) * jnp.where(mask, 1.0, 0.0)
    oc = _dot(p, vc_ref[0, 0])
    o_ref[...] = _unstack_heads(oc * _gate_rows(gate_ref[0, 0, 0]), tq)

    psum = p[0:tq]
    for g in range(1, NSA_G):
        psum = psum + p[g * tq:(g + 1) * tq]
    top_k = min(SEL_TOPK, ns)
    if sel_t:
        nsr = sel_ref.shape[2]
        sr = lax.broadcasted_iota(jnp.int32, (nsr, ncp), 0)
        cc = lax.broadcasted_iota(jnp.int32, (nsr, ncp), 1)
        hit = (cc * CMP_STRIDE < sr * SEL_BLOCK + SEL_BLOCK) & (cc * CMP_STRIDE + CMP_BLOCK > sr * SEL_BLOCK)
        c2s_t = jnp.where(hit & (cc < nc) & (sr < ns), 1.0, 0.0)
        imp_t = jnp.dot(c2s_t, psum.T, preferred_element_type=F32, precision=lax.Precision.HIGHEST)
        tt = q_pos0 + i * tq + lax.broadcasted_iota(jnp.int32, (nsr, tq), 1)
        st = lax.broadcasted_iota(jnp.int32, (nsr, tq), 0)
        cur_t = tt // SEL_BLOCK
        forced_t = (st == 0) | (st == cur_t) | (st == cur_t - 1)
        score_t = jnp.where(forced_t, NSA_G + 1.0, jnp.where(st * SEL_BLOCK <= tt, imp_t, -1.0))
        score_t = jnp.where(st < ns, score_t, -2.0)
        rank_t = jnp.zeros((nsr, tq), F32)
        for sp in range(ns):
            other = score_t[sp:sp + 1, :]
            rank_t = rank_t + jnp.where(other > score_t, 1.0, jnp.where((other == score_t) & (sp < st), 1.0, 0.0))
        sel_ref[0, 0] = jnp.where((rank_t < top_k) & (st < ns), 1.0, 0.0)
        return
    cr = lax.broadcasted_iota(jnp.int32, (ncp, nsp), 0)
    sc = lax.broadcasted_iota(jnp.int32, (ncp, nsp), 1)
    c2s = (cr * CMP_STRIDE < sc * SEL_BLOCK + SEL_BLOCK) & (cr * CMP_STRIDE + CMP_BLOCK > sc * SEL_BLOCK)
    c2s = jnp.where(c2s & (cr < nc) & (sc < ns), 1.0, 0.0)
    imp = jnp.dot(psum, c2s, preferred_element_type=F32, precision=lax.Precision.HIGHEST)

    t = q_pos0 + i * tq + lax.broadcasted_iota(jnp.int32, (tq, nsp), 0)
    s = lax.broadcasted_iota(jnp.int32, (tq, nsp), 1)
    cur = t // SEL_BLOCK
    forced = (s == 0) | (s == cur) | (s == cur - 1)
    valid = s * SEL_BLOCK <= t
    score = jnp.where(forced, NSA_G + 1.0, jnp.where(valid, imp, -1.0))
    score = jnp.where(s < ns, score, -2.0)
    rank = jnp.zeros((tq, nsp), F32)
    for sp in range(ns):
        other = score[:, sp:sp + 1]
        ahead = (other > score) | ((other == score) & (sp < s))
        rank = rank + jnp.where(ahead, 1.0, 0.0)
    picked = (rank < top_k) & (s < ns)
    sel_ref[0, 0] = jnp.where(picked, 1.0, 0.0)
    if n_idx:
        listed = jnp.where(picked & (s < ns - 1), 1.0, 0.0)
        before = _dot(listed, jnp.where(lax.broadcasted_iota(jnp.int32, (nsp, nsp), 0)
                                        < lax.broadcasted_iota(jnp.int32, (nsp, nsp), 1), 1.0, 0.0))
        lane = lax.broadcasted_iota(jnp.int32, (tq, HD), 1)
        sf = s.astype(F32)
        out = jnp.zeros((tq, HD), F32)
        for kk in range(n_idx):
            hit = (listed > 0.5) & (before == float(kk))
            out = out + jnp.where(lane == kk, jnp.sum(jnp.where(hit, sf, 0.0), axis=-1, keepdims=True), 0.0)
        idx_ref[0][0, 0] = out.astype(jnp.int32)


def _attn_cmp(q, kcvc, bias_c, gate, *, batch, seq, tq, q_pos0, nc, ns, nsp, n_idx=0, sel_t=False):
    nqt = seq // tq
    ncp = kcvc.shape[3]
    kern = functools.partial(_attn_cmp_kernel, tq=tq, q_pos0=q_pos0, nc=nc, ns=ns, nsp=nsp, n_idx=n_idx,
                             sel_t=sel_t)
    rows_per_b = seq // tq
    nsr = -(-ns // 8) * 8
    if sel_t:
        sel_spec = pl.BlockSpec((1, 1, nsr, tq), lambda b, h, i: (b, h, 0, i))
        sel_shape = jax.ShapeDtypeStruct((batch, NSA_KV, nsr, seq), F32)
    else:
        sel_spec = pl.BlockSpec((1, 1, tq, nsp), lambda b, h, i: (b, h, i, 0))
        sel_shape = jax.ShapeDtypeStruct((batch, NSA_KV, seq, nsp), F32)
    extra_specs = [pl.BlockSpec((1, 1, tq, HD), lambda b, h, i: (b, h, i, 0))] if n_idx else []
    extra_shapes = [jax.ShapeDtypeStruct((batch, NSA_KV, seq, HD), jnp.int32)] if n_idx else []
    return pl.pallas_call(
        kern,
        grid=(batch, NSA_KV, nqt),
        in_specs=[
            pl.BlockSpec((tq, NSA_G * HD), lambda b, h, i: (b * rows_per_b + i, h)),
            pl.BlockSpec((None, 1, 1, ncp, HD), lambda b, h, i: (0, h, b, 0, 0)),
            pl.BlockSpec((None, 1, 1, ncp, HD), lambda b, h, i: (1, h, b, 0, 0)),
            pl.BlockSpec((1, NSA_G, tq, ncp), lambda b, h, i: (h, 0, i, 0)),
            pl.BlockSpec((1, 1, 1, tq, NSA_G), lambda b, h, i: (0, b, h, i, 0)),
        ],
        out_specs=[
            pl.BlockSpec((tq, NSA_G * HD), lambda b, h, i: (b * rows_per_b + i, h)),
            sel_spec,
        ] + extra_specs,
        out_shape=[jax.ShapeDtypeStruct((batch * seq, NSA_HEADS * HD), F32), sel_shape] + extra_shapes,
        compiler_params=_cparams(("parallel", "parallel", "arbitrary")),
        name="attn_cmp",
    )(q, kcvc, kcvc, bias_c, gate)


def _flash_kernel(pt_ref, *refs, cfg):
    pps, tq, has_tail, use_sel = cfg["pps"], cfg["tq"], cfg["has_tail"], cfg["use_sel"]
    it = iter(refs)
    q_ref = next(it)
    k_refs = [next(it) for _ in range(pps)]
    v_refs = [next(it) for _ in range(pps)]
    b_refs = [next(it) for _ in range(pps)]
    if has_tail:
        kt_ref, vt_ref, bt_ref = next(it), next(it), next(it)
    sel_ref = next(it) if use_sel else None
    gate_ref = next(it)
    prev_ref = next(it)
    o_ref = next(it)
    m_ref, l_ref, acc_ref = next(it), next(it), next(it)

    i = pl.program_id(2)
    st = pl.program_id(3)
    rows = NSA_G * tq

    @pl.when(st == 0)
    def _():
        m_ref[...] = jnp.full((rows, 1), NEG, F32)
        l_ref[...] = jnp.zeros((rows, 1), F32)
        acc_ref[...] = jnp.zeros((rows, HD), F32)

    t0 = cfg["q_pos0"] + i * tq

    def tile(k, v, bias4, kt, p0):
        qs = _stack_heads(q_ref[...])
        s = _dot_nt(qs, k) + jnp.concatenate([bias4[g] for g in range(NSA_G)], axis=0)
        tt = t0 + lax.broadcasted_iota(jnp.int32, (tq, TK), 0)
        pos = p0 + lax.broadcasted_iota(jnp.int32, (tq, TK), 1)
        dist = tt - pos
        ok = dist >= 0
        if use_sel:
            nsp = sel_ref.shape[3]
            sr = lax.broadcasted_iota(jnp.int32, (nsp, TK), 0)
            sc = lax.broadcasted_iota(jnp.int32, (nsp, TK), 1)
            expand = jnp.where(sr == kt * (TK // SEL_BLOCK) + sc // SEL_BLOCK, 1.0, 0.0)
            picked = _dot(sel_ref[0, 0], expand)
            ok = ok & (picked > 0.5)
        else:
            ok = ok & (dist < WINDOW) & (pos >= cfg["w_pos0"])
        okf = jnp.where(ok, 1.0, 0.0)
        ok4 = jnp.concatenate([okf] * NSA_G, axis=0)
        s = jnp.where(ok4 > 0.5, s, NEG)
        m_old = m_ref[...]
        m_new = jnp.maximum(m_old, jnp.max(s, axis=-1, keepdims=True))
        alpha = jnp.exp(m_old - m_new)
        p = jnp.exp(s - m_new) * ok4
        l_ref[...] = alpha * l_ref[...] + jnp.sum(p, axis=-1, keepdims=True)
        acc_ref[...] = alpha * acc_ref[...] + _dot(p, v)
        m_ref[...] = m_new

    for pp in range(pps):
        kt = cfg["tile_of"](i, st, pp)
        active = cfg["active"](i, st, pp)
        p0 = cfg["kbase"] + kt * TK
        if active is True:
            tile(k_refs[pp][0], v_refs[pp][0], b_refs[pp][0, 0], kt, p0)
        else:
            @pl.when(active)
            def _(pp=pp, kt=kt, p0=p0):
                tile(k_refs[pp][0], v_refs[pp][0], b_refs[pp][0, 0], kt, p0)

    if has_tail:
        @pl.when(st == pl.num_programs(3) - 1)
        def _():
            tile(kt_ref[0], vt_ref[0], bt_ref[0, 0], cfg["tail_tile"], cfg["kbase"] + cfg["tail_tile"] * TK)

    @pl.when(st == pl.num_programs(3) - 1)
    def _():
        o = acc_ref[...] / l_ref[...] * _gate_rows(gate_ref[0, 0, 0])
        o_ref[...] = (prev_ref[...] + _unstack_heads(o, tq)).astype(o_ref.dtype)


def _flash(q, pages, ptab, kcol, vcol, bias_tiles, tails, sel, gate, branch, prev, *, batch, seq, tq, q_pos0,
           pps, nsteps, tile_of, active, kbase, w_pos0, npt, tail_tile, out_dtype):
    nqt = seq // tq
    has_tail = tails is not None
    use_sel = sel is not None
    n_delta = bias_tiles.shape[1]
    cfg = dict(pps=pps, tq=tq, has_tail=has_tail, use_sel=use_sel, q_pos0=q_pos0, tile_of=tile_of,
               active=active, kbase=kbase, w_pos0=w_pos0, tail_tile=tail_tile)

    def page_idx(b, i, s, pp, pt):
        kt = jnp.clip(tile_of(i, s, pp), 0, npt - 1)
        return pt[b * npt + kt]

    def didx(i, s, pp):
        kt = tile_of(i, s, pp)
        return jnp.clip((q_pos0 + i * tq - kbase - kt * TK) // TK, 0, n_delta - 1)

    in_specs = [pl.BlockSpec((tq, NSA_G * HD), lambda b, h, i, s, pt: (b * nqt + i, h))]
    args = [q]
    for col in (kcol, vcol):
        for pp in range(pps):
            in_specs.append(pl.BlockSpec(
                (1, TK, HD), lambda b, h, i, s, pt, pp=pp, col=col: (page_idx(b, i, s, pp, pt), 0, col + h)))
            args.append(pages)
    for pp in range(pps):
        in_specs.append(pl.BlockSpec(
            (1, 1, NSA_G, tq, TK), lambda b, h, i, s, pt, pp=pp: (h, didx(i, s, pp), 0, 0, 0)))
        args.append(bias_tiles)
    if has_tail:
        tail_pages, tkcol, tvcol = tails
        tdelta = min(max((q_pos0 - kbase - tail_tile * TK) // TK, 0), n_delta - 1)
        in_specs.append(pl.BlockSpec((1, TK, HD), lambda b, h, i, s, pt: (b, 0, tkcol + h)))
        in_specs.append(pl.BlockSpec((1, TK, HD), lambda b, h, i, s, pt: (b, 0, tvcol + h)))
        in_specs.append(pl.BlockSpec((1, 1, NSA_G, tq, TK), lambda b, h, i, s, pt: (h, tdelta, 0, 0, 0)))
        args += [tail_pages, tail_pages, bias_tiles]
    if use_sel:
        nsp = sel.shape[3]
        in_specs.append(pl.BlockSpec((1, 1, tq, nsp), lambda b, h, i, s, pt: (b, h, i, 0)))
        args.append(sel)
    in_specs.append(pl.BlockSpec((1, 1, 1, tq, NSA_G), lambda b, h, i, s, pt: (branch, b, h, i, 0)))
    args.append(gate)
    in_specs.append(pl.BlockSpec((tq, NSA_G * HD), lambda b, h, i, s, pt: (b * nqt + i, h)))
    args.append(prev)

    rows = NSA_G * tq
    grid_spec = pltpu.PrefetchScalarGridSpec(
        num_scalar_prefetch=1,
        grid=(batch, NSA_KV, nqt, nsteps),
        in_specs=in_specs,
        out_specs=pl.BlockSpec((tq, NSA_G * HD), lambda b, h, i, s, pt: (b * nqt + i, h)),
        scratch_shapes=[pltpu.VMEM((rows, 1), F32), pltpu.VMEM((rows, 1), F32), pltpu.VMEM((rows, HD), F32)],
    )
    return pl.pallas_call(
        functools.partial(_flash_kernel, cfg=cfg),
        grid_spec=grid_spec,
        out_shape=jax.ShapeDtypeStruct((batch * seq, NSA_HEADS * HD), out_dtype),
        compiler_params=_cparams(("parallel", "parallel", "arbitrary", "arbitrary")),
        name="flash_sel" if use_sel else "flash_win",
    )(ptab, *args)


def _flasht_kernel(pt_ref, qi_ref, si_ref, lf_ref, kt_ref, ktc_ref, dd_ref, *refs, cfg):
    pps, tq, use_sel = cfg["pps"], cfg["tq"], cfg["use_sel"]
    it = iter(refs)
    q_ref = next(it)
    k_refs = [next(it) for _ in range(pps)]
    v_refs = [next(it) for _ in range(pps)]
    b_refs = [next(it) for _ in range(pps)]
    sel_ref = next(it) if use_sel else None
    gate_ref = next(it)
    prev_ref = next(it)
    o_ref = next(it)
    qt_ref, m_ref, l_ref, acc_ref = next(it), next(it), next(it), next(it)

    n = pl.program_id(2)
    i = qi_ref[n]
    st = si_ref[n]
    cols = NSA_G * tq

    @pl.when(st == 0)
    def _():
        qb = q_ref[...].astype(F32)
        qt_ref[...] = jnp.concatenate([qb[:, g * HD:(g + 1) * HD].T for g in range(NSA_G)], axis=1).astype(BF16)
        m_ref[...] = jnp.full((1, cols), NEG, F32)
        l_ref[...] = jnp.zeros((1, cols), F32)
        acc_ref[...] = jnp.zeros((HD, cols), F32)

    t0 = cfg["q_pos0"] + i * tq
    qt = qt_ref[...]
    tt = t0 + lax.broadcasted_iota(jnp.int32, (TK, tq), 1)
    krow = lax.broadcasted_iota(jnp.int32, (TK, tq), 0)
    scores, oks = [], []
    for pp in range(pps):
        kt = kt_ref[n * pps + pp]
        pos = cfg["kbase"] + kt * TK + krow
        dist = tt - pos
        ok = dist >= 0
        if use_sel:
            nsr = sel_ref.shape[2]
            kr = lax.broadcasted_iota(jnp.int32, (TK, nsr), 0)
            sc = lax.broadcasted_iota(jnp.int32, (TK, nsr), 1)
            expand = jnp.where(sc == kt * (TK // SEL_BLOCK) + kr // SEL_BLOCK, 1.0, 0.0)
            ok = ok & (_dot(expand, sel_ref[0, 0]) > 0.5)
        else:
            ok = ok & (dist < WINDOW) & (pos >= cfg["w_pos0"])
        ok4 = jnp.concatenate([ok] * NSA_G, axis=1)
        bias = jnp.concatenate([b_refs[pp][0, g, 0] for g in range(NSA_G)], axis=1).astype(F32)
        s = jnp.dot(k_refs[pp][0].astype(BF16), qt, preferred_element_type=F32) + bias
        scores.append(jnp.where(ok4, s, NEG))
        oks.append(ok4)
    m_old = m_ref[...]
    m_new = functools.reduce(jnp.maximum, [jnp.max(s, axis=0, keepdims=True) for s in scores] + [m_old])
    alpha = jnp.exp(m_old - m_new)
    ps = [jnp.where(ok4, jnp.exp(s - m_new), 0.0) for s, ok4 in zip(scores, oks)]
    l_new = alpha * l_ref[...]
    acc = alpha * acc_ref[...]
    for pp in range(pps):
        l_new = l_new + jnp.sum(ps[pp], axis=0, keepdims=True)
        acc = acc + _dot_tn(v_refs[pp][0], ps[pp])
    l_ref[...] = l_new
    acc_ref[...] = acc
    m_ref[...] = m_new

    @pl.when(lf_ref[n] == 1)
    def _():
        gt = gate_ref[0, 0, 0]
        grow = jnp.concatenate([gt[g:g + 1, :] for g in range(NSA_G)], axis=1)
        ot = acc / l_new * grow
        o = jnp.concatenate([ot[:, g * tq:(g + 1) * tq].T for g in range(NSA_G)], axis=1)
        o_ref[...] = (prev_ref[...] + o).astype(o_ref.dtype)


def _flasht(q, pages, ptab, kcol, vcol, bias_t, sel, gate_t, branch, prev, *, batch, seq, tq, q_pos0, pps,
            steps_of, tile_of, kbase, w_pos0, npt, out_dtype):
    nqt = seq // tq
    use_sel = sel is not None
    n_delta = bias_t.shape[2]
    cols = NSA_G * tq
    pairs = [(i, s) for i in range(nqt) for s in range(steps_of(i))]
    qi = jnp.asarray([p[0] for p in pairs], jnp.int32)
    si = jnp.asarray([p[1] for p in pairs], jnp.int32)
    lf = jnp.asarray([int(s == steps_of(i) - 1) for (i, s) in pairs], jnp.int32)
    cfg = dict(pps=pps, tq=tq, use_sel=use_sel, q_pos0=q_pos0, kbase=kbase, w_pos0=w_pos0)
    kt_raw = [tile_of(i, s, pp) for (i, s) in pairs for pp in range(pps)]
    kt = jnp.asarray(kt_raw, jnp.int32)
    ktc = jnp.asarray([min(max(k, 0), npt - 1) for k in kt_raw], jnp.int32)
    d_neg = tq // TK - 1
    dd = jnp.asarray([min(max((q_pos0 + i * tq - kbase - tile_of(i, s, pp) * TK) // TK + d_neg, 0), n_delta - 1)
                      for (i, s) in pairs for pp in range(pps)], jnp.int32)

    qmap = lambda b, h, n, pt, qi_, *_: (b * nqt + qi_[n], h)
    in_specs = [pl.BlockSpec((tq, NSA_G * HD), qmap)]
    args = [q]
    for col in (kcol, vcol):
        for pp in range(pps):
            in_specs.append(pl.BlockSpec(
                (1, TK, HD),
                lambda b, h, n, pt, qi_, si_, lf_, kt_, ktc_, dd_, pp=pp, col=col:
                (pt[b * npt + ktc_[n * pps + pp]], 0, col + h)))
            args.append(pages)
    for pp in range(pps):
        in_specs.append(pl.BlockSpec(
            (1, NSA_G, 1, TK, tq),
            lambda b, h, n, pt, qi_, si_, lf_, kt_, ktc_, dd_, pp=pp: (h, 0, dd_[n * pps + pp], 0, 0)))
        args.append(bias_t)
    if use_sel:
        nsr = sel.shape[2]
        in_specs.append(pl.BlockSpec((1, 1, nsr, tq), lambda b, h, n, pt, qi_, *_: (b, h, 0, qi_[n])))
        args.append(sel)
    in_specs.append(pl.BlockSpec((1, 1, 1, NSA_G, tq), lambda b, h, n, pt, qi_, *_: (branch, b, h, 0, qi_[n])))
    args.append(gate_t)
    in_specs.append(pl.BlockSpec((tq, NSA_G * HD), qmap))
    args.append(prev)

    grid_spec = pltpu.PrefetchScalarGridSpec(
        num_scalar_prefetch=7,
        grid=(batch, NSA_KV, len(pairs)),
        in_specs=in_specs,
        out_specs=pl.BlockSpec((tq, NSA_G * HD), qmap),
        scratch_shapes=[pltpu.VMEM((HD, cols), BF16), pltpu.VMEM((1, cols), F32), pltpu.VMEM((1, cols), F32),
                        pltpu.VMEM((HD, cols), F32)],
    )
    return pl.pallas_call(
        functools.partial(_flasht_kernel, cfg=cfg),
        grid_spec=grid_spec,
        out_shape=jax.ShapeDtypeStruct((batch * seq, NSA_HEADS * HD), out_dtype),
        compiler_params=_cparams(("parallel", "parallel", "arbitrary")),
        name="flasht_sel" if use_sel else "flasht_win",
    )(ptab, qi, si, lf, kt, ktc, dd, *args)


def _selg_kernel(idx_ref, pt_ref, q_ref, *refs, nblk, tq, q_pos0, tail_pos0):
    kv_refs = refs[:nblk]
    tk_ref, tv_ref, bt_ref, gate_ref, prev_ref, o_ref, osc_ref = refs[nblk:]
    ns = bt_ref.shape[2]
    b = pl.program_id(0)
    h = pl.program_id(1)
    qi = pl.program_id(2)
    nq = pl.num_programs(2)
    rows = NSA_G * tq
    nch = 2 * NSA_KV

    @pl.when(qi == 0)
    def _():
        osc_ref[...] = jnp.zeros((rows, HD), F32)

    qs = _stack_heads(q_ref[...])
    t = q_pos0 + qi
    base = ((b * NSA_KV + h) * nq + qi) * nblk
    jj = lax.broadcasted_iota(jnp.int32, (rows, SEL_BLOCK), 1)

    def bias_rows(blk):
        bb = bt_ref[0, 0, blk]
        return jnp.concatenate([jnp.broadcast_to(bb[g:g + 1, :], (tq, SEL_BLOCK)) for g in range(NSA_G)], axis=0)

    scores, vals = [], []
    for n in range(nblk):
        k = kv_refs[n][pl.ds(h, SEL_BLOCK, stride=nch), :]
        v = kv_refs[n][pl.ds(NSA_KV + h, SEL_BLOCK, stride=nch), :]
        blk = idx_ref[base + n]
        s = _dot_nt(qs, k) + bias_rows(blk)
        pos = blk * SEL_BLOCK + jj
        scores.append(jnp.where(pos <= t, s, NEG))
        vals.append(v)
    s = _dot_nt(qs, tk_ref[0]) + bias_rows(ns - 1)
    scores.append(jnp.where(tail_pos0 + jj <= t, s, NEG))
    vals.append(tv_ref[0])

    m = functools.reduce(jnp.maximum, [jnp.max(s, axis=-1, keepdims=True) for s in scores])
    l = jnp.zeros((rows, 1), F32)
    acc = jnp.zeros((rows, HD), F32)
    for s, v in zip(scores, vals):
        p = jnp.exp(s - m)
        l = l + jnp.sum(p, axis=-1, keepdims=True)
        acc = acc + _dot(p, v)
    rowq = lax.broadcasted_iota(jnp.int32, (rows, 1), 0) % tq
    osc = jnp.where(rowq == qi, acc / l, osc_ref[...])
    osc_ref[...] = osc

    @pl.when(qi == nq - 1)
    def _():
        o_ref[...] = prev_ref[...] + _unstack_heads(osc * _gate_rows(gate_ref[0, 0, 0]), tq)


def _sel_gather(q, cache_rows, ptab, idx, bias_blk, tail, gate, prev, *, batch, tq, nq, q_pos0, npt, ns):
    nblk = idx.shape[0] // (batch * NSA_KV * nq)
    half = SEL_BLOCK * 2 * NSA_KV
    per_page = PAGE // SEL_BLOCK

    def blk(b, h, qi, n, idx_ref):
        return idx_ref[((b * NSA_KV + h) * nq + qi) * nblk + n]

    def kv_map(n):
        def f(b, h, qi, idx_ref, pt_ref):
            s = blk(b, h, qi, n, idx_ref)
            return (pt_ref[b * npt + s // per_page] * per_page + s % per_page, 0)
        return f

    in_specs = [pl.BlockSpec((tq, NSA_G * HD), lambda b, h, qi, i_, p_: (b, h))]
    in_specs += [pl.BlockSpec((half, HD), kv_map(n)) for n in range(nblk)]
    in_specs += [
        pl.BlockSpec((1, SEL_BLOCK, HD), lambda b, h, qi, i_, p_: (b, 0, h)),
        pl.BlockSpec((1, SEL_BLOCK, HD), lambda b, h, qi, i_, p_: (b, 0, NSA_KV + h)),
        pl.BlockSpec((1, 1, ns, NSA_G, SEL_BLOCK), lambda b, h, qi, i_, p_: (qi, h, 0, 0, 0)),
        pl.BlockSpec((1, 1, 1, tq, NSA_G), lambda b, h, qi, i_, p_: (1, b, h, 0, 0)),
        pl.BlockSpec((tq, NSA_G * HD), lambda b, h, qi, i_, p_: (b, h)),
    ]
    grid_spec = pltpu.PrefetchScalarGridSpec(
        num_scalar_prefetch=2,
        grid=(batch, NSA_KV, nq),
        in_specs=in_specs,
        out_specs=pl.BlockSpec((tq, NSA_G * HD), lambda b, h, qi, i_, p_: (b, h)),
        scratch_shapes=[pltpu.VMEM((NSA_G * tq, HD), F32)],
    )
    kern = functools.partial(_selg_kernel, nblk=nblk, tq=tq, q_pos0=q_pos0, tail_pos0=(ns - 1) * SEL_BLOCK)
    return pl.pallas_call(
        kern,
        grid_spec=grid_spec,
        out_shape=jax.ShapeDtypeStruct((batch * tq, NSA_HEADS * HD), F32),
        compiler_params=_cparams(("parallel", "parallel", "arbitrary")),
        name="sel_gather",
    )(idx, ptab, q, *([cache_rows] * nblk), tail, tail, bias_blk, gate, prev)


def _rel_bucket(dist):
    n = jnp.maximum(dist, 0)
    max_exact = REL_BUCKETS // 2
    nf = jnp.maximum(n, 1).astype(F32)
    large = max_exact + (jnp.log(nf / max_exact) / math.log(REL_MAX_DIST / max_exact)
                         * (REL_BUCKETS - max_exact)).astype(jnp.int32)
    return jnp.where(n < max_exact, n, jnp.minimum(large, REL_BUCKETS - 1))


def _bias_by_distance(rel_bias):
    return rel_bias.astype(F32)[_rel_bucket(jnp.arange(REL_MAX_DIST))]


def _toeplitz(r, nrows, ncols):
    p = r.shape[-1]
    flat = jnp.tile(r, (1,) * (r.ndim - 1) + (nrows,))[..., :nrows * (p - 1)]
    return flat.reshape(r.shape[:-1] + (nrows, p - 1))[..., :ncols]


def _bias_tiles(rel_bias, tq, n_delta, transposed=False, d_neg=0):
    fd = _bias_by_distance(rel_bias).T
    lo = TK * (1 + d_neg)
    span = TK * n_delta + tq
    hi = span - REL_MAX_DIST
    padded = jnp.concatenate([jnp.tile(fd[:, :1], (1, lo)), fd[:, :span]]
                             + ([jnp.tile(fd[:, -1:], (1, hi))] if hi > 0 else []), axis=1)
    z = jnp.stack([padded[:, TK * d + 1:TK * d + TK + tq] for d in range(n_delta + d_neg)], axis=1)
    if transposed:
        r = jnp.roll(z, -(TK - 1), axis=2).astype(BF16)
        return _toeplitz(r, TK, tq).reshape(NSA_KV, NSA_G, n_delta + d_neg, TK, tq)
    r = jnp.roll(z[:, :, ::-1], -(tq - 1), axis=2)
    t = _toeplitz(r, tq, TK)
    return t.reshape(NSA_KV, NSA_G, n_delta, tq, TK).transpose(0, 2, 1, 3, 4)


def _bias_cmp(rel_bias, q_pos0, tqs, ncp):
    fd = _bias_by_distance(rel_bias)
    last = CMP_BLOCK - 1
    if tqs <= SEL_BLOCK:
        dist = (q_pos0 + jnp.arange(tqs))[:, None] - (jnp.arange(ncp) * CMP_STRIDE + last)[None, :]
        b = fd[jnp.clip(dist, 0, REL_MAX_DIST - 1)]
        return b.reshape(tqs, ncp, NSA_KV, NSA_G).transpose(2, 3, 0, 1)
    assert q_pos0 == 0 and tqs % CMP_STRIDE == 0
    ntau = tqs // CMP_STRIDE
    period = ntau + ncp
    kappa = period - jnp.arange(period)
    dist = CMP_STRIDE * kappa[None, :] + jnp.arange(CMP_STRIDE)[:, None] - last
    dist = jnp.where(kappa[None, :] < ntau, dist, 0)
    r = fd[jnp.clip(dist, 0, REL_MAX_DIST - 1)]
    t = _toeplitz(r.transpose(2, 0, 1), ntau, ncp)
    return t.transpose(0, 2, 1, 3).reshape(NSA_KV, NSA_G, tqs, ncp)


def _bias_blocks(rel_bias, q_pos0, nq, ns):
    fd = _bias_by_distance(rel_bias)
    n = ns * SEL_BLOCK
    rows = []
    for qq in range(nq):
        t = q_pos0 + qq
        far = max(min(t - (REL_MAX_DIST - 1), n), 0)
        mid_hi = min(t + 1, n)
        parts = [jnp.tile(fd[-1:], (far, 1))] if far else []
        if mid_hi > far:
            parts.append(fd[t - mid_hi + 1:t - far + 1][::-1])
        if n > mid_hi:
            parts.append(jnp.tile(fd[:1], (n - mid_hi, 1)))
        rows.append(jnp.concatenate(parts, axis=0))
    t = jnp.stack(rows)
    return t.reshape(nq, ns, SEL_BLOCK, NSA_KV, NSA_G).transpose(0, 3, 1, 4, 2)


def _forward_group(x, conv0, ssm0, past, P, *, batch, seq, t_valid, q_pos0, tq):
    M = batch * seq
    conv_out, ssm_out = [], []
    for l in range(2):
        proj = _nmm(x, P["mix_norm"][l], P["gdn_w_main"][l], tn=1024, tm=1024)
        bg = _nmm(x, P["mix_norm"][l], P["gdn_w_gate"][l], tn=128, mode="gdn_gate", aux=P["gdn_gate_aux"][l],
                  seq=seq, t_valid=t_valid)
        o, s_new = _gdn(proj, bg, P["gdn_conv_w"][l], conv0[l], ssm0[l], P["gdn_out_norm"][l],
                        batch=batch, seq=seq)
        conv_out.append(proj.reshape(batch, seq, -1)[:, t_valid - (GDN_CONV - 1):t_valid, :GDN_CONV_DIM])
        ssm_out.append(s_new)
        x = _mm_res(o, P["gdn_w_out"][l], x, tn=1024)
        x = _mlp(x, P["mlp_norm"][l], P["mlp_w1"][l], P["mlp_w2"][l])
    x, cmp_rows, sel_rows, win_state = _nsa_layers(x, past, P, batch=batch, seq=seq, t_valid=t_valid,
                                                   q_pos0=q_pos0, tq=tq)
    return x, jnp.stack(conv_out), jnp.stack(ssm_out), cmp_rows, sel_rows, win_state


def _nsa_layers(x, past, P, *, batch, seq, t_valid, q_pos0, tq):
    M = batch * seq
    kv = _nmm(x, P["kv_norm"], P["nsa_w_kv"], tn=512, tm=1024, mode="headnorm", aux=P["kv_aux"],
              norm_tiles=(2, 4), n_split=3)
    kv4 = kv.reshape(3, batch, seq, 2 * NSA_KV * HD)
    new_cmp, new_sel, win_new = (kv4[br][:, :t_valid] for br in range(3))
    cmp_rows = new_cmp.reshape(batch, t_valid, 2, NSA_KV, HD)
    sel_rows = new_sel.reshape(batch, t_valid, 2, NSA_KV, HD)

    ident = jnp.arange(M // PAGE, dtype=jnp.int32) if seq % PAGE == 0 else None
    if past is None:
        n_tot = t_valid
        npages = seq // PAGE
        kv_pages = kv.reshape(3 * M // PAGE, PAGE, 2 * NSA_KV * HD)
        first = _cmp_stage1(kv_pages, ident, P["cmp_w1cat"], row_packed=False)
        f6 = first.reshape(2, NSA_KV, batch, npages * (PAGE // CMP_STRIDE), 2 * HD)
        win_seq = win_new
        w_pos0 = 0
    else:
        n_past = past["page_table"].shape[1] * PAGE
        n_tot = n_past + t_valid
        npages = n_past // PAGE
        ptab = past["page_table"].reshape(-1)
        first = _cmp_stage1(past["cmp_rows"], ptab, P["cmp_w1cat"], row_packed=True)
        f6 = first.reshape(2, NSA_KV, batch, npages * (PAGE // CMP_STRIDE), 2 * HD)
        tail_cmp = jnp.pad(new_cmp, ((0, 0), (0, PAGE - t_valid), (0, 0)))
        tail_cmp = jnp.pad(tail_cmp, ((0, (-batch) % CMP_PPS), (0, 0), (0, 0)))
        tfirst = _cmp_stage1(tail_cmp, jnp.arange(tail_cmp.shape[0], dtype=jnp.int32), P["cmp_w1cat"],
                             row_packed=False)
        t6 = tfirst.reshape(2, NSA_KV, -1, PAGE // CMP_STRIDE, 2 * HD)[:, :, :batch, :(-(-t_valid // CMP_STRIDE))]
        f6 = jnp.concatenate([f6, t6], axis=3)
        win_seq = jnp.concatenate([past["win"], win_new], axis=1)
        w_pos0 = q_pos0 + t_valid - win_seq.shape[1]
    nc = -(-n_tot // CMP_STRIDE) - 1
    ns = -(-n_tot // SEL_BLOCK)
    ncp = -(-nc // 128) * 128
    nsp = -(-ns // 128) * 128
    a = f6[:, :, :, 0:nc, :HD]
    b = f6[:, :, :, 1:nc + 1, HD:]
    if b.shape[3] < nc:
        b = jnp.pad(b, ((0, 0), (0, 0), (0, 0), (0, nc - b.shape[3]), (0, 0)))
    a = jnp.pad(a, ((0, 0), (0, 0), (0, 0), (0, ncp - nc), (0, 0))).reshape(2, batch * NSA_KV * ncp, HD)
    b = jnp.pad(b, ((0, 0), (0, 0), (0, 0), (0, ncp - nc), (0, 0))).reshape(2, batch * NSA_KV * ncp, HD)
    R = batch * NSA_KV * ncp
    kcvc = _cmp_stage2(a, b, P["cmp_pe8"], P["cmp_w1flat"], P["cmp_b1"], P["cmp_w2"], P["cmp_b2"],
                       P["k_cmp_norm"], tr=min(R, 2048))
    kcvc = kcvc.reshape(2, NSA_KV, batch, ncp, HD)

    n_keep = min(WINDOW, win_seq.shape[1])
    win_state = win_seq[:, win_seq.shape[1] - n_keep:].reshape(batch, n_keep, 2, NSA_KV, HD)

    seq_q = seq if past is None else tq
    bias_c = _bias_cmp(P["rel_bias"], q_pos0, seq_q, ncp)
    n_delta = min(N_DELTA, (q_pos0 + seq_q) // TK + 1)
    if past is None:
        tqf = FLASH_TQ
        btiles_t = _bias_tiles(P["rel_bias"], tqf, n_delta, transposed=True, d_neg=tqf // TK - 1)
        sel_pages, sel_ptab, sel_npt = kv_pages, ident + M // PAGE, seq // PAGE
        sel_kcol, sel_vcol = 0, NSA_KV
        sel_pps = 4
        sel_tile_of = lambda i, s, pp: s * sel_pps + pp
        win_pages, win_ptab, win_npt = kv_pages, ident + 2 * (M // PAGE), seq // PAGE
        win_kcol, win_vcol = 0, NSA_KV
        win_pps = WINDOW // TK + tqf // TK
        win_tile_of = lambda i, s, pp: (i * tqf) // TK - WINDOW // TK + pp
        win_kbase = 0
    else:
        btiles = _bias_tiles(P["rel_bias"], tq, n_delta)
        assert n_past % SEL_BLOCK == 0 and t_valid <= SEL_BLOCK and ns - 1 > SEL_TOPK
        tail_sel = jnp.pad(new_sel, ((0, 0), (0, SEL_BLOCK - t_valid), (0, 0)))
        bias_blk = _bias_blocks(P["rel_bias"], q_pos0, t_valid, ns)
        nwt = -(-win_seq.shape[1] // TK)
        win_pages = jnp.pad(win_seq, ((0, 0), (0, nwt * TK - win_seq.shape[1]), (0, 0)))
        win_pages = win_pages.reshape(batch * nwt, TK, 2 * NSA_KV * HD)
        win_ptab, win_npt = jnp.arange(batch * nwt, dtype=jnp.int32), nwt
        win_kcol, win_vcol = 0, 4
        win_pps = nwt
        win_tile_of = lambda i, s, pp: pp
        win_active = lambda i, s, pp: True
        win_kbase = w_pos0

    for jj in range(2):
        l = 2 + jj
        q = _nmm(x, P["mix_norm"][l], P["nsa_w_q"][jj], tn=512, tm=1024, out_dtype=BF16, mode="headnorm",
                 aux=P["nsa_q_aux"][jj], scale=HD ** -0.5)
        gates = _nmm(x, P["mix_norm"][l], P["nsa_w_g"][jj], tn=128, mode="sigmoid")
        gate = gates[:, :NSA_HEADS * 3].reshape(batch, seq, NSA_KV, NSA_G, 3).transpose(4, 0, 2, 1, 3)
        if seq_q != seq:
            q = q.reshape(batch, seq, -1)[:, :seq_q].reshape(batch * seq_q, -1)
            gate = gate[:, :, :, :seq_q]
        if past is None:
            gate_t = gates[:, :NSA_HEADS * 3].reshape(batch, seq, NSA_KV, NSA_G, 3).transpose(4, 0, 2, 3, 1)
            o_c, sel = _attn_cmp(q, kcvc, bias_c, gate, batch=batch, seq=seq_q, tq=tq, q_pos0=q_pos0, nc=nc,
                                 ns=ns, nsp=nsp, sel_t=True)
            o_s = _flasht(q, sel_pages, sel_ptab, sel_kcol, sel_vcol, btiles_t, sel, gate_t, 1, o_c,
                          batch=batch, seq=seq_q, tq=tqf, q_pos0=q_pos0, pps=sel_pps,
                          steps_of=lambda i: (i * tqf + tqf - 1) // (TK * sel_pps) + 1,
                          tile_of=sel_tile_of, kbase=0, w_pos0=0, npt=sel_npt, out_dtype=F32)
            o_w = _flasht(q, win_pages, win_ptab, win_kcol, win_vcol, btiles_t, None, gate_t, 2, o_s,
                          batch=batch, seq=seq_q, tq=tqf, q_pos0=q_pos0, pps=win_pps, steps_of=lambda i: 1,
                          tile_of=win_tile_of, kbase=win_kbase, w_pos0=w_pos0, npt=win_npt, out_dtype=BF16)
        else:
            o_c, sel, idx = _attn_cmp(q, kcvc, bias_c, gate, batch=batch, seq=seq_q, tq=tq, q_pos0=q_pos0,
                                      nc=nc, ns=ns, nsp=nsp, n_idx=SEL_TOPK - 1)
            o_s = _sel_gather(q, past["sel_rows"], past["page_table"].reshape(-1),
                              idx[:, :, :t_valid, :SEL_TOPK - 1].reshape(-1), bias_blk, tail_sel, gate, o_c,
                              batch=batch, tq=tq, nq=t_valid, q_pos0=q_pos0, npt=npages, ns=ns)
            o_w = _flash(q, win_pages, win_ptab, win_kcol, win_vcol, btiles, None, None, gate, 2, o_s,
                         batch=batch, seq=seq_q, tq=tq, q_pos0=q_pos0, pps=win_pps,
                         nsteps=1, tile_of=win_tile_of, active=win_active, kbase=win_kbase, w_pos0=w_pos0,
                         npt=win_npt, tail_tile=0, out_dtype=F32)
        if seq_q != seq:
            o_w = jnp.pad(o_w.reshape(batch, seq_q, -1), ((0, 0), (0, seq - seq_q), (0, 0))).reshape(M, -1)
        x = _mm_res(o_w.astype(BF16), P["nsa_w_out"][jj], x, tm=1024, tn=1024)
        x = _mlp(x, P["mlp_norm"][l], P["mlp_w1"][l], P["mlp_w2"][l])
    return x, cmp_rows, sel_rows, win_state


def _prepare_params(mix_norm, mlp_norm, mlp_w1, mlp_w2, gdn_w_in, gdn_conv_w, gdn_a_log, gdn_dt_bias,
                    gdn_out_norm, gdn_w_out, kv_norm, nsa_w_kv, k_sel_norm, k_win_norm, k_cmp_norm, cmp_pe,
                    cmp_w1, cmp_b1, cmp_w2, cmp_b2, nsa_w_in, nsa_q_norm, nsa_w_out, rel_bias):
    n_lay = gdn_w_in.shape[0]
    main = GDN_CONV_DIM + GDN_VAL_DIM
    zpad = lambda n: jnp.zeros((1, n), F32)
    gate_aux = jnp.stack([
        jnp.concatenate([
            jnp.concatenate([zpad(GDN_V_HEADS), gdn_a_log[l][None].astype(F32), zpad(HD - 2 * GDN_V_HEADS)], 1),
            jnp.concatenate([zpad(GDN_V_HEADS), gdn_dt_bias[l][None].astype(F32), zpad(HD - 2 * GDN_V_HEADS)], 1),
        ], 0)[None] for l in range(n_lay)])
    tile4 = lambda w: jnp.tile(w.astype(F32), NSA_KV)[None, None]
    kv_aux = jnp.concatenate([jnp.ones((2, 1, 512), F32), tile4(k_sel_norm), jnp.ones((1, 1, 512), F32),
                              tile4(k_win_norm), jnp.ones((1, 1, 512), F32)], 0)
    nq = NSA_HEADS * HD
    w1r = cmp_w1.reshape(2, 2, CMP_STRIDE, HD, HD)
    P = dict(
        mix_norm=mix_norm, mlp_norm=mlp_norm,
        mlp_w1=[mlp_w1[l].astype(BF16) for l in range(mlp_w1.shape[0])],
        mlp_w2=[mlp_w2[l].astype(BF16) for l in range(mlp_w2.shape[0])],
        gdn_w_main=[gdn_w_in[l][:, :main].astype(BF16) for l in range(n_lay)],
        gdn_w_gate=jnp.pad(gdn_w_in[:, :, main:], ((0, 0), (0, 0), (0, HD - 2 * GDN_V_HEADS))).astype(BF16),
        gdn_gate_aux=gate_aux, gdn_conv_w=gdn_conv_w, gdn_out_norm=gdn_out_norm,
        gdn_w_out=[gdn_w_out[l].astype(BF16) for l in range(n_lay)],
        kv_norm=kv_norm, nsa_w_kv=nsa_w_kv.astype(BF16), kv_aux=kv_aux, k_cmp_norm=k_cmp_norm,
        cmp_w1cat=jnp.concatenate([w1r[:, 0], w1r[:, 1]], axis=-1).astype(BF16),
        cmp_w1flat=cmp_w1.reshape(2, CMP_BLOCK * HD, HD).astype(BF16),
        cmp_pe8=jnp.pad(cmp_pe.reshape(2, 1, CMP_BLOCK * HD), ((0, 0), (0, 7), (0, 0))),
        cmp_b1=cmp_b1, cmp_w2=cmp_w2.astype(BF16), cmp_b2=cmp_b2,
        nsa_w_q=nsa_w_in[:, :, :nq].astype(BF16),
        nsa_w_g=jnp.pad(nsa_w_in[:, :, nq:], ((0, 0), (0, 0), (0, HD - 3 * NSA_HEADS))).astype(BF16),
        nsa_q_aux=jnp.stack([jnp.tile(tile4(nsa_q_norm[jj]), (nq // 512, 1, 1)) for jj in range(2)]),
        nsa_w_out=nsa_w_out.astype(BF16), rel_bias=rel_bias,
    )
    return P


def kernel(x_prompt, x_sample, state_conv, state_ssm, cache_cmp, cache_sel, cache_win, page_table, mix_norm,
           mlp_norm, mlp_w1, mlp_w2, gdn_w_in, gdn_conv_w, gdn_a_log, gdn_dt_bias, gdn_out_norm, gdn_w_out,
           kv_norm, nsa_w_kv, k_sel_norm, k_win_norm, k_cmp_norm, cmp_pe, cmp_w1, cmp_b1, cmp_w2, cmp_b2,
           nsa_w_in, nsa_q_norm, nsa_w_out, rel_bias):
    bp, tp, _ = x_prompt.shape
    bs, ts, _ = x_sample.shape
    n_lay = gdn_w_in.shape[0]
    P = _prepare_params(mix_norm, mlp_norm, mlp_w1, mlp_w2, gdn_w_in, gdn_conv_w, gdn_a_log, gdn_dt_bias,
                        gdn_out_norm, gdn_w_out, kv_norm, nsa_w_kv, k_sel_norm, k_win_norm, k_cmp_norm, cmp_pe,
                        cmp_w1, cmp_b1, cmp_w2, cmp_b2, nsa_w_in, nsa_q_norm, nsa_w_out, rel_bias)

    conv0 =jnp.zeros((n_lay, bp, GDN_CONV - 1, GDN_CONV_DIM), F32)
    ssm0 = jnp.zeros((n_lay, bp, GDN_V_HEADS, HD, HD), F32)
    yp, conv_p, ssm_p, cmp_p, sel_p, win_p = _forward_group(
        x_prompt.reshape(bp * tp, D_MODEL), conv0, ssm0, None, P,
        batch=bp, seq=tp, t_valid=tp, q_pos0=0, tq=128)

    seq_s = GDN_CHUNK
    xs = jnp.pad(x_sample, ((0, 0), (0, seq_s - ts), (0, 0))).reshape(bs * seq_s, D_MODEL)
    n_pool = cache_cmp.shape[0]
    past = dict(cmp_rows=cache_cmp.reshape(n_pool * PAGE * 2 * NSA_KV, HD),
                sel_rows=cache_sel.reshape(n_pool * PAGE * 2 * NSA_KV, HD),
                page_table=page_table.astype(jnp.int32),
                win=cache_win.reshape(bs, cache_win.shape[1], 2 * NSA_KV * HD))
    n_past = page_table.shape[1] * PAGE
    ys, conv_s, ssm_s, cmp_s, sel_s, win_s = _forward_group(
        xs, state_conv, state_ssm, past, P, batch=bs, seq=seq_s, t_valid=ts, q_pos0=n_past, tq=8)
    y_sample = ys.reshape(bs, seq_s, D_MODEL)[:, :ts]
    return (yp.reshape(bp, tp, D_MODEL), y_sample, conv_p, ssm_p, cmp_p, sel_p, win_p,
            conv_s, ssm_s, cmp_s, sel_s, win_s)
```

```python
import functools
import math

import jax
import jax.numpy as jnp
from jax import lax
from jax.experimental import pallas as pl
from jax.experimental.pallas import tpu as pltpu

F32 = jnp.float32
BF16 = jnp.bfloat16

D_MODEL = 2048
D_FF = 4 * D_MODEL
NORM_EPS = 1e-6
L2_EPS = 1e-6
PAGE = 128

HD = 128
GDN_QK_HEADS = 16
GDN_V_HEADS = 32
GDN_KEY_DIM = GDN_QK_HEADS * HD
GDN_VAL_DIM = GDN_V_HEADS * HD
GDN_CONV = 4
GDN_CHUNK = 64
GDN_CONV_DIM = 2 * GDN_KEY_DIM + GDN_VAL_DIM

NSA_HEADS = 16
NSA_KV = 4
NSA_G = NSA_HEADS // NSA_KV
CMP_BLOCK = 32
CMP_STRIDE = 16
SEL_BLOCK = 64
SEL_TOPK = 16
WINDOW = 512
REL_BUCKETS = 32
REL_MAX_DIST = 4096
NEG = -1e30

TK = 128
FLASH_TQ = 256
N_DELTA = REL_MAX_DIST // TK + 2

VMEM_LIMIT = 56 * 1024 * 1024


def _cparams(sem):
    return pltpu.CompilerParams(dimension_semantics=sem, vmem_limit_bytes=VMEM_LIMIT)


def _sigmoid(x):
    return 1.0 / (1.0 + jnp.exp(-x))


def _softplus(x):
    return jnp.maximum(x, 0.0) + jnp.log(1.0 + jnp.exp(-jnp.abs(x)))


def _dot(a, b):
    return jnp.dot(a.astype(BF16), b.astype(BF16), preferred_element_type=F32)


def _dot_nt(a, b):
    return lax.dot_general(a.astype(BF16), b.astype(BF16), (((1,), (1,)), ((), ())),
                           preferred_element_type=F32)


def _dot_tn(a, b):
    return lax.dot_general(a.astype(BF16), b.astype(BF16), (((0,), (0,)), ((), ())),
                           preferred_element_type=F32)


def _headnorm(acc, gw):
    parts = []
    for g in range(acc.shape[1] // HD):
        a = acc[:, g * HD:(g + 1) * HD]
        parts.append(a * lax.rsqrt(jnp.mean(a * a, axis=-1, keepdims=True) + NORM_EPS))
    return jnp.concatenate(parts, axis=1) * gw


def _nmm_kernel(x_ref, nw_ref, w_ref, aux_ref, o_ref, h_ref, *, mode, norm_tiles, scale, seq, t_valid):
    i = pl.program_id(0)
    j = pl.program_id(1)

    @pl.when(j == 0)
    def _():
        x = x_ref[...]
        h = x * lax.rsqrt(jnp.mean(x * x, axis=-1, keepdims=True) + NORM_EPS) * nw_ref[...]
        h_ref[...] = h.astype(BF16)

    acc = jnp.dot(h_ref[...], w_ref[...], preferred_element_type=F32)
    if mode == "plain":
        o_ref[...] = acc.astype(o_ref.dtype)
    elif mode == "headnorm":
        if norm_tiles is None:
            o_ref[...] = (_headnorm(acc, aux_ref[0]) * scale).astype(o_ref.dtype)
        else:
            is_n = functools.reduce(jnp.logical_or, [j == t for t in norm_tiles])

            @pl.when(is_n)
            def _():
                o_ref[...] = (_headnorm(acc, aux_ref[0]) * scale).astype(o_ref.dtype)

            @pl.when(jnp.logical_not(is_n))
            def _():
                o_ref[...] = acc.astype(o_ref.dtype)
    elif mode == "sigmoid":
        o_ref[...] = _sigmoid(acc)
    elif mode == "gdn_gate":
        tm = acc.shape[0]
        aux = aux_ref[0]
        lane = lax.broadcasted_iota(jnp.int32, acc.shape, 1)
        row = lax.broadcasted_iota(jnp.int32, acc.shape, 0) + i * tm
        live = (row % seq) < t_valid
        beta = jnp.where(live, _sigmoid(acc), 0.0)
        g = jnp.where(live, -jnp.exp(aux[0:1, :]) * _softplus(acc + aux[1:2, :]), 0.0)
        g = jnp.where((lane >= GDN_V_HEADS) & (lane < 2 * GDN_V_HEADS), g, 0.0)
        r = lax.broadcasted_iota(jnp.int32, (tm, tm), 0)
        c = lax.broadcasted_iota(jnp.int32, (tm, tm), 1)
        tri = ((r // GDN_CHUNK) == (c // GDN_CHUNK)) & (c <= r)
        gcum = jnp.dot(jnp.where(tri, 1.0, 0.0), g, preferred_element_type=F32,
                       precision=lax.Precision.HIGHEST)
        o_ref[...] = jnp.where(lane < GDN_V_HEADS, beta, gcum)
    else:
        raise ValueError(mode)


def _nmm(x, nw, w, *, tn, out_dtype=F32, mode="plain", aux=None, norm_tiles=None, scale=1.0,
         seq=1, t_valid=1, tm=512, n_split=1, layer=0):
    M, K = x.shape
    N = w.shape[-1]
    tm = min(tm, M)
    assert M % tm == 0 and N % (tn * n_split) == 0
    if w.ndim == 3:
        w_spec = pl.BlockSpec((None, K, tn), lambda i, j: (layer, 0, j))
    else:
        w_spec = pl.BlockSpec((K, tn), lambda i, j: (0, j))
    if aux is None:
        aux = jnp.zeros((N // tn, 1, tn), F32)
    if n_split == 1:
        out_spec = pl.BlockSpec((tm, tn), lambda i, j: (i, j))
        out_shape = jax.ShapeDtypeStruct((M, N), out_dtype)
    else:
        per = N // n_split // tn
        out_spec = pl.BlockSpec((None, tm, tn), lambda i, j: (j // per, i, j % per))
        out_shape = jax.ShapeDtypeStruct((n_split, M, N // n_split), out_dtype)
    kern = functools.partial(_nmm_kernel, mode=mode, norm_tiles=norm_tiles, scale=scale, seq=seq,
                             t_valid=t_valid)
    return pl.pallas_call(
        kern,
        grid=(M // tm, N // tn),
        in_specs=[
            pl.BlockSpec((tm, K), lambda i, j: (i, 0)),
            pl.BlockSpec((1, K), lambda i, j: (0, 0)),
            w_spec,
            pl.BlockSpec((1,) + aux.shape[1:], lambda i, j: (j, 0, 0)),
        ],
        out_specs=out_spec,
        out_shape=out_shape,
        scratch_shapes=[pltpu.VMEM((tm, K), BF16)],
        compiler_params=_cparams(("parallel", "arbitrary")),
        name="nmm_" + mode,
    )(x, nw.reshape(1, K), w, aux)


def _mmres_kernel(x_ref, w_ref, r_ref, o_ref):
    o_ref[...] = r_ref[...] + jnp.dot(x_ref[...], w_ref[...], preferred_element_type=F32)


def _mm_res(x, w, res, layer, *, tm=512, tn=512):
    M, K = x.shape
    N = w.shape[2]
    tm = min(tm, M)
    return pl.pallas_call(
        _mmres_kernel,
        grid=(M // tm, N // tn),
        in_specs=[
            pl.BlockSpec((tm, K), lambda i, j: (i, 0)),
            pl.BlockSpec((None, K, tn), lambda i, j: (layer, 0, j)),
            pl.BlockSpec((tm, tn), lambda i, j: (i, j)),
        ],
        out_specs=pl.BlockSpec((tm, tn), lambda i, j: (i, j)),
        out_shape=jax.ShapeDtypeStruct((M, N), F32),
        compiler_params=_cparams(("parallel", "arbitrary")),
        name="mm_res",
    )(x, w, res)


def _mlp_kernel(x_ref, nw_ref, w1_ref, w2_ref, o_ref, h_ref, acc_ref):
    f = pl.program_id(1)

    @pl.when(f == 0)
    def _():
        x = x_ref[...]
        h = x * lax.rsqrt(jnp.mean(x * x, axis=-1, keepdims=True) + NORM_EPS) * nw_ref[...]
        h_ref[...] = h.astype(BF16)
        acc_ref[...] = x

    a = jnp.maximum(jnp.dot(h_ref[...], w1_ref[...], preferred_element_type=F32), 0.0)
    acc_ref[...] += jnp.dot((a * a).astype(BF16), w2_ref[...], preferred_element_type=F32)

    @pl.when(f == pl.num_programs(1) - 1)
    def _():
        o_ref[...] = acc_ref[...]


def _mlp(x, nw, w1, w2, layer, *, tm=512, tf=1024):
    M, D = x.shape
    Fdim = w1.shape[2]
    tm = min(tm, M)
    return pl.pallas_call(
        _mlp_kernel,
        grid=(M // tm, Fdim // tf),
        in_specs=[
            pl.BlockSpec((tm, D), lambda i, f: (i, 0)),
            pl.BlockSpec((1, D), lambda i, f: (0, 0)),
            pl.BlockSpec((None, D, tf), lambda i, f: (layer, 0, f)),
            pl.BlockSpec((None, tf, D), lambda i, f: (layer, f, 0)),
        ],
        out_specs=pl.BlockSpec((tm, D), lambda i, f: (i, 0)),
        out_shape=jax.ShapeDtypeStruct((M, D), F32),
        scratch_shapes=[pltpu.VMEM((tm, D), BF16), pltpu.VMEM((tm, D), F32)],
        compiler_params=_cparams(("parallel", "arbitrary")),
        name="mlp",
    )(x, nw.reshape(1, D), w1, w2)


def _unit_lower_inverse(mats, r, c):
    eye = jnp.where(r == c, 1.0, 0.0)
    in8 = (r // 8) == (c // 8)
    d0 = [jnp.where(in8, a, 0.0) for a in mats]
    d2 = [_dot(d, d) for d in d0]
    d4 = [_dot(d, d) for d in d2]
    x = [_dot(eye - a, eye + b) for a, b in zip(d0, d2)]
    x = [_dot(a, eye + b) for a, b in zip(x, d4)]
    s = 8
    while s < GDN_CHUNK:
        off = ((r // (2 * s)) == (c // (2 * s))) & ((r // s) != (c // s))
        bx = [_dot(jnp.where(off, a, 0.0), xi) for a, xi in zip(mats, x)]
        xbx = [_dot(xi, b) for xi, b in zip(x, bx)]
        x = [xi - b for xi, b in zip(x, xbx)]
        s *= 2
    return x


GDN_PAIRS = 8


def _gdn_chunk(qn, kn, vc, zb, beta, gc, st, onw):
    C = GDN_CHUNK
    R = 2 * C
    n = len(qn)
    rcol = lax.broadcasted_iota(jnp.int32, (R, 1), 0)
    top = rcol < C
    r = lax.broadcasted_iota(jnp.int32, (R, R), 0)
    c = lax.broadcasted_iota(jnp.int32, (R, R), 1)
    same = (r // C) == (c // C)
    low = same & (c <= r)
    slow = same & (c < r)
    srow = lax.broadcasted_iota(jnp.int32, (2 * HD, 1), 0)

    def blocked(a):
        return jnp.concatenate([jnp.where(top, a, 0.0), jnp.where(top, 0.0, a)], axis=1)

    beta2 = [jnp.concatenate(b, axis=0) for b in beta]
    gc2 = [jnp.concatenate(g, axis=0) for g in gc]
    gl2 = [jnp.where(top, g[0][C - 1:C, :], g[1][C - 1:C, :]) for g in gc]
    gls = [jnp.exp(jnp.where(srow < HD, g[0][C - 1:C, :], g[1][C - 1:C, :])) for g in gc]
    dec = []
    for g2 in gc2:
        colm = jnp.broadcast_to(g2, (R, R))
        dec.append(jnp.exp(jnp.where(low, colm - colm.T, NEG)))
    k2 = [jnp.concatenate([k, k], axis=0) for k in kn]
    q2 = [jnp.concatenate([q, q], axis=0) for q in qn]
    v2 = [jnp.concatenate([v[:, :HD], v[:, HD:]], axis=0) for v in vc]
    kk = [_dot_nt(k, k) for k in k2]
    qk = [_dot_nt(q, k) for q, k in zip(q2, k2)]
    amat = [jnp.where(slow, kk[i] * beta2[i] * dec[i], 0.0) for i in range(n)]
    attn = [qk[i] * dec[i] for i in range(n)]
    tinv = _unit_lower_inverse(amat, r, c)

    e2 = [jnp.exp(g) for g in gc2]
    rhs = [jnp.concatenate([v2[i] * beta2[i], k2[i] * beta2[i] * e2[i]], axis=1) for i in range(n)]
    sol = [_dot(tinv[i], rhs[i]) for i in range(n)]
    lhs = [jnp.concatenate([blocked(sol[i][:, HD:]), blocked(q2[i] * e2[i])], axis=0) for i in range(n)]
    ws = [_dot(lhs[i], st[i]) for i in range(n)]
    vnew = [sol[i][:, :HD] - ws[i][:R] for i in range(n)]
    av = [_dot(attn[i], vnew[i]) for i in range(n)]
    kd = [blocked(k2[i] * jnp.exp(gl2[i] - gc2[i])) for i in range(n)]
    kv = [_dot_tn(kd[i], vnew[i]) for i in range(n)]
    st_new = [st[i] * gls[i] + kv[i] for i in range(n)]

    outs = []
    for i in range(n):
        o2 = ws[i][R:] + av[i]
        z2 = jnp.concatenate([zb[i][:, :HD], zb[i][:, HD:]], axis=0)
        on = o2 * lax.rsqrt(jnp.mean(o2 * o2, axis=-1, keepdims=True) + NORM_EPS) * onw
        out2 = on * (z2 * _sigmoid(z2))
        outs.append(jnp.concatenate([out2[:C], out2[C:]], axis=1))
    return outs, st_new


def _gdn_kernel(q_ref, k_ref, v_ref, z_ref, bg_ref, wq_ref, wk_ref, wv_ref, cq_ref, ck_ref, cv_ref,
                s0_ref, onw_ref, o_ref, sout_ref, st_ref, bq_ref, bk_ref, bv_ref, *, single_chunk):
    C = GDN_CHUNK
    G = GDN_PAIRS
    jg = pl.program_id(1)
    ch = pl.program_id(2)

    def load_state():
        for p in range(G):
            st_ref[p, 0:HD, :] = s0_ref[0, 2 * p]
            st_ref[p, HD:2 * HD, :] = s0_ref[0, 2 * p + 1]
        bq_ref[5:8, :] = cq_ref[0]
        bk_ref[5:8, :] = ck_ref[0]
        bv_ref[5:8, :] = cv_ref[0]

    if single_chunk:
        load_state()
    else:
        pl.when(ch == 0)(load_state)

    def conv_silu(x_ref, buf_ref, w_ref):
        buf_ref[8:8 + C, :] = x_ref[...]
        w = w_ref[...]
        y = w[0:1, :] * buf_ref[5:5 + C, :]
        for t in range(1, GDN_CONV):
            y = y + w[t:t + 1, :] * buf_ref[5 + t:5 + t + C, :]
        buf_ref[5:8, :] = buf_ref[5 + C:8 + C, :]
        return y * _sigmoid(y)

    qc = conv_silu(q_ref, bq_ref, wq_ref)
    kc = conv_silu(k_ref, bk_ref, wk_ref)
    vc = conv_silu(v_ref, bv_ref, wv_ref)
    zb = z_ref[...]
    bg = bg_ref[...]
    lane = lax.broadcasted_iota(jnp.int32, bg.shape, 1)

    def col(idx):
        return jnp.sum(jnp.where(lane == idx, bg, 0.0), axis=-1, keepdims=True)

    qn, kn, beta, gc = [], [], [], []
    for p in range(G):
        qp = qc[:, p * HD:(p + 1) * HD]
        kp = kc[:, p * HD:(p + 1) * HD]
        qn.append(qp * lax.rsqrt(jnp.sum(qp * qp, axis=-1, keepdims=True) + L2_EPS) * (HD ** -0.5))
        kn.append(kp * lax.rsqrt(jnp.sum(kp * kp, axis=-1, keepdims=True) + L2_EPS))
        head = 2 * (jg * G + p)
        beta.append((col(head), col(head + 1)))
        gc.append((col(GDN_V_HEADS + head), col(GDN_V_HEADS + head + 1)))
    outs, new_states = _gdn_chunk(
        qn, kn, [vc[:, 2 * p * HD:2 * (p + 1) * HD] for p in range(G)],
        [zb[:, 2 * p * HD:2 * (p + 1) * HD] for p in range(G)], beta, gc,
        [st_ref[p] for p in range(G)], onw_ref[...])
    for p in range(G):
        st_ref[p] = new_states[p]
    o_ref[...] = jnp.concatenate(outs, axis=1).astype(o_ref.dtype)

    def write_state():
        for p in range(G):
            sout_ref[0, 2 * p] = new_states[p][:HD]
            sout_ref[0, 2 * p + 1] = new_states[p][HD:]

    if single_chunk:
        write_state()
    else:
        pl.when(ch == pl.num_programs(2) - 1)(write_state)


def _gdn(proj, bg, conv_w, conv0, ssm0, out_norm, *, batch, seq):
    C = GDN_CHUNK
    G = GDN_PAIRS
    nch = seq // C
    ng = GDN_QK_HEADS // G
    row = lambda b, j, c: b * nch + c
    return pl.pallas_call(
        functools.partial(_gdn_kernel, single_chunk=nch == 1),
        grid=(batch, ng, nch),
        in_specs=[
            pl.BlockSpec((C, G * HD), lambda b, j, c: (row(b, j, c), j)),
            pl.BlockSpec((C, G * HD), lambda b, j, c: (row(b, j, c), ng + j)),
            pl.BlockSpec((C, 2 * G * HD), lambda b, j, c: (row(b, j, c), ng + j)),
            pl.BlockSpec((C, 2 * G * HD), lambda b, j, c: (row(b, j, c), 2 * ng + j)),
            pl.BlockSpec((C, HD), lambda b, j, c: (row(b, j, c), 0)),
            pl.BlockSpec((GDN_CONV, G * HD), lambda b, j, c: (0, j)),
            pl.BlockSpec((GDN_CONV, G * HD), lambda b, j, c: (0, ng + j)),
            pl.BlockSpec((GDN_CONV, 2 * G * HD), lambda b, j, c: (0, ng + j)),
            pl.BlockSpec((1, GDN_CONV - 1, G * HD), lambda b, j, c: (b, 0, j)),
            pl.BlockSpec((1, GDN_CONV - 1, G * HD), lambda b, j, c: (b, 0, ng + j)),
            pl.BlockSpec((1, GDN_CONV - 1, 2 * G * HD), lambda b, j, c: (b, 0, ng + j)),
            pl.BlockSpec((1, 2 * G, HD, HD), lambda b, j, c: (b, j, 0, 0)),
            pl.BlockSpec((1, HD), lambda b, j, c: (0, 0)),
        ],
        out_specs=[
            pl.BlockSpec((C, 2 * G * HD), lambda b, j, c: (row(b, j, c), j)),
            pl.BlockSpec((1, 2 * G, HD, HD), lambda b, j, c: (b, j, 0, 0)),
        ],
        out_shape=[
            jax.ShapeDtypeStruct((batch * seq, GDN_VAL_DIM), BF16),
            jax.ShapeDtypeStruct((batch, GDN_V_HEADS, HD, HD), F32),
        ],
        scratch_shapes=[
            pltpu.VMEM((G, 2 * HD, HD), F32),
            pltpu.VMEM((8 + C, G * HD), F32),
            pltpu.VMEM((8 + C, G * HD), F32),
            pltpu.VMEM((8 + C, 2 * G * HD), F32),
        ],
        compiler_params=_cparams(("parallel", "parallel", "arbitrary")),
        name="gdn",
    )(proj, proj, proj, proj, bg, conv_w, conv_w, conv_w, conv0, conv0, conv0, ssm0,
      out_norm.reshape(1, HD))


CMP_PPS = 8


def _cmp1_kernel(pt_ref, *refs, row_packed):
    page_refs = refs[:CMP_PPS]
    w_ref = refs[CMP_PPS]
    o_ref = refs[CMP_PPS + 1]
    nseg = PAGE // CMP_STRIDE
    nch = 2 * NSA_KV
    pr = lax.broadcasted_iota(jnp.int32, (PAGE, PAGE), 0)
    pc = lax.broadcasted_iota(jnp.int32, (PAGE, PAGE), 1)
    perm = jnp.where(pc == (pr % nseg) * CMP_STRIDE + pr // nseg, 1.0, 0.0).astype(BF16)

    def slab(p, ch):
        if row_packed:
            return p[pl.ds(ch, PAGE, stride=nch), :]
        return p[0, :, ch * HD:(ch + 1) * HD]

    perm_slabs = [[jnp.dot(perm, slab(p, ch).astype(BF16), preferred_element_type=F32) for ch in range(nch)]
                  for p in page_refs]
    for cc in range(2):
        acc = jnp.zeros((NSA_KV * CMP_PPS * nseg, 2 * HD), F32)
        for rp in range(CMP_STRIDE // 2):
            lhs = jnp.concatenate(
                [jnp.concatenate([ps[cc * NSA_KV + h][(2 * rp) * nseg:(2 * rp + 1) * nseg],
                                  ps[cc * NSA_KV + h][(2 * rp + 1) * nseg:(2 * rp + 2) * nseg]], axis=1)
                 for h in range(NSA_KV) for ps in perm_slabs], axis=0)
            acc = acc + jnp.dot(lhs.astype(BF16), w_ref[cc, rp], preferred_element_type=F32)
        for h in range(NSA_KV):
            o_ref[cc, h] = acc[h * CMP_PPS * nseg:(h + 1) * CMP_PPS * nseg]


def _cmp_stage1(pages, ptab, w1cat, *, row_packed):
    n = ptab.shape[0]
    nst = n // CMP_PPS
    nseg = PAGE // CMP_STRIDE
    if row_packed:
        specs = [pl.BlockSpec((PAGE * 2 * NSA_KV, HD), lambda s, pt, p=p: (pt[s * CMP_PPS + p], 0))
                 for p in range(CMP_PPS)]
    else:
        specs = [pl.BlockSpec((1, PAGE, 2 * NSA_KV * HD), lambda s, pt, p=p: (pt[s * CMP_PPS + p], 0, 0))
                 for p in range(CMP_PPS)]
    grid_spec = pltpu.PrefetchScalarGridSpec(
        num_scalar_prefetch=1,
        grid=(nst,),
        in_specs=specs + [pl.BlockSpec((2, CMP_STRIDE // 2, 2 * HD, 2 * HD), lambda s, pt: (0, 0, 0, 0))],
        out_specs=pl.BlockSpec((2, NSA_KV, CMP_PPS * nseg, 2 * HD), lambda s, pt: (0, 0, s, 0)),
    )
    return pl.pallas_call(
        functools.partial(_cmp1_kernel, row_packed=row_packed),
        grid_spec=grid_spec,
        out_shape=jax.ShapeDtypeStruct((2, NSA_KV, n * nseg, 2 * HD), F32),
        compiler_params=_cparams(("arbitrary",)),
        name="cmp_stage1",
    )(ptab, *([pages] * CMP_PPS), w1cat.reshape(2, CMP_STRIDE // 2, 2 * HD, 2 * HD))


def _cmp2_kernel(a_ref, b_ref, pe_ref, w1_ref, b1_ref, w2_ref, b2_ref, nw_ref, o_ref):
    cc = pl.program_id(0)
    pe = pe_ref[0]
    pec = jnp.dot(pe.astype(BF16), w1_ref[0], preferred_element_type=F32)[0:1, :]
    hid = a_ref[0] + b_ref[0] + pec + b1_ref[0]
    hid = hid * _sigmoid(hid)
    out = jnp.dot(hid.astype(BF16), w2_ref[0], preferred_element_type=F32) + b2_ref[0]

    @pl.when(cc == 0)
    def _():
        o_ref[0] = out * lax.rsqrt(jnp.mean(out * out, axis=-1, keepdims=True) + NORM_EPS) * nw_ref[...]

    @pl.when(cc != 0)
    def _():
        o_ref[0] = out


def _cmp_stage2(a, b, pe8, w1flat, b1, w2, b2, nw, *, tr):
    R = a.shape[1]
    return pl.pallas_call(
        _cmp2_kernel,
        grid=(2, R // tr),
        in_specs=[
            pl.BlockSpec((1, tr, HD), lambda c, i: (c, i, 0)),
            pl.BlockSpec((1, tr, HD), lambda c, i: (c, i, 0)),
            pl.BlockSpec((1, 8, CMP_BLOCK * HD), lambda c, i: (c, 0, 0)),
            pl.BlockSpec((1, CMP_BLOCK * HD, HD), lambda c, i: (c, 0, 0)),
            pl.BlockSpec((1, 1, HD), lambda c, i: (c, 0, 0)),
            pl.BlockSpec((1, HD, HD), lambda c, i: (c, 0, 0)),
            pl.BlockSpec((1, 1, HD), lambda c, i: (c, 0, 0)),
            pl.BlockSpec((1, HD), lambda c, i: (0, 0)),
        ],
        out_specs=pl.BlockSpec((1, tr, HD), lambda c, i: (c, i, 0)),
        out_shape=jax.ShapeDtypeStruct((2, R, HD), F32),
        compiler_params=_cparams(("arbitrary", "arbitrary")),
        name="cmp_stage2",
    )(a, b, pe8, w1flat, b1.reshape(2, 1, HD), w2, b2.reshape(2, 1, HD), nw.reshape(1, HD))


def _stack_heads(qb):
    return jnp.concatenate([qb[:, g * HD:(g + 1) * HD] for g in range(NSA_G)], axis=0)


def _unstack_heads(o, tq):
    return jnp.concatenate([o[g * tq:(g + 1) * tq] for g in range(NSA_G)], axis=1)


def _gate_rows(gt):
    return jnp.concatenate([gt[:, g:g + 1] for g in range(NSA_G)], axis=0)


def _attn_cmp_kernel(q_ref, kc_ref, vc_ref, bias_ref, gate_ref, o_ref, sel_ref, *idx_ref, tq, q_pos0, nc, ns, nsp,
                     n_idx, sel_t):
    i = pl.program_id(2)
    ncp = kc_ref.shape[2]
    qs = _stack_heads(q_ref[...])
    logits = _dot_nt(qs, kc_ref[0, 0])
    logits = logits + jnp.concatenate([bias_ref[0, g] for g in range(NSA_G)], axis=0)
    rows = NSA_G * tq
    t4 = q_pos0 + i * tq + lax.broadcasted_iota(jnp.int32, (rows, ncp), 0) % tq
    cidx = lax.broadcasted_iota(jnp.int32, (rows, ncp), 1)
    mask = (cidx * CMP_STRIDE + (CMP_BLOCK - 1) <= t4) & (cidx < nc)
    lg = jnp.where(mask, logits, NEG)
    mx = jnp.max(lg, axis=-1, keepdims=True)
    ex = jnp.exp(lg - mx)
    p = ex / jnp.sum(ex, axis=-1, keepdims=True) * jnp.where(mask, 1.0, 0.0)
    oc = _dot(p, vc_ref[0, 0])
    o_ref[...] = _unstack_heads(oc * _gate_rows(gate_ref[0, 0, 0]), tq)

    psum = p[0:tq]
    for g in range(1, NSA_G):
        psum = psum + p[g * tq:(g + 1) * tq]
    top_k = min(SEL_TOPK, ns)
    if sel_t:
        nsr = sel_ref.shape[2]
        sr = lax.broadcasted_iota(jnp.int32, (nsr, ncp), 0)
        cc = lax.broadcasted_iota(jnp.int32, (nsr, ncp), 1)
        hit = (cc * CMP_STRIDE < sr * SEL_BLOCK + SEL_BLOCK) & (cc * CMP_STRIDE + CMP_BLOCK > sr * SEL_BLOCK)
        c2s_t = jnp.where(hit & (cc < nc) & (sr < ns), 1.0, 0.0)
        imp_t = jnp.dot(c2s_t, psum.T, preferred_element_type=F32, precision=lax.Precision.HIGHEST)
        tt = q_pos0 + i * tq + lax.broadcasted_iota(jnp.int32, (nsr, tq), 1)
        st = lax.broadcasted_iota(jnp.int32, (nsr, tq), 0)
        cur_t = tt // SEL_BLOCK
        forced_t = (st == 0) | (st == cur_t) | (st == cur_t - 1)
        score_t = jnp.where(forced_t, NSA_G + 1.0, jnp.where(st * SEL_BLOCK <= tt, imp_t, -1.0))
        score_t = jnp.where(st < ns, score_t, -2.0)
        rank_t = jnp.zeros((nsr, tq), F32)
        for sp in range(ns):
            other = score_t[sp:sp + 1, :]
            rank_t = rank_t + jnp.where(other > score_t, 1.0, jnp.where((other == score_t) & (sp < st), 1.0, 0.0))
        sel_ref[0, 0] = jnp.where((rank_t < top_k) & (st < ns), 1.0, 0.0)
        return
    cr = lax.broadcasted_iota(jnp.int32, (ncp, nsp), 0)
    sc = lax.broadcasted_iota(jnp.int32, (ncp, nsp), 1)
    c2s = (cr * CMP_STRIDE < sc * SEL_BLOCK + SEL_BLOCK) & (cr * CMP_STRIDE + CMP_BLOCK > sc * SEL_BLOCK)
    c2s = jnp.where(c2s & (cr < nc) & (sc < ns), 1.0, 0.0)
    imp = jnp.dot(psum, c2s, preferred_element_type=F32, precision=lax.Precision.HIGHEST)

    t = q_pos0 + i * tq + lax.broadcasted_iota(jnp.int32, (tq, nsp), 0)
    s = lax.broadcasted_iota(jnp.int32, (tq, nsp), 1)
    cur = t // SEL_BLOCK
    forced = (s == 0) | (s == cur) | (s == cur - 1)
    valid = s * SEL_BLOCK <= t
    score = jnp.where(forced, NSA_G + 1.0, jnp.where(valid, imp, -1.0))
    score = jnp.where(s < ns, score, -2.0)
    rank = jnp.zeros((tq, nsp), F32)
    for sp in range(ns):
        other = score[:, sp:sp + 1]
        ahead = (other > score) | ((other == score) & (sp < s))
        rank = rank + jnp.where(ahead, 1.0, 0.0)
    picked = (rank < top_k) & (s < ns)
    sel_ref[0, 0] = jnp.where(picked, 1.0, 0.0)
    if n_idx:
        listed = jnp.where(picked & (s < ns - 1), 1.0, 0.0)
        before = _dot(listed, jnp.where(lax.broadcasted_iota(jnp.int32, (nsp, nsp), 0)
                                        < lax.broadcasted_iota(jnp.int32, (nsp, nsp), 1), 1.0, 0.0))
        lane = lax.broadcasted_iota(jnp.int32, (tq, HD), 1)
        sf = s.astype(F32)
        out = jnp.zeros((tq, HD), F32)
        for kk in range(n_idx):
            hit = (listed > 0.5) & (before == float(kk))
            out = out + jnp.where(lane == kk, jnp.sum(jnp.where(hit, sf, 0.0), axis=-1, keepdims=True), 0.0)
        idx_ref[0][0, 0] = out.astype(jnp.int32)


def _attn_cmp(q, kcvc, bias_c, gate, *, batch, seq, tq, q_pos0, nc, ns, nsp, n_idx=0, sel_t=False):
    nqt = seq // tq
    ncp = kcvc.shape[3]
    kern = functools.partial(_attn_cmp_kernel, tq=tq, q_pos0=q_pos0, nc=nc, ns=ns, nsp=nsp, n_idx=n_idx,
                             sel_t=sel_t)
    rows_per_b = seq // tq
    nsr = -(-ns // 8) * 8
    if sel_t:
        sel_spec = pl.BlockSpec((1, 1, nsr, tq), lambda b, h, i: (b, h, 0, i))
        sel_shape = jax.ShapeDtypeStruct((batch, NSA_KV, nsr, seq), F32)
    else:
        sel_spec = pl.BlockSpec((1, 1, tq, nsp), lambda b, h, i: (b, h, i, 0))
        sel_shape = jax.ShapeDtypeStruct((batch, NSA_KV, seq, nsp), F32)
    extra_specs = [pl.BlockSpec((1, 1, tq, HD), lambda b, h, i: (b, h, i, 0))] if n_idx else []
    extra_shapes = [jax.ShapeDtypeStruct((batch, NSA_KV, seq, HD), jnp.int32)] if n_idx else []
    return pl.pallas_call(
        kern,
        grid=(batch, NSA_KV, nqt),
        in_specs=[
            pl.BlockSpec((tq, NSA_G * HD), lambda b, h, i: (b * rows_per_b + i, h)),
            pl.BlockSpec((None, 1, 1, ncp, HD), lambda b, h, i: (0, h, b, 0, 0)),
            pl.BlockSpec((None, 1, 1, ncp, HD), lambda b, h, i: (1, h, b, 0, 0)),
            pl.BlockSpec((1, NSA_G, tq, ncp), lambda b, h, i: (h, 0, i, 0)),
            pl.BlockSpec((1, 1, 1, tq, NSA_G), lambda b, h, i: (0, b, h, i, 0)),
        ],
        out_specs=[
            pl.BlockSpec((tq, NSA_G * HD), lambda b, h, i: (b * rows_per_b + i, h)),
            sel_spec,
        ] + extra_specs,
        out_shape=[jax.ShapeDtypeStruct((batch * seq, NSA_HEADS * HD), F32), sel_shape] + extra_shapes,
        compiler_params=_cparams(("parallel", "parallel", "arbitrary")),
        name="attn_cmp",
    )(q, kcvc, kcvc, bias_c, gate)


def _flash_kernel(pt_ref, *refs, cfg):
    pps, tq, has_tail, use_sel = cfg["pps"], cfg["tq"], cfg["has_tail"], cfg["use_sel"]
    it = iter(refs)
    q_ref = next(it)
    k_refs = [next(it) for _ in range(pps)]
    v_refs = [next(it) for _ in range(pps)]
    b_refs = [next(it) for _ in range(pps)]
    if has_tail:
        kt_ref, vt_ref, bt_ref = next(it), next(it), next(it)
    sel_ref = next(it) if use_sel else None
    gate_ref = next(it)
    prev_ref = next(it)
    o_ref = next(it)
    m_ref, l_ref, acc_ref = next(it), next(it), next(it)

    i = pl.program_id(2)
    st = pl.program_id(3)
    rows = NSA_G * tq

    @pl.when(st == 0)
    def _():
        m_ref[...] = jnp.full((rows, 1), NEG, F32)
        l_ref[...] = jnp.zeros((rows, 1), F32)
        acc_ref[...] = jnp.zeros((rows, HD), F32)

    t0 = cfg["q_pos0"] + i * tq

    def tile(k, v, bias4, kt, p0):
        qs = _stack_heads(q_ref[...])
        s = _dot_nt(qs, k) + jnp.concatenate([bias4[g] for g in range(NSA_G)], axis=0)
        tt = t0 + lax.broadcasted_iota(jnp.int32, (tq, TK), 0)
        pos = p0 + lax.broadcasted_iota(jnp.int32, (tq, TK), 1)
        dist = tt - pos
        ok = dist >= 0
        if use_sel:
            nsp = sel_ref.shape[3]
            sr = lax.broadcasted_iota(jnp.int32, (nsp, TK), 0)
            sc = lax.broadcasted_iota(jnp.int32, (nsp, TK), 1)
            expand = jnp.where(sr == kt * (TK // SEL_BLOCK) + sc // SEL_BLOCK, 1.0, 0.0)
            picked = _dot(sel_ref[0, 0], expand)
            ok = ok & (picked > 0.5)
        else:
            ok = ok & (dist < WINDOW) & (pos >= cfg["w_pos0"])
        okf = jnp.where(ok, 1.0, 0.0)
        ok4 = jnp.concatenate([okf] * NSA_G, axis=0)
        s = jnp.where(ok4 > 0.5, s, NEG)
        m_old = m_ref[...]
        m_new = jnp.maximum(m_old, jnp.max(s, axis=-1, keepdims=True))
        alpha = jnp.exp(m_old - m_new)
        p = jnp.exp(s - m_new) * ok4
        l_ref[...] = alpha * l_ref[...] + jnp.sum(p, axis=-1, keepdims=True)
        acc_ref[...] = alpha * acc_ref[...] + _dot(p, v)
        m_ref[...] = m_new

    for pp in range(pps):
        kt = cfg["tile_of"](i, st, pp)
        active = cfg["active"](i, st, pp)
        p0 = cfg["kbase"] + kt * TK
        if active is True:
            tile(k_refs[pp][0], v_refs[pp][0], b_refs[pp][0, 0], kt, p0)
        else:
            @pl.when(active)
            def _(pp=pp, kt=kt, p0=p0):
                tile(k_refs[pp][0], v_refs[pp][0], b_refs[pp][0, 0], kt, p0)

    if has_tail:
        @pl.when(st == pl.num_programs(3) - 1)
        def _():
            tile(kt_ref[0], vt_ref[0], bt_ref[0, 0], cfg["tail_tile"], cfg["kbase"] + cfg["tail_tile"] * TK)

    @pl.when(st == pl.num_programs(3) - 1)
    def _():
        o = acc_ref[...] / l_ref[...] * _gate_rows(gate_ref[0, 0, 0])
        o_ref[...] = (prev_ref[...] + _unstack_heads(o, tq)).astype(o_ref.dtype)


def _flash(q, pages, ptab, kcol, vcol, bias_tiles, tails, sel, gate, branch, prev, *, batch, seq, tq, q_pos0,
           pps, nsteps, tile_of, active, kbase, w_pos0, npt, tail_tile, out_dtype):
    nqt = seq // tq
    has_tail = tails is not None
    use_sel = sel is not None
    n_delta = bias_tiles.shape[1]
    cfg = dict(pps=pps, tq=tq, has_tail=has_tail, use_sel=use_sel, q_pos0=q_pos0, tile_of=tile_of,
               active=active, kbase=kbase, w_pos0=w_pos0, tail_tile=tail_tile)

    def page_idx(b, i, s, pp, pt):
        kt = jnp.clip(tile_of(i, s, pp), 0, npt - 1)
        return pt[b * npt + kt]

    def didx(i, s, pp):
        kt = tile_of(i, s, pp)
        return jnp.clip((q_pos0 + i * tq - kbase - kt * TK) // TK, 0, n_delta - 1)

    in_specs = [pl.BlockSpec((tq, NSA_G * HD), lambda b, h, i, s, pt: (b * nqt + i, h))]
    args = [q]
    for col in (kcol, vcol):
        for pp in range(pps):
            in_specs.append(pl.BlockSpec(
                (1, TK, HD), lambda b, h, i, s, pt, pp=pp, col=col: (page_idx(b, i, s, pp, pt), 0, col + h)))
            args.append(pages)
    for pp in range(pps):
        in_specs.append(pl.BlockSpec(
            (1, 1, NSA_G, tq, TK), lambda b, h, i, s, pt, pp=pp: (h, didx(i, s, pp), 0, 0, 0)))
        args.append(bias_tiles)
    if has_tail:
        tail_pages, tkcol, tvcol = tails
        tdelta = min(max((q_pos0 - kbase - tail_tile * TK) // TK, 0), n_delta - 1)
        in_specs.append(pl.BlockSpec((1, TK, HD), lambda b, h, i, s, pt: (b, 0, tkcol + h)))
        in_specs.append(pl.BlockSpec((1, TK, HD), lambda b, h, i, s, pt: (b, 0, tvcol + h)))
        in_specs.append(pl.BlockSpec((1, 1, NSA_G, tq, TK), lambda b, h, i, s, pt: (h, tdelta, 0, 0, 0)))
        args += [tail_pages, tail_pages, bias_tiles]
    if use_sel:
        nsp = sel.shape[3]
        in_specs.append(pl.BlockSpec((1, 1, tq, nsp), lambda b, h, i, s, pt: (b, h, i, 0)))
        args.append(sel)
    in_specs.append(pl.BlockSpec((1, 1, 1, tq, NSA_G), lambda b, h, i, s, pt: (branch, b, h, i, 0)))
    args.append(gate)
    in_specs.append(pl.BlockSpec((tq, NSA_G * HD), lambda b, h, i, s, pt: (b * nqt + i, h)))
    args.append(prev)

    rows = NSA_G * tq
    grid_spec = pltpu.PrefetchScalarGridSpec(
        num_scalar_prefetch=1,
        grid=(batch, NSA_KV, nqt, nsteps),
        in_specs=in_specs,
        out_specs=pl.BlockSpec((tq, NSA_G * HD), lambda b, h, i, s, pt: (b * nqt + i, h)),
        scratch_shapes=[pltpu.VMEM((rows, 1), F32), pltpu.VMEM((rows, 1), F32), pltpu.VMEM((rows, HD), F32)],
    )
    return pl.pallas_call(
        functools.partial(_flash_kernel, cfg=cfg),
        grid_spec=grid_spec,
        out_shape=jax.ShapeDtypeStruct((batch * seq, NSA_HEADS * HD), out_dtype),
        compiler_params=_cparams(("parallel", "parallel", "arbitrary", "arbitrary")),
        name="flash_sel" if use_sel else "flash_win",
    )(ptab, *args)


def _flasht_kernel(pt_ref, qi_ref, si_ref, lf_ref, kt_ref, ktc_ref, dd_ref, *refs, cfg):
    pps, tq, use_sel = cfg["pps"], cfg["tq"], cfg["use_sel"]
    it = iter(refs)
    q_ref = next(it)
    k_refs = [next(it) for _ in range(pps)]
    v_refs = [next(it) for _ in range(pps)]
    b_refs = [next(it) for _ in range(pps)]
    sel_ref = next(it) if use_sel else None
    gate_ref = next(it)
    prev_ref = next(it)
    o_ref = next(it)
    qt_ref, m_ref, l_ref, acc_ref = next(it), next(it), next(it), next(it)

    n = pl.program_id(2)
    i = qi_ref[n]
    st = si_ref[n]
    cols = NSA_G * tq

    @pl.when(st == 0)
    def _():
        qb = q_ref[...].astype(F32)
        qt_ref[...] = jnp.concatenate([qb[:, g * HD:(g + 1) * HD].T for g in range(NSA_G)], axis=1).astype(BF16)
        m_ref[...] = jnp.full((1, cols), NEG, F32)
        l_ref[...] = jnp.zeros((1, cols), F32)
        acc_ref[...] = jnp.zeros((HD, cols), F32)

    t0 = cfg["q_pos0"] + i * tq
    qt = qt_ref[...]
    tt = t0 + lax.broadcasted_iota(jnp.int32, (TK, tq), 1)
    krow = lax.broadcasted_iota(jnp.int32, (TK, tq), 0)
    scores, oks = [], []
    for pp in range(pps):
        kt = kt_ref[n * pps + pp]
        pos = cfg["kbase"] + kt * TK + krow
        dist = tt - pos
        ok = dist >= 0
        if use_sel:
            nsr = sel_ref.shape[2]
            kr = lax.broadcasted_iota(jnp.int32, (TK, nsr), 0)
            sc = lax.broadcasted_iota(jnp.int32, (TK, nsr), 1)
            expand = jnp.where(sc == kt * (TK // SEL_BLOCK) + kr // SEL_BLOCK, 1.0, 0.0)
            ok = ok & (_dot(expand, sel_ref[0, 0]) > 0.5)
        else:
            ok = ok & (dist < WINDOW) & (pos >= cfg["w_pos0"])
        ok4 = jnp.concatenate([ok] * NSA_G, axis=1)
        bias = jnp.concatenate([b_refs[pp][0, g, 0] for g in range(NSA_G)], axis=1).astype(F32)
        s = jnp.dot(k_refs[pp][0].astype(BF16), qt, preferred_element_type=F32) + bias
        scores.append(jnp.where(ok4, s, NEG))
        oks.append(ok4)
    m_old = m_ref[...]
    m_new = functools.reduce(jnp.maximum, [jnp.max(s, axis=0, keepdims=True) for s in scores] + [m_old])
    alpha = jnp.exp(m_old - m_new)
    ps = [jnp.where(ok4, jnp.exp(s - m_new), 0.0) for s, ok4 in zip(scores, oks)]
    l_new = alpha * l_ref[...]
    acc = alpha * acc_ref[...]
    for pp in range(pps):
        l_new = l_new + jnp.sum(ps[pp], axis=0, keepdims=True)
        acc = acc + _dot_tn(v_refs[pp][0], ps[pp])
    l_ref[...] = l_new
    acc_ref[...] = acc
    m_ref[...] = m_new

    @pl.when(lf_ref[n] == 1)
    def _():
        gt = gate_ref[0, 0, 0]
        grow = jnp.concatenate([gt[g:g + 1, :] for g in range(NSA_G)], axis=1)
        ot = acc / l_new * grow
        o = jnp.concatenate([ot[:, g * tq:(g + 1) * tq].T for g in range(NSA_G)], axis=1)
        o_ref[...] = (prev_ref[...] + o).astype(o_ref.dtype)


def _flasht(q, pages, ptab, kcol, vcol, bias_t, sel, gate_t, branch, prev, *, batch, seq, tq, q_pos0, pps,
            steps_of, tile_of, kbase, w_pos0, npt, out_dtype):
    nqt = seq // tq
    use_sel = sel is not None
    n_delta = bias_t.shape[2]
    cols = NSA_G * tq
    pairs = [(i, s) for i in range(nqt) for s in range(steps_of(i))]
    qi = jnp.asarray([p[0] for p in pairs], jnp.int32)
    si = jnp.asarray([p[1] for p in pairs], jnp.int32)
    lf = jnp.asarray([int(s == steps_of(i) - 1) for (i, s) in pairs], jnp.int32)
    cfg = dict(pps=pps, tq=tq, use_sel=use_sel, q_pos0=q_pos0, kbase=kbase, w_pos0=w_pos0)
    kt_raw = [tile_of(i, s, pp) for (i, s) in pairs for pp in range(pps)]
    kt = jnp.asarray(kt_raw, jnp.int32)
    ktc = jnp.asarray([min(max(k, 0), npt - 1) for k in kt_raw], jnp.int32)
    d_neg = tq // TK - 1
    dd = jnp.asarray([min(max((q_pos0 + i * tq - kbase - tile_of(i, s, pp) * TK) // TK + d_neg, 0), n_delta - 1)
                      for (i, s) in pairs for pp in range(pps)], jnp.int32)

    qmap = lambda b, h, n, pt, qi_, *_: (b * nqt + qi_[n], h)
    in_specs = [pl.BlockSpec((tq, NSA_G * HD), qmap)]
    args = [q]
    for col in (kcol, vcol):
        for pp in range(pps):
            in_specs.append(pl.BlockSpec(
                (1, TK, HD),
                lambda b, h, n, pt, qi_, si_, lf_, kt_, ktc_, dd_, pp=pp, col=col:
                (pt[b * npt + ktc_[n * pps + pp]], 0, col + h)))
            args.append(pages)
    for pp in range(pps):
        in_specs.append(pl.BlockSpec(
            (1, NSA_G, 1, TK, tq),
            lambda b, h, n, pt, qi_, si_, lf_, kt_, ktc_, dd_, pp=pp: (h, 0, dd_[n * pps + pp], 0, 0)))
        args.append(bias_t)
    if use_sel:
        nsr = sel.shape[2]
        in_specs.append(pl.BlockSpec((1, 1, nsr, tq), lambda b, h, n, pt, qi_, *_: (b, h, 0, qi_[n])))
        args.append(sel)
    in_specs.append(pl.BlockSpec((1, 1, 1, NSA_G, tq), lambda b, h, n, pt, qi_, *_: (branch, b, h, 0, qi_[n])))
    args.append(gate_t)
    in_specs.append(pl.BlockSpec((tq, NSA_G * HD), qmap))
    args.append(prev)

    grid_spec = pltpu.PrefetchScalarGridSpec(
        num_scalar_prefetch=7,
        grid=(batch, NSA_KV, len(pairs)),
        in_specs=in_specs,
        out_specs=pl.BlockSpec((tq, NSA_G * HD), qmap),
        scratch_shapes=[pltpu.VMEM((HD, cols), BF16), pltpu.VMEM((1, cols), F32), pltpu.VMEM((1, cols), F32),
                        pltpu.VMEM((HD, cols), F32)],
    )
    return pl.pallas_call(
        functools.partial(_flasht_kernel, cfg=cfg),
        grid_spec=grid_spec,
        out_shape=jax.ShapeDtypeStruct((batch * seq, NSA_HEADS * HD), out_dtype),
        compiler_params=_cparams(("parallel", "parallel", "arbitrary")),
        name="flasht_sel" if use_sel else "flasht_win",
    )(ptab, qi, si, lf, kt, ktc, dd, *args)


def _selg_kernel(idx_ref, pt_ref, q_ref, *refs, nblk, tq, q_pos0, tail_pos0):
    kv_refs = refs[:nblk]
    tk_ref, tv_ref, bt_ref, gate_ref, prev_ref, o_ref, osc_ref = refs[nblk:]
    ns = bt_ref.shape[2]
    b = pl.program_id(0)
    h = pl.program_id(1)
    qi = pl.program_id(2)
    nq = pl.num_programs(2)
    rows = NSA_G * tq
    nch = 2 * NSA_KV

    @pl.when(qi == 0)
    def _():
        osc_ref[...] = jnp.zeros((rows, HD), F32)

    qs = _stack_heads(q_ref[...])
    t = q_pos0 + qi
    base = ((b * NSA_KV + h) * nq + qi) * nblk
    jj = lax.broadcasted_iota(jnp.int32, (rows, SEL_BLOCK), 1)

    def bias_rows(blk):
        bb = bt_ref[0, 0, blk]
        return jnp.concatenate([jnp.broadcast_to(bb[g:g + 1, :], (tq, SEL_BLOCK)) for g in range(NSA_G)], axis=0)

    scores, vals = [], []
    for n in range(nblk):
        k = kv_refs[n][pl.ds(h, SEL_BLOCK, stride=nch), :]
        v = kv_refs[n][pl.ds(NSA_KV + h, SEL_BLOCK, stride=nch), :]
        blk = idx_ref[base + n]
        s = _dot_nt(qs, k) + bias_rows(blk)
        pos = blk * SEL_BLOCK + jj
        scores.append(jnp.where(pos <= t, s, NEG))
        vals.append(v)
    s = _dot_nt(qs, tk_ref[0]) + bias_rows(ns - 1)
    scores.append(jnp.where(tail_pos0 + jj <= t, s, NEG))
    vals.append(tv_ref[0])

    m = functools.reduce(jnp.maximum, [jnp.max(s, axis=-1, keepdims=True) for s in scores])
    l = jnp.zeros((rows, 1), F32)
    acc = jnp.zeros((rows, HD), F32)
    for s, v in zip(scores, vals):
        p = jnp.exp(s - m)
        l = l + jnp.sum(p, axis=-1, keepdims=True)
        acc = acc + _dot(p, v)
    rowq = lax.broadcasted_iota(jnp.int32, (rows, 1), 0) % tq
    osc = jnp.where(rowq == qi, acc / l, osc_ref[...])
    osc_ref[...] = osc

    @pl.when(qi == nq - 1)
    def _():
        o_ref[...] = prev_ref[...] + _unstack_heads(osc * _gate_rows(gate_ref[0, 0, 0]), tq)


def _sel_gather(q, cache_rows, ptab, idx, bias_blk, tail, gate, prev, *, batch, tq, nq, q_pos0, npt, ns):
    nblk = idx.shape[0] // (batch * NSA_KV * nq)
    half = SEL_BLOCK * 2 * NSA_KV
    per_page = PAGE // SEL_BLOCK

    def blk(b, h, qi, n, idx_ref):
        return idx_ref[((b * NSA_KV + h) * nq + qi) * nblk + n]

    def kv_map(n):
        def f(b, h, qi, idx_ref, pt_ref):
            s = blk(b, h, qi, n, idx_ref)
            return (pt_ref[b * npt + s // per_page] * per_page + s % per_page, 0)
        return f

    in_specs = [pl.BlockSpec((tq, NSA_G * HD), lambda b, h, qi, i_, p_: (b, h))]
    in_specs += [pl.BlockSpec((half, HD), kv_map(n)) for n in range(nblk)]
    in_specs += [
        pl.BlockSpec((1, SEL_BLOCK, HD), lambda b, h, qi, i_, p_: (b, 0, h)),
        pl.BlockSpec((1, SEL_BLOCK, HD), lambda b, h, qi, i_, p_: (b, 0, NSA_KV + h)),
        pl.BlockSpec((1, 1, ns, NSA_G, SEL_BLOCK), lambda b, h, qi, i_, p_: (qi, h, 0, 0, 0)),
        pl.BlockSpec((1, 1, 1, tq, NSA_G), lambda b, h, qi, i_, p_: (1, b, h, 0, 0)),
        pl.BlockSpec((tq, NSA_G * HD), lambda b, h, qi, i_, p_: (b, h)),
    ]
    grid_spec = pltpu.PrefetchScalarGridSpec(
        num_scalar_prefetch=2,
        grid=(batch, NSA_KV, nq),
        in_specs=in_specs,
        out_specs=pl.BlockSpec((tq, NSA_G * HD), lambda b, h, qi, i_, p_: (b, h)),
        scratch_shapes=[pltpu.VMEM((NSA_G * tq, HD), F32)],
    )
    kern = functools.partial(_selg_kernel, nblk=nblk, tq=tq, q_pos0=q_pos0, tail_pos0=(ns - 1) * SEL_BLOCK)
    return pl.pallas_call(
        kern,
        grid_spec=grid_spec,
        out_shape=jax.ShapeDtypeStruct((batch * tq, NSA_HEADS * HD), F32),
        compiler_params=_cparams(("parallel", "parallel", "arbitrary")),
        name="sel_gather",
    )(idx, ptab, q, *([cache_rows] * nblk), tail, tail, bias_blk, gate, prev)


def _rel_bucket(dist):
    n = jnp.maximum(dist, 0)
    max_exact = REL_BUCKETS // 2
    nf = jnp.maximum(n, 1).astype(F32)
    large = max_exact + (jnp.log(nf / max_exact) / math.log(REL_MAX_DIST / max_exact)
                         * (REL_BUCKETS - max_exact)).astype(jnp.int32)
    return jnp.where(n < max_exact, n, jnp.minimum(large, REL_BUCKETS - 1))


def _bias_by_distance(rel_bias):
    onehot = jax.nn.one_hot(_rel_bucket(jnp.arange(REL_MAX_DIST)), REL_BUCKETS, dtype=F32)
    return jnp.dot(onehot, rel_bias.astype(F32), precision=lax.Precision.HIGHEST)


def _toeplitz(r, nrows, ncols):
    p = r.shape[-1]
    flat = jnp.tile(r, (1,) * (r.ndim - 1) + (nrows,))[..., :nrows * (p - 1)]
    return flat.reshape(r.shape[:-1] + (nrows, p - 1))[..., :ncols]


def _bias_tiles(rel_bias, tq, n_delta, transposed=False, d_neg=0):
    fd = _bias_by_distance(rel_bias).T
    lo = TK * (1 + d_neg)
    span = TK * n_delta + tq
    hi = span - REL_MAX_DIST
    padded = jnp.concatenate([jnp.tile(fd[:, :1], (1, lo)), fd[:, :span]]
                             + ([jnp.tile(fd[:, -1:], (1, hi))] if hi > 0 else []), axis=1)
    z = jnp.stack([padded[:, TK * d + 1:TK * d + TK + tq] for d in range(n_delta + d_neg)], axis=1)
    if transposed:
        r = jnp.roll(z, -(TK - 1), axis=2).astype(BF16)
        return _toeplitz(r, TK, tq).reshape(NSA_KV, NSA_G, n_delta + d_neg, TK, tq)
    r = jnp.roll(z[:, :, ::-1], -(tq - 1), axis=2)
    t = _toeplitz(r, tq, TK)
    return t.reshape(NSA_KV, NSA_G, n_delta, tq, TK).transpose(0, 2, 1, 3, 4)


def _bias_cmp(rel_bias, q_pos0, tqs, ncp):
    fd = _bias_by_distance(rel_bias)
    last = CMP_BLOCK - 1
    if tqs <= SEL_BLOCK:
        dist = (q_pos0 + jnp.arange(tqs))[:, None] - (jnp.arange(ncp) * CMP_STRIDE + last)[None, :]
        b = fd[jnp.clip(dist, 0, REL_MAX_DIST - 1)]
        return b.reshape(tqs, ncp, NSA_KV, NSA_G).transpose(2, 3, 0, 1)
    assert q_pos0 == 0 and tqs % CMP_STRIDE == 0
    ntau = tqs // CMP_STRIDE
    period = ntau + ncp
    kappa = period - jnp.arange(period)
    dist = CMP_STRIDE * kappa[None, :] + jnp.arange(CMP_STRIDE)[:, None] - last
    dist = jnp.where(kappa[None, :] < ntau, dist, 0)
    r = fd[jnp.clip(dist, 0, REL_MAX_DIST - 1)]
    t = _toeplitz(r.transpose(2, 0, 1), ntau, ncp)
    return t.transpose(0, 2, 1, 3).reshape(NSA_KV, NSA_G, tqs, ncp)


def _bias_blocks(rel_bias, q_pos0, nq, ns):
    fd = _bias_by_distance(rel_bias)
    n = ns * SEL_BLOCK
    rows = []
    for qq in range(nq):
        t = q_pos0 + qq
        far = max(min(t - (REL_MAX_DIST - 1), n), 0)
        mid_hi = min(t + 1, n)
        parts = [jnp.tile(fd[-1:], (far, 1))] if far else []
        if mid_hi > far:
            parts.append(fd[t - mid_hi + 1:t - far + 1][::-1])
        if n > mid_hi:
            parts.append(jnp.tile(fd[:1], (n - mid_hi, 1)))
        rows.append(jnp.concatenate(parts, axis=0))
    t = jnp.stack(rows)
    return t.reshape(nq, ns, SEL_BLOCK, NSA_KV, NSA_G).transpose(0, 3, 1, 4, 2)


def _forward_group(x, conv0, ssm0, past, P, *, batch, seq, t_valid, q_pos0, tq):
    M = batch * seq
    conv_out, ssm_out = [], []
    for l in range(2):
        proj = _nmm(x, P["mix_norm"][l], P["gdn_w_main"], tn=1024, tm=1024, layer=l)
        bg = _nmm(x, P["mix_norm"][l], P["gdn_w_gate"][l], tn=128, mode="gdn_gate", aux=P["gdn_gate_aux"][l],
                  seq=seq, t_valid=t_valid)
        o, s_new = _gdn(proj, bg, P["gdn_conv_w"][l], conv0[l], ssm0[l], P["gdn_out_norm"][l],
                        batch=batch, seq=seq)
        conv_out.append(proj.reshape(batch, seq, -1)[:, t_valid - (GDN_CONV - 1):t_valid, :GDN_CONV_DIM])
        ssm_out.append(s_new)
        x = _mm_res(o, P["gdn_w_out"], x, l, tn=1024)
        x = _mlp(x, P["mlp_norm"][l], P["mlp_w1"], P["mlp_w2"], l)
    x, cmp_rows, sel_rows, win_state = _nsa_layers(x, past, P, batch=batch, seq=seq, t_valid=t_valid,
                                                   q_pos0=q_pos0, tq=tq)
    return x, jnp.stack(conv_out), jnp.stack(ssm_out), cmp_rows, sel_rows, win_state


def _nsa_layers(x, past, P, *, batch, seq, t_valid, q_pos0, tq):
    M = batch * seq
    kv = _nmm(x, P["kv_norm"], P["nsa_w_kv"], tn=512, tm=1024, mode="headnorm", aux=P["kv_aux"],
              norm_tiles=(2, 4), n_split=3)
    kv4 = kv.reshape(3, batch, seq, 2 * NSA_KV * HD)
    new_cmp, new_sel, win_new = (kv4[br][:, :t_valid] for br in range(3))
    cmp_rows = new_cmp.reshape(batch, t_valid, 2, NSA_KV, HD)
    sel_rows = new_sel.reshape(batch, t_valid, 2, NSA_KV, HD)

    ident = jnp.arange(M // PAGE, dtype=jnp.int32) if seq % PAGE == 0 else None
    if past is None:
        n_tot = t_valid
        npages = seq // PAGE
        kv_pages = kv.reshape(3 * M // PAGE, PAGE, 2 * NSA_KV * HD)
        first = _cmp_stage1(kv_pages, ident, P["cmp_w1cat"], row_packed=False)
        f6 = first.reshape(2, NSA_KV, batch, npages * (PAGE // CMP_STRIDE), 2 * HD)
        win_seq = win_new
        w_pos0 = 0
    else:
        n_past = past["page_table"].shape[1] * PAGE
        n_tot = n_past + t_valid
        npages = n_past // PAGE
        ptab = past["page_table"].reshape(-1)
        first = _cmp_stage1(past["cmp_rows"], ptab, P["cmp_w1cat"], row_packed=True)
        f6 = first.reshape(2, NSA_KV, batch, npages * (PAGE // CMP_STRIDE), 2 * HD)
        tail_cmp = jnp.pad(new_cmp, ((0, 0), (0, PAGE - t_valid), (0, 0)))
        tail_cmp = jnp.pad(tail_cmp, ((0, (-batch) % CMP_PPS), (0, 0), (0, 0)))
        tfirst = _cmp_stage1(tail_cmp, jnp.arange(tail_cmp.shape[0], dtype=jnp.int32), P["cmp_w1cat"],
                             row_packed=False)
        t6 = tfirst.reshape(2, NSA_KV, -1, PAGE // CMP_STRIDE, 2 * HD)[:, :, :batch, :(-(-t_valid // CMP_STRIDE))]
        f6 = jnp.concatenate([f6, t6], axis=3)
        win_seq = jnp.concatenate([past["win"], win_new], axis=1)
        w_pos0 = q_pos0 + t_valid - win_seq.shape[1]
    nc = -(-n_tot // CMP_STRIDE) - 1
    ns = -(-n_tot // SEL_BLOCK)
    ncp = -(-nc // 128) * 128
    nsp = -(-ns // 128) * 128
    a = f6[:, :, :, 0:nc, :HD]
    b = f6[:, :, :, 1:nc + 1, HD:]
    if b.shape[3] < nc:
        b = jnp.pad(b, ((0, 0), (0, 0), (0, 0), (0, nc - b.shape[3]), (0, 0)))
    a = jnp.pad(a, ((0, 0), (0, 0), (0, 0), (0, ncp - nc), (0, 0))).reshape(2, batch * NSA_KV * ncp, HD)
    b = jnp.pad(b, ((0, 0), (0, 0), (0, 0), (0, ncp - nc), (0, 0))).reshape(2, batch * NSA_KV * ncp, HD)
    R = batch * NSA_KV * ncp
    kcvc = _cmp_stage2(a, b, P["cmp_pe8"], P["cmp_w1flat"], P["cmp_b1"], P["cmp_w2"], P["cmp_b2"],
                       P["k_cmp_norm"], tr=min(R, 2048))
    kcvc = kcvc.reshape(2, NSA_KV, batch, ncp, HD)

    n_keep = min(WINDOW, win_seq.shape[1])
    win_state = win_seq[:, win_seq.shape[1] - n_keep:].reshape(batch, n_keep, 2, NSA_KV, HD)

    seq_q = seq if past is None else tq
    bias_c = _bias_cmp(P["rel_bias"], q_pos0, seq_q, ncp)
    n_delta = min(N_DELTA, (q_pos0 + seq_q) // TK + 1)
    if past is None:
        tqf = FLASH_TQ
        btiles_t = _bias_tiles(P["rel_bias"], tqf, n_delta, transposed=True, d_neg=tqf // TK - 1)
        sel_pages, sel_ptab, sel_npt = kv_pages, ident + M // PAGE, seq // PAGE
        sel_kcol, sel_vcol = 0, NSA_KV
        sel_pps = 4
        sel_tile_of = lambda i, s, pp: s * sel_pps + pp
        win_pages, win_ptab, win_npt = kv_pages, ident + 2 * (M // PAGE), seq // PAGE
        win_kcol, win_vcol = 0, NSA_KV
        win_pps = WINDOW // TK + tqf // TK
        win_tile_of = lambda i, s, pp: (i * tqf) // TK - WINDOW // TK + pp
        win_kbase = 0
    else:
        btiles = _bias_tiles(P["rel_bias"], tq, n_delta)
        assert n_past % SEL_BLOCK == 0 and t_valid <= SEL_BLOCK and ns - 1 > SEL_TOPK
        tail_sel = jnp.pad(new_sel, ((0, 0), (0, SEL_BLOCK - t_valid), (0, 0)))
        bias_blk = _bias_blocks(P["rel_bias"], q_pos0, t_valid, ns)
        nwt = -(-win_seq.shape[1] // TK)
        win_pages = jnp.pad(win_seq, ((0, 0), (0, nwt * TK - win_seq.shape[1]), (0, 0)))
        win_pages = win_pages.reshape(batch * nwt, TK, 2 * NSA_KV * HD)
        win_ptab, win_npt = jnp.arange(batch * nwt, dtype=jnp.int32), nwt
        win_kcol, win_vcol = 0, 4
        win_pps = nwt
        win_tile_of = lambda i, s, pp: pp
        win_active = lambda i, s, pp: True
        win_kbase = w_pos0

    for jj in range(2):
        l = 2 + jj
        q = _nmm(x, P["mix_norm"][l], P["nsa_w_q"], tn=512, tm=1024, layer=jj, out_dtype=BF16, mode="headnorm",
                 aux=P["nsa_q_aux"][jj], scale=HD ** -0.5)
        gates = _nmm(x, P["mix_norm"][l], P["nsa_w_g"][jj], tn=128, mode="sigmoid")
        gate = gates[:, :NSA_HEADS * 3].reshape(batch, seq, NSA_KV, NSA_G, 3).transpose(4, 0, 2, 1, 3)
        if seq_q != seq:
            q = q.reshape(batch, seq, -1)[:, :seq_q].reshape(batch * seq_q, -1)
            gate = gate[:, :, :, :seq_q]
        if past is None:
            gate_t = gates[:, :NSA_HEADS * 3].reshape(batch, seq, NSA_KV, NSA_G, 3).transpose(4, 0, 2, 3, 1)
            o_c, sel = _attn_cmp(q, kcvc, bias_c, gate, batch=batch, seq=seq_q, tq=tq, q_pos0=q_pos0, nc=nc,
                                 ns=ns, nsp=nsp, sel_t=True)
            o_s = _flasht(q, sel_pages, sel_ptab, sel_kcol, sel_vcol, btiles_t, sel, gate_t, 1, o_c,
                          batch=batch, seq=seq_q, tq=tqf, q_pos0=q_pos0, pps=sel_pps,
                          steps_of=lambda i: (i * tqf + tqf - 1) // (TK * sel_pps) + 1,
                          tile_of=sel_tile_of, kbase=0, w_pos0=0, npt=sel_npt, out_dtype=F32)
            o_w = _flasht(q, win_pages, win_ptab, win_kcol, win_vcol, btiles_t, None, gate_t, 2, o_s,
                          batch=batch, seq=seq_q, tq=tqf, q_pos0=q_pos0, pps=win_pps, steps_of=lambda i: 1,
                          tile_of=win_tile_of, kbase=win_kbase, w_pos0=w_pos0, npt=win_npt, out_dtype=BF16)
        else:
            o_c, sel, idx = _attn_cmp(q, kcvc, bias_c, gate, batch=batch, seq=seq_q, tq=tq, q_pos0=q_pos0,
                                      nc=nc, ns=ns, nsp=nsp, n_idx=SEL_TOPK - 1)
            o_s = _sel_gather(q, past["sel_rows"], past["page_table"].reshape(-1),
                              idx[:, :, :t_valid, :SEL_TOPK - 1].reshape(-1), bias_blk, tail_sel, gate, o_c,
                              batch=batch, tq=tq, nq=t_valid, q_pos0=q_pos0, npt=npages, ns=ns)
            o_w = _flash(q, win_pages, win_ptab, win_kcol, win_vcol, btiles, None, None, gate, 2, o_s,
                         batch=batch, seq=seq_q, tq=tq, q_pos0=q_pos0, pps=win_pps,
                         nsteps=1, tile_of=win_tile_of, active=win_active, kbase=win_kbase, w_pos0=w_pos0,
                         npt=win_npt, tail_tile=0, out_dtype=F32)
        if seq_q != seq:
            o_w = jnp.pad(o_w.reshape(batch, seq_q, -1), ((0, 0), (0, seq - seq_q), (0, 0))).reshape(M, -1)
        x = _mm_res(o_w.astype(BF16), P["nsa_w_out"], x, jj, tm=1024, tn=1024)
        x = _mlp(x, P["mlp_norm"][l], P["mlp_w1"], P["mlp_w2"], l)
    return x, cmp_rows, sel_rows, win_state


def _prepare_params(mix_norm, mlp_norm, mlp_w1, mlp_w2, gdn_w_in, gdn_conv_w, gdn_a_log, gdn_dt_bias,
                    gdn_out_norm, gdn_w_out, kv_norm, nsa_w_kv, k_sel_norm, k_win_norm, k_cmp_norm, cmp_pe,
                    cmp_w1, cmp_b1, cmp_w2, cmp_b2, nsa_w_in, nsa_q_norm, nsa_w_out, rel_bias):
    n_lay = gdn_w_in.shape[0]
    main = GDN_CONV_DIM + GDN_VAL_DIM
    zpad = lambda n: jnp.zeros((1, n), F32)
    gate_aux = jnp.stack([
        jnp.concatenate([
            jnp.concatenate([zpad(GDN_V_HEADS), gdn_a_log[l][None].astype(F32), zpad(HD - 2 * GDN_V_HEADS)], 1),
            jnp.concatenate([zpad(GDN_V_HEADS), gdn_dt_bias[l][None].astype(F32), zpad(HD - 2 * GDN_V_HEADS)], 1),
        ], 0)[None] for l in range(n_lay)])
    tile4 = lambda w: jnp.tile(w.astype(F32), NSA_KV)[None, None]
    kv_aux = jnp.concatenate([jnp.ones((2, 1, 512), F32), tile4(k_sel_norm), jnp.ones((1, 1, 512), F32),
                              tile4(k_win_norm), jnp.ones((1, 1, 512), F32)], 0)
    nq = NSA_HEADS * HD
    w1r = cmp_w1.reshape(2, 2, CMP_STRIDE, HD, HD)
    P = dict(
        mix_norm=mix_norm, mlp_norm=mlp_norm,
        mlp_w1=mlp_w1.astype(BF16), mlp_w2=mlp_w2.astype(BF16),
        gdn_w_main=gdn_w_in[:, :, :main].astype(BF16),
        gdn_w_gate=jnp.pad(gdn_w_in[:, :, main:], ((0, 0), (0, 0), (0, HD - 2 * GDN_V_HEADS))).astype(BF16),
        gdn_gate_aux=gate_aux, gdn_conv_w=gdn_conv_w, gdn_out_norm=gdn_out_norm,
        gdn_w_out=gdn_w_out.astype(BF16),
        kv_norm=kv_norm, nsa_w_kv=nsa_w_kv.astype(BF16), kv_aux=kv_aux, k_cmp_norm=k_cmp_norm,
        cmp_w1cat=jnp.concatenate([w1r[:, 0], w1r[:, 1]], axis=-1).astype(BF16),
        cmp_w1flat=cmp_w1.reshape(2, CMP_BLOCK * HD, HD).astype(BF16),
        cmp_pe8=jnp.pad(cmp_pe.reshape(2, 1, CMP_BLOCK * HD), ((0, 0), (0, 7), (0, 0))),
        cmp_b1=cmp_b1, cmp_w2=cmp_w2.astype(BF16), cmp_b2=cmp_b2,
        nsa_w_q=nsa_w_in[:, :, :nq].astype(BF16),
        nsa_w_g=jnp.pad(nsa_w_in[:, :, nq:], ((0, 0), (0, 0), (0, HD - 3 * NSA_HEADS))).astype(BF16),
        nsa_q_aux=jnp.stack([jnp.tile(tile4(nsa_q_norm[jj]), (nq // 512, 1, 1)) for jj in range(2)]),
        nsa_w_out=nsa_w_out.astype(BF16), rel_bias=rel_bias,
    )
    return P


def kernel(x_prompt, x_sample, state_conv, state_ssm, cache_cmp, cache_sel, cache_win, page_table, mix_norm,
           mlp_norm, mlp_w1, mlp_w2, gdn_w_in, gdn_conv_w, gdn_a_log, gdn_dt_bias, gdn_out_norm, gdn_w_out,
           kv_norm, nsa_w_kv, k_sel_norm, k_win_norm, k_cmp_norm, cmp_pe, cmp_w1, cmp_b1, cmp_w2, cmp_b2,
           nsa_w_in, nsa_q_norm, nsa_w_out, rel_bias):
    bp, tp, _ = x_prompt.shape
    bs, ts, _ = x_sample.shape
    n_lay = gdn_w_in.shape[0]
    P = _prepare_params(mix_norm, mlp_norm, mlp_w1, mlp_w2, gdn_w_in, gdn_conv_w, gdn_a_log, gdn_dt_bias,
                        gdn_out_norm, gdn_w_out, kv_norm, nsa_w_kv, k_sel_norm, k_win_norm, k_cmp_norm, cmp_pe,
                        cmp_w1, cmp_b1, cmp_w2, cmp_b2, nsa_w_in, nsa_q_norm, nsa_w_out, rel_bias)

    conv0 =jnp.zeros((n_lay, bp, GDN_CONV - 1, GDN_CONV_DIM), F32)
    ssm0 = jnp.zeros((n_lay, bp, GDN_V_HEADS, HD, HD), F32)
    yp, conv_p, ssm_p, cmp_p, sel_p, win_p = _forward_group(
        x_prompt.reshape(bp * tp, D_MODEL), conv0, ssm0, None, P,
        batch=bp, seq=tp, t_valid=tp, q_pos0=0, tq=128)

    seq_s = GDN_CHUNK
    xs = jnp.pad(x_sample, ((0, 0), (0, seq_s - ts), (0, 0))).reshape(bs * seq_s, D_MODEL)
    n_pool = cache_cmp.shape[0]
    past = dict(cmp_rows=cache_cmp.reshape(n_pool * PAGE * 2 * NSA_KV, HD),
                sel_rows=cache_sel.reshape(n_pool * PAGE * 2 * NSA_KV, HD),
                page_table=page_table.astype(jnp.int32),
                win=cache_win.reshape(bs, cache_win.shape[1], 2 * NSA_KV * HD))
    n_past = page_table.shape[1] * PAGE
    ys, conv_s, ssm_s, cmp_s, sel_s, win_s = _forward_group(
        xs, state_conv, state_ssm, past, P, batch=bs, seq=seq_s, t_valid=ts, q_pos0=n_past, tq=8)
    y_sample = ys.reshape(bs, seq_s, D_MODEL)[:, :ts]
    return (yp.reshape(bp, tp, D_MODEL), y_sample, conv_p, ssm_p, cmp_p, sel_p, win_p,
            conv_s, ssm_s, cmp_s, sel_s, win_s)
```

```python
import functools
import math

import jax
import jax.numpy as jnp
from jax import lax
from jax.experimental import pallas as pl
from jax.experimental.pallas import tpu as pltpu

F32 = jnp.float32
BF16 = jnp.bfloat16

D_MODEL = 2048
D_FF = 4 * D_MODEL
NORM_EPS = 1e-6
L2_EPS = 1e-6
PAGE = 128

HD = 128
GDN_QK_HEADS = 16
GDN_V_HEADS = 32
GDN_KEY_DIM = GDN_QK_HEADS * HD
GDN_VAL_DIM = GDN_V_HEADS * HD
GDN_CONV = 4
GDN_CHUNK = 64
GDN_CONV_DIM = 2 * GDN_KEY_DIM + GDN_VAL_DIM

NSA_HEADS = 16
NSA_KV = 4
NSA_G = NSA_HEADS // NSA_KV
CMP_BLOCK = 32
CMP_STRIDE = 16
SEL_BLOCK = 64
SEL_TOPK = 16
WINDOW = 512
REL_BUCKETS = 32
REL_MAX_DIST = 4096
NEG = -1e30

TK = 128
FLASH_TQ = 256
N_DELTA = REL_MAX_DIST // TK + 2

VMEM_LIMIT = 56 * 1024 * 1024


def _cparams(sem):
    return pltpu.CompilerParams(dimension_semantics=sem, vmem_limit_bytes=VMEM_LIMIT)


def _sigmoid(x):
    return 1.0 / (1.0 + jnp.exp(-x))


def _softplus(x):
    return jnp.maximum(x, 0.0) + jnp.log(1.0 + jnp.exp(-jnp.abs(x)))


def _dot(a, b):
    return jnp.dot(a.astype(BF16), b.astype(BF16), preferred_element_type=F32)


def _dot_nt(a, b):
    return lax.dot_general(a.astype(BF16), b.astype(BF16), (((1,), (1,)), ((), ())),
                           preferred_element_type=F32)


def _dot_tn(a, b):
    return lax.dot_general(a.astype(BF16), b.astype(BF16), (((0,), (0,)), ((), ())),
                           preferred_element_type=F32)


def _headnorm(acc, gw):
    parts = []
    for g in range(acc.shape[1] // HD):
        a = acc[:, g * HD:(g + 1) * HD]
        parts.append(a * lax.rsqrt(jnp.mean(a * a, axis=-1, keepdims=True) + NORM_EPS))
    return jnp.concatenate(parts, axis=1) * gw


def _nmm_kernel(x_ref, nw_ref, w_ref, aux_ref, o_ref, h_ref, *, mode, norm_tiles, scale, seq, t_valid):
    i = pl.program_id(0)
    j = pl.program_id(1)

    @pl.when(j == 0)
    def _():
        x = x_ref[...]
        h = x * lax.rsqrt(jnp.mean(x * x, axis=-1, keepdims=True) + NORM_EPS) * nw_ref[...]
        h_ref[...] = h.astype(BF16)

    acc = jnp.dot(h_ref[...], w_ref[...], preferred_element_type=F32)
    if mode == "plain":
        o_ref[...] = acc.astype(o_ref.dtype)
    elif mode == "headnorm":
        if norm_tiles is None:
            o_ref[...] = (_headnorm(acc, aux_ref[0]) * scale).astype(o_ref.dtype)
        else:
            is_n = functools.reduce(jnp.logical_or, [j == t for t in norm_tiles])

            @pl.when(is_n)
            def _():
                o_ref[...] = (_headnorm(acc, aux_ref[0]) * scale).astype(o_ref.dtype)

            @pl.when(jnp.logical_not(is_n))
            def _():
                o_ref[...] = acc.astype(o_ref.dtype)
    elif mode == "sigmoid":
        o_ref[...] = _sigmoid(acc)
    elif mode == "gdn_gate":
        tm = acc.shape[0]
        aux = aux_ref[0]
        lane = lax.broadcasted_iota(jnp.int32, acc.shape, 1)
        row = lax.broadcasted_iota(jnp.int32, acc.shape, 0) + i * tm
        live = (row % seq) < t_valid
        beta = jnp.where(live, _sigmoid(acc), 0.0)
        g = jnp.where(live, -jnp.exp(aux[0:1, :]) * _softplus(acc + aux[1:2, :]), 0.0)
        g = jnp.where((lane >= GDN_V_HEADS) & (lane < 2 * GDN_V_HEADS), g, 0.0)
        r = lax.broadcasted_iota(jnp.int32, (tm, tm), 0)
        c = lax.broadcasted_iota(jnp.int32, (tm, tm), 1)
        tri = ((r // GDN_CHUNK) == (c // GDN_CHUNK)) & (c <= r)
        gcum = jnp.dot(jnp.where(tri, 1.0, 0.0), g, preferred_element_type=F32,
                       precision=lax.Precision.HIGHEST)
        o_ref[...] = jnp.where(lane < GDN_V_HEADS, beta, gcum)
    else:
        raise ValueError(mode)


def _nmm(x, nw, w, *, tn, out_dtype=F32, mode="plain", aux=None, norm_tiles=None, scale=1.0,
         seq=1, t_valid=1, tm=512, n_split=1, layer=0):
    M, K = x.shape
    N = w.shape[-1]
    tm = min(tm, M)
    assert M % tm == 0 and N % (tn * n_split) == 0
    if w.ndim == 3:
        w_spec = pl.BlockSpec((None, K, tn), lambda i, j: (layer, 0, j))
    else:
        w_spec = pl.BlockSpec((K, tn), lambda i, j: (0, j))
    if aux is None:
        aux = jnp.zeros((N // tn, 1, tn), F32)
    if n_split == 1:
        out_spec = pl.BlockSpec((tm, tn), lambda i, j: (i, j))
        out_shape = jax.ShapeDtypeStruct((M, N), out_dtype)
    else:
        per = N // n_split // tn
        out_spec = pl.BlockSpec((None, tm, tn), lambda i, j: (j // per, i, j % per))
        out_shape = jax.ShapeDtypeStruct((n_split, M, N // n_split), out_dtype)
    kern = functools.partial(_nmm_kernel, mode=mode, norm_tiles=norm_tiles, scale=scale, seq=seq,
                             t_valid=t_valid)
    return pl.pallas_call(
        kern,
        grid=(M // tm, N // tn),
        in_specs=[
            pl.BlockSpec((tm, K), lambda i, j: (i, 0)),
            pl.BlockSpec((1, K), lambda i, j: (0, 0)),
            w_spec,
            pl.BlockSpec((1,) + aux.shape[1:], lambda i, j: (j, 0, 0)),
        ],
        out_specs=out_spec,
        out_shape=out_shape,
        scratch_shapes=[pltpu.VMEM((tm, K), BF16)],
        compiler_params=_cparams(("parallel", "arbitrary")),
        name="nmm_" + mode,
    )(x, nw.reshape(1, K), w, aux)


def _mmres_kernel(x_ref, w_ref, r_ref, o_ref):
    o_ref[...] = r_ref[...] + jnp.dot(x_ref[...], w_ref[...], preferred_element_type=F32)


def _mm_res(x, w, res, layer, *, tm=512, tn=512):
    M, K = x.shape
    N = w.shape[2]
    tm = min(tm, M)
    return pl.pallas_call(
        _mmres_kernel,
        grid=(M // tm, N // tn),
        in_specs=[
            pl.BlockSpec((tm, K), lambda i, j: (i, 0)),
            pl.BlockSpec((None, K, tn), lambda i, j: (layer, 0, j)),
            pl.BlockSpec((tm, tn), lambda i, j: (i, j)),
        ],
        out_specs=pl.BlockSpec((tm, tn), lambda i, j: (i, j)),
        out_shape=jax.ShapeDtypeStruct((M, N), F32),
        compiler_params=_cparams(("parallel", "arbitrary")),
        name="mm_res",
    )(x, w, res)


def _mlp_kernel(x_ref, nw_ref, w1_ref, w2_ref, o_ref, h_ref, acc_ref):
    f = pl.program_id(1)

    @pl.when(f == 0)
    def _():
        x = x_ref[...]
        h = x * lax.rsqrt(jnp.mean(x * x, axis=-1, keepdims=True) + NORM_EPS) * nw_ref[...]
        h_ref[...] = h.astype(BF16)
        acc_ref[...] = x

    a = jnp.maximum(jnp.dot(h_ref[...], w1_ref[...], preferred_element_type=F32), 0.0)
    acc_ref[...] += jnp.dot((a * a).astype(BF16), w2_ref[...], preferred_element_type=F32)

    @pl.when(f == pl.num_programs(1) - 1)
    def _():
        o_ref[...] = acc_ref[...]


def _mlp(x, nw, w1, w2, layer, *, tm=512, tf=1024):
    M, D = x.shape
    Fdim = w1.shape[2]
    tm = min(tm, M)
    return pl.pallas_call(
        _mlp_kernel,
        grid=(M // tm, Fdim // tf),
        in_specs=[
            pl.BlockSpec((tm, D), lambda i, f: (i, 0)),
            pl.BlockSpec((1, D), lambda i, f: (0, 0)),
            pl.BlockSpec((None, D, tf), lambda i, f: (layer, 0, f)),
            pl.BlockSpec((None, tf, D), lambda i, f: (layer, f, 0)),
        ],
        out_specs=pl.BlockSpec((tm, D), lambda i, f: (i, 0)),
        out_shape=jax.ShapeDtypeStruct((M, D), F32),
        scratch_shapes=[pltpu.VMEM((tm, D), BF16), pltpu.VMEM((tm, D), F32)],
        compiler_params=_cparams(("parallel", "arbitrary")),
        name="mlp",
    )(x, nw.reshape(1, D), w1, w2)


def _unit_lower_inverse(mats, r, c):
    eye = jnp.where(r == c, 1.0, 0.0)
    in8 = (r // 8) == (c // 8)
    d0 = [jnp.where(in8, a, 0.0) for a in mats]
    d2 = [_dot(d, d) for d in d0]
    d4 = [_dot(d, d) for d in d2]
    x = [_dot(eye - a, eye + b) for a, b in zip(d0, d2)]
    x = [_dot(a, eye + b) for a, b in zip(x, d4)]
    s = 8
    while s < GDN_CHUNK:
        off = ((r // (2 * s)) == (c // (2 * s))) & ((r // s) != (c // s))
        bx = [_dot(jnp.where(off, a, 0.0), xi) for a, xi in zip(mats, x)]
        xbx = [_dot(xi, b) for xi, b in zip(x, bx)]
        x = [xi - b for xi, b in zip(x, xbx)]
        s *= 2
    return x


GDN_PAIRS = 8


def _gdn_chunk(qn, kn, vc, zb, beta, gc, st, onw):
    C = GDN_CHUNK
    R = 2 * C
    n = len(qn)
    rcol = lax.broadcasted_iota(jnp.int32, (R, 1), 0)
    top = rcol < C
    r = lax.broadcasted_iota(jnp.int32, (R, R), 0)
    c = lax.broadcasted_iota(jnp.int32, (R, R), 1)
    same = (r // C) == (c // C)
    low = same & (c <= r)
    slow = same & (c < r)
    srow = lax.broadcasted_iota(jnp.int32, (2 * HD, 1), 0)

    beta2 = [jnp.concatenate(b, axis=0) for b in beta]
    gc2 = [jnp.concatenate(g, axis=0) for g in gc]
    gl2 = [jnp.where(top, g[0][C - 1:C, :], g[1][C - 1:C, :]) for g in gc]
    gls = [jnp.exp(jnp.where(srow < HD, g[0][C - 1:C, :], g[1][C - 1:C, :])) for g in gc]
    dec = []
    for g2 in gc2:
        colm = jnp.broadcast_to(g2, (R, R))
        dec.append(jnp.exp(jnp.where(low, colm - colm.T, NEG)))
    k2 = [jnp.concatenate([k, k], axis=0) for k in kn]
    q2 = [jnp.concatenate([q, q], axis=0) for q in qn]
    v2 = [jnp.concatenate([v[:, :HD], v[:, HD:]], axis=0) for v in vc]
    kk = [_dot_nt(k, k) for k in k2]
    qk = [_dot_nt(q, k) for q, k in zip(q2, k2)]
    amat = [jnp.where(slow, kk[i] * beta2[i] * dec[i], 0.0) for i in range(n)]
    attn = [qk[i] * dec[i] for i in range(n)]
    tinv = _unit_lower_inverse(amat, r, c)

    e2 = [jnp.exp(g) for g in gc2]
    rhs = [jnp.concatenate([v2[i] * beta2[i], k2[i] * beta2[i] * e2[i]], axis=1) for i in range(n)]
    sol = [_dot(tinv[i], rhs[i]) for i in range(n)]
    qg = [q2[i] * e2[i] for i in range(n)]
    ws_a = [_dot(jnp.concatenate([sol[i][:C, HD:], qg[i][:C]], axis=0), st[i][:HD]) for i in range(n)]
    ws_b = [_dot(jnp.concatenate([sol[i][C:, HD:], qg[i][C:]], axis=0), st[i][HD:]) for i in range(n)]
    vnew = [sol[i][:, :HD] - jnp.concatenate([ws_a[i][:C], ws_b[i][:C]], axis=0) for i in range(n)]
    av = [_dot(attn[i], vnew[i]) for i in range(n)]
    kd = [k2[i] * jnp.exp(gl2[i] - gc2[i]) for i in range(n)]
    kv_a = [_dot_tn(kd[i][:C], vnew[i][:C]) for i in range(n)]
    kv_b = [_dot_tn(kd[i][C:], vnew[i][C:]) for i in range(n)]
    st_new = [st[i] * gls[i] + jnp.concatenate([kv_a[i], kv_b[i]], axis=0) for i in range(n)]
    ws = [jnp.concatenate([ws_a[i][C:], ws_b[i][C:]], axis=0) for i in range(n)]

    outs = []
    for i in range(n):
        o2 = ws[i] + av[i]
        z2 = jnp.concatenate([zb[i][:, :HD], zb[i][:, HD:]], axis=0)
        on = o2 * lax.rsqrt(jnp.mean(o2 * o2, axis=-1, keepdims=True) + NORM_EPS) * onw
        out2 = on * (z2 * _sigmoid(z2))
        outs.append(jnp.concatenate([out2[:C], out2[C:]], axis=1))
    return outs, st_new


def _gdn_kernel(q_ref, k_ref, v_ref, z_ref, bg_ref, wq_ref, wk_ref, wv_ref, cq_ref, ck_ref, cv_ref,
                s0_ref, onw_ref, o_ref, sout_ref, st_ref, bq_ref, bk_ref, bv_ref, *, single_chunk):
    C = GDN_CHUNK
    G = GDN_PAIRS
    jg = pl.program_id(1)
    ch = pl.program_id(2)

    def load_state():
        for p in range(G):
            st_ref[p, 0:HD, :] = s0_ref[0, 2 * p]
            st_ref[p, HD:2 * HD, :] = s0_ref[0, 2 * p + 1]
        bq_ref[5:8, :] = cq_ref[0]
        bk_ref[5:8, :] = ck_ref[0]
        bv_ref[5:8, :] = cv_ref[0]

    if single_chunk:
        load_state()
    else:
        pl.when(ch == 0)(load_state)

    def conv_silu(x_ref, buf_ref, w_ref):
        buf_ref[8:8 + C, :] = x_ref[...]
        w = w_ref[...]
        y = w[0:1, :] * buf_ref[5:5 + C, :]
        for t in range(1, GDN_CONV):
            y = y + w[t:t + 1, :] * buf_ref[5 + t:5 + t + C, :]
        buf_ref[5:8, :] = buf_ref[5 + C:8 + C, :]
        return y * _sigmoid(y)

    qc = conv_silu(q_ref, bq_ref, wq_ref)
    kc = conv_silu(k_ref, bk_ref, wk_ref)
    vc = conv_silu(v_ref, bv_ref, wv_ref)
    zb = z_ref[...]
    bg = bg_ref[...]
    lane = lax.broadcasted_iota(jnp.int32, bg.shape, 1)

    def col(idx):
        return jnp.sum(jnp.where(lane == idx, bg, 0.0), axis=-1, keepdims=True)

    qn, kn, beta, gc = [], [], [], []
    for p in range(G):
        qp = qc[:, p * HD:(p + 1) * HD]
        kp = kc[:, p * HD:(p + 1) * HD]
        qn.append(qp * lax.rsqrt(jnp.sum(qp * qp, axis=-1, keepdims=True) + L2_EPS) * (HD ** -0.5))
        kn.append(kp * lax.rsqrt(jnp.sum(kp * kp, axis=-1, keepdims=True) + L2_EPS))
        head = 2 * (jg * G + p)
        beta.append((col(head), col(head + 1)))
        gc.append((col(GDN_V_HEADS + head), col(GDN_V_HEADS + head + 1)))
    outs, new_states = _gdn_chunk(
        qn, kn, [vc[:, 2 * p * HD:2 * (p + 1) * HD] for p in range(G)],
        [zb[:, 2 * p * HD:2 * (p + 1) * HD] for p in range(G)], beta, gc,
        [st_ref[p] for p in range(G)], onw_ref[...])
    for p in range(G):
        st_ref[p] = new_states[p]
    o_ref[...] = jnp.concatenate(outs, axis=1).astype(o_ref.dtype)

    def write_state():
        for p in range(G):
            sout_ref[0, 2 * p] = new_states[p][:HD]
            sout_ref[0, 2 * p + 1] = new_states[p][HD:]

    if single_chunk:
        write_state()
    else:
        pl.when(ch == pl.num_programs(2) - 1)(write_state)


def _gdn(proj, bg, conv_w, conv0, ssm0, out_norm, *, batch, seq):
    C = GDN_CHUNK
    G = GDN_PAIRS
    nch = seq // C
    ng = GDN_QK_HEADS // G
    row = lambda b, j, c: b * nch + c
    return pl.pallas_call(
        functools.partial(_gdn_kernel, single_chunk=nch == 1),
        grid=(batch, ng, nch),
        in_specs=[
            pl.BlockSpec((C, G * HD), lambda b, j, c: (row(b, j, c), j)),
            pl.BlockSpec((C, G * HD), lambda b, j, c: (row(b, j, c), ng + j)),
            pl.BlockSpec((C, 2 * G * HD), lambda b, j, c: (row(b, j, c), ng + j)),
            pl.BlockSpec((C, 2 * G * HD), lambda b, j, c: (row(b, j, c), 2 * ng + j)),
            pl.BlockSpec((C, HD), lambda b, j, c: (row(b, j, c), 0)),
            pl.BlockSpec((GDN_CONV, G * HD), lambda b, j, c: (0, j)),
            pl.BlockSpec((GDN_CONV, G * HD), lambda b, j, c: (0, ng + j)),
            pl.BlockSpec((GDN_CONV, 2 * G * HD), lambda b, j, c: (0, ng + j)),
            pl.BlockSpec((1, GDN_CONV - 1, G * HD), lambda b, j, c: (b, 0, j)),
            pl.BlockSpec((1, GDN_CONV - 1, G * HD), lambda b, j, c: (b, 0, ng + j)),
            pl.BlockSpec((1, GDN_CONV - 1, 2 * G * HD), lambda b, j, c: (b, 0, ng + j)),
            pl.BlockSpec((1, 2 * G, HD, HD), lambda b, j, c: (b, j, 0, 0)),
            pl.BlockSpec((1, HD), lambda b, j, c: (0, 0)),
        ],
        out_specs=[
            pl.BlockSpec((C, 2 * G * HD), lambda b, j, c: (row(b, j, c), j)),
            pl.BlockSpec((1, 2 * G, HD, HD), lambda b, j, c: (b, j, 0, 0)),
        ],
        out_shape=[
            jax.ShapeDtypeStruct((batch * seq, GDN_VAL_DIM), BF16),
            jax.ShapeDtypeStruct((batch, GDN_V_HEADS, HD, HD), F32),
        ],
        scratch_shapes=[
            pltpu.VMEM((G, 2 * HD, HD), F32),
            pltpu.VMEM((8 + C, G * HD), F32),
            pltpu.VMEM((8 + C, G * HD), F32),
            pltpu.VMEM((8 + C, 2 * G * HD), F32),
        ],
        compiler_params=_cparams(("parallel", "parallel", "arbitrary")),
        name="gdn",
    )(proj, proj, proj, proj, bg, conv_w, conv_w, conv_w, conv0, conv0, conv0, ssm0,
      out_norm.reshape(1, HD))


CMP_PPS = 8


def _cmp1_kernel(pt_ref, *refs, row_packed):
    page_refs = refs[:CMP_PPS]
    w_ref = refs[CMP_PPS]
    o_ref = refs[CMP_PPS + 1]
    nseg = PAGE // CMP_STRIDE
    nch = 2 * NSA_KV
    pr = lax.broadcasted_iota(jnp.int32, (PAGE, PAGE), 0)
    pc = lax.broadcasted_iota(jnp.int32, (PAGE, PAGE), 1)
    perm = jnp.where(pc == (pr % nseg) * CMP_STRIDE + pr // nseg, 1.0, 0.0).astype(BF16)

    def slab(p, ch):
        if row_packed:
            return p[pl.ds(ch, PAGE, stride=nch), :]
        return p[0, :, ch * HD:(ch + 1) * HD]

    perm_slabs = [[jnp.dot(perm, slab(p, ch).astype(BF16), preferred_element_type=F32) for ch in range(nch)]
                  for p in page_refs]
    for cc in range(2):
        acc = jnp.zeros((NSA_KV * CMP_PPS * nseg, 2 * HD), F32)
        for rp in range(CMP_STRIDE // 2):
            lhs = jnp.concatenate(
                [jnp.concatenate([ps[cc * NSA_KV + h][(2 * rp) * nseg:(2 * rp + 1) * nseg],
                                  ps[cc * NSA_KV + h][(2 * rp + 1) * nseg:(2 * rp + 2) * nseg]], axis=1)
                 for h in range(NSA_KV) for ps in perm_slabs], axis=0)
            acc = acc + jnp.dot(lhs.astype(BF16), w_ref[cc, rp], preferred_element_type=F32)
        for h in range(NSA_KV):
            o_ref[cc, h] = acc[h * CMP_PPS * nseg:(h + 1) * CMP_PPS * nseg]


def _cmp_stage1(pages, ptab, w1cat, *, row_packed):
    n = ptab.shape[0]
    nst = n // CMP_PPS
    nseg = PAGE // CMP_STRIDE
    if row_packed:
        specs = [pl.BlockSpec((PAGE * 2 * NSA_KV, HD), lambda s, pt, p=p: (pt[s * CMP_PPS + p], 0))
                 for p in range(CMP_PPS)]
    else:
        specs = [pl.BlockSpec((1, PAGE, 2 * NSA_KV * HD), lambda s, pt, p=p: (pt[s * CMP_PPS + p], 0, 0))
                 for p in range(CMP_PPS)]
    grid_spec = pltpu.PrefetchScalarGridSpec(
        num_scalar_prefetch=1,
        grid=(nst,),
        in_specs=specs + [pl.BlockSpec((2, CMP_STRIDE // 2, 2 * HD, 2 * HD), lambda s, pt: (0, 0, 0, 0))],
        out_specs=pl.BlockSpec((2, NSA_KV, CMP_PPS * nseg, 2 * HD), lambda s, pt: (0, 0, s, 0)),
    )
    return pl.pallas_call(
        functools.partial(_cmp1_kernel, row_packed=row_packed),
        grid_spec=grid_spec,
        out_shape=jax.ShapeDtypeStruct((2, NSA_KV, n * nseg, 2 * HD), F32),
        compiler_params=_cparams(("arbitrary",)),
        name="cmp_stage1",
    )(ptab, *([pages] * CMP_PPS), w1cat.reshape(2, CMP_STRIDE // 2, 2 * HD, 2 * HD))


def _cmp2_kernel(a_ref, b_ref, pe_ref, w1_ref, b1_ref, w2_ref, b2_ref, nw_ref, o_ref):
    cc = pl.program_id(0)
    pe = pe_ref[0]
    pec = jnp.dot(pe.astype(BF16), w1_ref[0], preferred_element_type=F32)[0:1, :]
    hid = a_ref[0] + b_ref[0] + pec + b1_ref[0]
    hid = hid * _sigmoid(hid)
    out = jnp.dot(hid.astype(BF16), w2_ref[0], preferred_element_type=F32) + b2_ref[0]

    @pl.when(cc == 0)
    def _():
        o_ref[0] = out * lax.rsqrt(jnp.mean(out * out, axis=-1, keepdims=True) + NORM_EPS) * nw_ref[...]

    @pl.when(cc != 0)
    def _():
        o_ref[0] = out


def _cmp_stage2(a, b, pe8, w1flat, b1, w2, b2, nw, *, tr):
    R = a.shape[1]
    return pl.pallas_call(
        _cmp2_kernel,
        grid=(2, R // tr),
        in_specs=[
            pl.BlockSpec((1, tr, HD), lambda c, i: (c, i, 0)),
            pl.BlockSpec((1, tr, HD), lambda c, i: (c, i, 0)),
            pl.BlockSpec((1, 8, CMP_BLOCK * HD), lambda c, i: (c, 0, 0)),
            pl.BlockSpec((1, CMP_BLOCK * HD, HD), lambda c, i: (c, 0, 0)),
            pl.BlockSpec((1, 1, HD), lambda c, i: (c, 0, 0)),
            pl.BlockSpec((1, HD, HD), lambda c, i: (c, 0, 0)),
            pl.BlockSpec((1, 1, HD), lambda c, i: (c, 0, 0)),
            pl.BlockSpec((1, HD), lambda c, i: (0, 0)),
        ],
        out_specs=pl.BlockSpec((1, tr, HD), lambda c, i: (c, i, 0)),
        out_shape=jax.ShapeDtypeStruct((2, R, HD), F32),
        compiler_params=_cparams(("arbitrary", "arbitrary")),
        name="cmp_stage2",
    )(a, b, pe8, w1flat, b1.reshape(2, 1, HD), w2, b2.reshape(2, 1, HD), nw.reshape(1, HD))


def _stack_heads(qb):
    return jnp.concatenate([qb[:, g * HD:(g + 1) * HD] for g in range(NSA_G)], axis=0)


def _unstack_heads(o, tq):
    return jnp.concatenate([o[g * tq:(g + 1) * tq] for g in range(NSA_G)], axis=1)


def _gate_rows(gt):
    return jnp.concatenate([gt[:, g:g + 1] for g in range(NSA_G)], axis=0)


def _attn_cmp_kernel(q_ref, kc_ref, vc_ref, bias_ref, gate_ref, o_ref, sel_ref, *idx_ref, tq, q_pos0, nc, ns, nsp,
                     n_idx, sel_t):
    i = pl.program_id(2)
    ncp = kc_ref.shape[2]
    qs = _stack_heads(q_ref[...])
    logits = _dot_nt(qs, kc_ref[0, 0])
    logits = logits + jnp.concatenate([bias_ref[0, g] for g in range(NSA_G)], axis=0)
    rows = NSA_G * tq
    t4 = q_pos0 + i * tq + lax.broadcasted_iota(jnp.int32, (rows, ncp), 0) % tq
    cidx = lax.broadcasted_iota(jnp.int32, (rows, ncp), 1)
    mask = (cidx * CMP_STRIDE + (CMP_BLOCK - 1) <= t4) & (cidx < nc)
    lg = jnp.where(mask, logits, NEG)
    mx = jnp.max(lg, axis=-1, keepdims=True)
    ex = jnp.exp(lg - mx)
    p = ex / jnp.sum(ex, axis=-1, keepdims=True) * jnp.where(mask, 1.0, 0.0)
    oc = _dot(p, vc_ref[0, 0])
    o_ref[...] = _unstack_heads(oc * _gate_rows(gate_ref[0, 0, 0]), tq)

    psum = p[0:tq]
    for g in range(1, NSA_G):
        psum = psum + p[g * tq:(g + 1) * tq]
    top_k = min(SEL_TOPK, ns)
    if sel_t:
        nsr = sel_ref.shape[2]
        sr = lax.broadcasted_iota(jnp.int32, (nsr, ncp), 0)
        cc = lax.broadcasted_iota(jnp.int32, (nsr, ncp), 1)
        hit = (cc * CMP_STRIDE < sr * SEL_BLOCK + SEL_BLOCK) & (cc * CMP_STRIDE + CMP_BLOCK > sr * SEL_BLOCK)
        c2s_t = jnp.where(hit & (cc < nc) & (sr < ns), 1.0, 0.0)
        imp_t = jnp.dot(c2s_t, psum.T, preferred_element_type=F32, precision=lax.Precision.HIGHEST)
        tt = q_pos0 + i * tq + lax.broadcasted_iota(jnp.int32, (nsr, tq), 1)
        st = lax.broadcasted_iota(jnp.int32, (nsr, tq), 0)
        cur_t = tt // SEL_BLOCK
        forced_t = (st == 0) | (st == cur_t) | (st == cur_t - 1)
        score_t = jnp.where(forced_t, NSA_G + 1.0, jnp.where(st * SEL_BLOCK <= tt, imp_t, -1.0))
        score_t = jnp.where(st < ns, score_t, -2.0)
        rank_t = jnp.zeros((nsr, tq), F32)
        for sp in range(ns):
            other = score_t[sp:sp + 1, :]
            rank_t = rank_t + jnp.where(other > score_t, 1.0, jnp.where((other == score_t) & (sp < st), 1.0, 0.0))
        sel_ref[0, 0] = jnp.where((rank_t < top_k) & (st < ns), 1.0, 0.0)
        return
    cr = lax.broadcasted_iota(jnp.int32, (ncp, nsp), 0)
    sc = lax.broadcasted_iota(jnp.int32, (ncp, nsp), 1)
    c2s = (cr * CMP_STRIDE < sc * SEL_BLOCK + SEL_BLOCK) & (cr * CMP_STRIDE + CMP_BLOCK > sc * SEL_BLOCK)
    c2s = jnp.where(c2s & (cr < nc) & (sc < ns), 1.0, 0.0)
    imp = jnp.dot(psum, c2s, preferred_element_type=F32, precision=lax.Precision.HIGHEST)

    t = q_pos0 + i * tq + lax.broadcasted_iota(jnp.int32, (tq, nsp), 0)
    s = lax.broadcasted_iota(jnp.int32, (tq, nsp), 1)
    cur = t // SEL_BLOCK
    forced = (s == 0) | (s == cur) | (s == cur - 1)
    valid = s * SEL_BLOCK <= t
    score = jnp.where(forced, NSA_G + 1.0, jnp.where(valid, imp, -1.0))
    score = jnp.where(s < ns, score, -2.0)
    rank = jnp.zeros((tq, nsp), F32)
    for sp in range(ns):
        other = score[:, sp:sp + 1]
        ahead = (other > score) | ((other == score) & (sp < s))
        rank = rank + jnp.where(ahead, 1.0, 0.0)
    picked = (rank < top_k) & (s < ns)
    sel_ref[0, 0] = jnp.where(picked, 1.0, 0.0)
    if n_idx:
        listed = jnp.where(picked & (s < ns - 1), 1.0, 0.0)
        before = _dot(listed, jnp.where(lax.broadcasted_iota(jnp.int32, (nsp, nsp), 0)
                                        < lax.broadcasted_iota(jnp.int32, (nsp, nsp), 1), 1.0, 0.0))
        lane = lax.broadcasted_iota(jnp.int32, (tq, HD), 1)
        sf = s.astype(F32)
        out = jnp.zeros((tq, HD), F32)
        for kk in range(n_idx):
            hit = (listed > 0.5) & (before == float(kk))
            out = out + jnp.where(lane == kk, jnp.sum(jnp.where(hit, sf, 0.0), axis=-1, keepdims=True), 0.0)
        idx_ref[0][0, 0] = out.astype(jnp.int32)


def _attn_cmp(q, kcvc, bias_c, gate, *, batch, seq, tq, q_pos0, nc, ns, nsp, n_idx=0, sel_t=False):
    nqt = seq // tq
    ncp = kcvc.shape[3]
    kern = functools.partial(_attn_cmp_kernel, tq=tq, q_pos0=q_pos0, nc=nc, ns=ns, nsp=nsp, n_idx=n_idx,
                             sel_t=sel_t)
    rows_per_b = seq // tq
    nsr = -(-ns // 8) * 8
    if sel_t:
        sel_spec = pl.BlockSpec((1, 1, nsr, tq), lambda b, h, i: (b, h, 0, i))
        sel_shape = jax.ShapeDtypeStruct((batch, NSA_KV, nsr, seq), F32)
    else:
        sel_spec = pl.BlockSpec((1, 1, tq, nsp), lambda b, h, i: (b, h, i, 0))
        sel_shape = jax.ShapeDtypeStruct((batch, NSA_KV, seq, nsp), F32)
    extra_specs = [pl.BlockSpec((1, 1, tq, HD), lambda b, h, i: (b, h, i, 0))] if n_idx else []
    extra_shapes = [jax.ShapeDtypeStruct((batch, NSA_KV, seq, HD), jnp.int32)] if n_idx else []
    return pl.pallas_call(
        kern,
        grid=(batch, NSA_KV, nqt),
        in_specs=[
            pl.BlockSpec((tq, NSA_G * HD), lambda b, h, i: (b * rows_per_b + i, h)),
            pl.BlockSpec((None, 1, 1, ncp, HD), lambda b, h, i: (0, h, b, 0, 0)),
            pl.BlockSpec((None, 1, 1, ncp, HD), lambda b, h, i: (1, h, b, 0, 0)),
            pl.BlockSpec((1, NSA_G, tq, ncp), lambda b, h, i: (h, 0, i, 0)),
            pl.BlockSpec((1, 1, 1, tq, NSA_G), lambda b, h, i: (0, b, h, i, 0)),
        ],
        out_specs=[
            pl.BlockSpec((tq, NSA_G * HD), lambda b, h, i: (b * rows_per_b + i, h)),
            sel_spec,
        ] + extra_specs,
        out_shape=[jax.ShapeDtypeStruct((batch * seq, NSA_HEADS * HD), F32), sel_shape] + extra_shapes,
        compiler_params=_cparams(("parallel", "parallel", "arbitrary")),
        name="attn_cmp",
    )(q, kcvc, kcvc, bias_c, gate)


def _flash_kernel(pt_ref, *refs, cfg):
    pps, tq, has_tail, use_sel = cfg["pps"], cfg["tq"], cfg["has_tail"], cfg["use_sel"]
    it = iter(refs)
    q_ref = next(it)
    k_refs = [next(it) for _ in range(pps)]
    v_refs = [next(it) for _ in range(pps)]
    b_refs = [next(it) for _ in range(pps)]
    if has_tail:
        kt_ref, vt_ref, bt_ref = next(it), next(it), next(it)
    sel_ref = next(it) if use_sel else None
    gate_ref = next(it)
    prev_ref = next(it)
    o_ref = next(it)
    m_ref, l_ref, acc_ref = next(it), next(it), next(it)

    i = pl.program_id(2)
    st = pl.program_id(3)
    rows = NSA_G * tq

    @pl.when(st == 0)
    def _():
        m_ref[...] = jnp.full((rows, 1), NEG, F32)
        l_ref[...] = jnp.zeros((rows, 1), F32)
        acc_ref[...] = jnp.zeros((rows, HD), F32)

    t0 = cfg["q_pos0"] + i * tq

    def tile(k, v, bias4, kt, p0):
        qs = _stack_heads(q_ref[...])
        s = _dot_nt(qs, k) + jnp.concatenate([bias4[g] for g in range(NSA_G)], axis=0)
        tt = t0 + lax.broadcasted_iota(jnp.int32, (tq, TK), 0)
        pos = p0 + lax.broadcasted_iota(jnp.int32, (tq, TK), 1)
        dist = tt - pos
        ok = dist >= 0
        if use_sel:
            nsp = sel_ref.shape[3]
            sr = lax.broadcasted_iota(jnp.int32, (nsp, TK), 0)
            sc = lax.broadcasted_iota(jnp.int32, (nsp, TK), 1)
            expand = jnp.where(sr == kt * (TK // SEL_BLOCK) + sc // SEL_BLOCK, 1.0, 0.0)
            picked = _dot(sel_ref[0, 0], expand)
            ok = ok & (picked > 0.5)
        else:
            ok = ok & (dist < WINDOW) & (pos >= cfg["w_pos0"])
        okf = jnp.where(ok, 1.0, 0.0)
        ok4 = jnp.concatenate([okf] * NSA_G, axis=0)
        s = jnp.where(ok4 > 0.5, s, NEG)
        m_old = m_ref[...]
        m_new = jnp.maximum(m_old, jnp.max(s, axis=-1, keepdims=True))
        alpha = jnp.exp(m_old - m_new)
        p = jnp.exp(s - m_new) * ok4
        l_ref[...] = alpha * l_ref[...] + jnp.sum(p, axis=-1, keepdims=True)
        acc_ref[...] = alpha * acc_ref[...] + _dot(p, v)
        m_ref[...] = m_new

    for pp in range(pps):
        kt = cfg["tile_of"](i, st, pp)
        active = cfg["active"](i, st, pp)
        p0 = cfg["kbase"] + kt * TK
        if active is True:
            tile(k_refs[pp][0], v_refs[pp][0], b_refs[pp][0, 0], kt, p0)
        else:
            @pl.when(active)
            def _(pp=pp, kt=kt, p0=p0):
                tile(k_refs[pp][0], v_refs[pp][0], b_refs[pp][0, 0], kt, p0)

    if has_tail:
        @pl.when(st == pl.num_programs(3) - 1)
        def _():
            tile(kt_ref[0], vt_ref[0], bt_ref[0, 0], cfg["tail_tile"], cfg["kbase"] + cfg["tail_tile"] * TK)

    @pl.when(st == pl.num_programs(3) - 1)
    def _():
        o = acc_ref[...] / l_ref[...] * _gate_rows(gate_ref[0, 0, 0])
        o_ref[...] = (prev_ref[...] + _unstack_heads(o, tq)).astype(o_ref.dtype)


def _flash(q, pages, ptab, kcol, vcol, bias_tiles, tails, sel, gate, branch, prev, *, batch, seq, tq, q_pos0,
           pps, nsteps, tile_of, active, kbase, w_pos0, npt, tail_tile, out_dtype):
    nqt = seq // tq
    has_tail = tails is not None
    use_sel = sel is not None
    n_delta = bias_tiles.shape[1]
    cfg = dict(pps=pps, tq=tq, has_tail=has_tail, use_sel=use_sel, q_pos0=q_pos0, tile_of=tile_of,
               active=active, kbase=kbase, w_pos0=w_pos0, tail_tile=tail_tile)

    def page_idx(b, i, s, pp, pt):
        kt = jnp.clip(tile_of(i, s, pp), 0, npt - 1)
        return pt[b * npt + kt]

    def didx(i, s, pp):
        kt = tile_of(i, s, pp)
        return jnp.clip((q_pos0 + i * tq - kbase - kt * TK) // TK, 0, n_delta - 1)

    in_specs = [pl.BlockSpec((tq, NSA_G * HD), lambda b, h, i, s, pt: (b * nqt + i, h))]
    args = [q]
    for col in (kcol, vcol):
        for pp in range(pps):
            in_specs.append(pl.BlockSpec(
                (1, TK, HD), lambda b, h, i, s, pt, pp=pp, col=col: (page_idx(b, i, s, pp, pt), 0, col + h)))
            args.append(pages)
    for pp in range(pps):
        in_specs.append(pl.BlockSpec(
            (1, 1, NSA_G, tq, TK), lambda b, h, i, s, pt, pp=pp: (h, didx(i, s, pp), 0, 0, 0)))
        args.append(bias_tiles)
    if has_tail:
        tail_pages, tkcol, tvcol = tails
        tdelta = min(max((q_pos0 - kbase - tail_tile * TK) // TK, 0), n_delta - 1)
        in_specs.append(pl.BlockSpec((1, TK, HD), lambda b, h, i, s, pt: (b, 0, tkcol + h)))
        in_specs.append(pl.BlockSpec((1, TK, HD), lambda b, h, i, s, pt: (b, 0, tvcol + h)))
        in_specs.append(pl.BlockSpec((1, 1, NSA_G, tq, TK), lambda b, h, i, s, pt: (h, tdelta, 0, 0, 0)))
        args += [tail_pages, tail_pages, bias_tiles]
    if use_sel:
        nsp = sel.shape[3]
        in_specs.append(pl.BlockSpec((1, 1, tq, nsp), lambda b, h, i, s, pt: (b, h, i, 0)))
        args.append(sel)
    in_specs.append(pl.BlockSpec((1, 1, 1, tq, NSA_G), lambda b, h, i, s, pt: (branch, b, h, i, 0)))
    args.append(gate)
    in_specs.append(pl.BlockSpec((tq, NSA_G * HD), lambda b, h, i, s, pt: (b * nqt + i, h)))
    args.append(prev)

    rows = NSA_G * tq
    grid_spec = pltpu.PrefetchScalarGridSpec(
        num_scalar_prefetch=1,
        grid=(batch, NSA_KV, nqt, nsteps),
        in_specs=in_specs,
        out_specs=pl.BlockSpec((tq, NSA_G * HD), lambda b, h, i, s, pt: (b * nqt + i, h)),
        scratch_shapes=[pltpu.VMEM((rows, 1), F32), pltpu.VMEM((rows, 1), F32), pltpu.VMEM((rows, HD), F32)],
    )
    return pl.pallas_call(
        functools.partial(_flash_kernel, cfg=cfg),
        grid_spec=grid_spec,
        out_shape=jax.ShapeDtypeStruct((batch * seq, NSA_HEADS * HD), out_dtype),
        compiler_params=_cparams(("parallel", "parallel", "arbitrary", "arbitrary")),
        name="flash_sel" if use_sel else "flash_win",
    )(ptab, *args)


def _flasht_kernel(pt_ref, qi_ref, si_ref, lf_ref, kt_ref, ktc_ref, dd_ref, *refs, cfg):
    pps, tq, use_sel = cfg["pps"], cfg["tq"], cfg["use_sel"]
    it = iter(refs)
    q_ref = next(it)
    k_refs = [next(it) for _ in range(pps)]
    v_refs = [next(it) for _ in range(pps)]
    b_refs = [next(it) for _ in range(pps)]
    sel_ref = next(it) if use_sel else None
    gate_ref = next(it)
    prev_ref = next(it)
    o_ref = next(it)
    qt_ref, m_ref, l_ref, acc_ref = next(it), next(it), next(it), next(it)

    n = pl.program_id(2)
    i = qi_ref[n]
    st = si_ref[n]
    cols = NSA_G * tq

    @pl.when(st == 0)
    def _():
        qb = q_ref[...].astype(F32)
        qt_ref[...] = jnp.concatenate([qb[:, g * HD:(g + 1) * HD].T for g in range(NSA_G)], axis=1).astype(BF16)
        m_ref[...] = jnp.full((1, cols), NEG, F32)
        l_ref[...] = jnp.zeros((1, cols), F32)
        acc_ref[...] = jnp.zeros((HD, cols), F32)

    t0 = cfg["q_pos0"] + i * tq
    qt = qt_ref[...]
    tt = t0 + lax.broadcasted_iota(jnp.int32, (TK, tq), 1)
    krow = lax.broadcasted_iota(jnp.int32, (TK, tq), 0)
    scores, oks = [], []
    for pp in range(pps):
        kt = kt_ref[n * pps + pp]
        pos = cfg["kbase"] + kt * TK + krow
        dist = tt - pos
        ok = dist >= 0
        if use_sel:
            nsr = sel_ref.shape[2]
            kr = lax.broadcasted_iota(jnp.int32, (TK, nsr), 0)
            sc = lax.broadcasted_iota(jnp.int32, (TK, nsr), 1)
            expand = jnp.where(sc == kt * (TK // SEL_BLOCK) + kr // SEL_BLOCK, 1.0, 0.0)
            ok = ok & (_dot(expand, sel_ref[0, 0]) > 0.5)
        else:
            ok = ok & (dist < WINDOW) & (pos >= cfg["w_pos0"])
        ok4 = jnp.concatenate([ok] * NSA_G, axis=1)
        bias = jnp.concatenate([b_refs[pp][0, g, 0] for g in range(NSA_G)], axis=1).astype(F32)
        s = jnp.dot(k_refs[pp][0].astype(BF16), qt, preferred_element_type=F32) + bias
        scores.append(jnp.where(ok4, s, NEG))
        oks.append(ok4)
    m_old = m_ref[...]
    m_new = functools.reduce(jnp.maximum, [jnp.max(s, axis=0, keepdims=True) for s in scores] + [m_old])
    alpha = jnp.exp(m_old - m_new)
    ps = [jnp.where(ok4, jnp.exp(s - m_new), 0.0) for s, ok4 in zip(scores, oks)]
    l_new = alpha * l_ref[...]
    acc = alpha * acc_ref[...]
    for pp in range(pps):
        l_new = l_new + jnp.sum(ps[pp], axis=0, keepdims=True)
        acc = acc + _dot_tn(v_refs[pp][0], ps[pp])
    l_ref[...] = l_new
    acc_ref[...] = acc
    m_ref[...] = m_new

    @pl.when(lf_ref[n] == 1)
    def _():
        gt = gate_ref[0, 0, 0]
        grow = jnp.concatenate([gt[g:g + 1, :] for g in range(NSA_G)], axis=1)
        ot = acc / l_new * grow
        o = jnp.concatenate([ot[:, g * tq:(g + 1) * tq].T for g in range(NSA_G)], axis=1)
        o_ref[...] = (prev_ref[...] + o).astype(o_ref.dtype)


def _flasht(q, pages, ptab, kcol, vcol, bias_t, sel, gate_t, branch, prev, *, batch, seq, tq, q_pos0, pps,
            steps_of, tile_of, kbase, w_pos0, npt, out_dtype):
    nqt = seq // tq
    use_sel = sel is not None
    n_delta = bias_t.shape[2]
    cols = NSA_G * tq
    pairs = [(i, s) for i in range(nqt) for s in range(steps_of(i))]
    qi = jnp.asarray([p[0] for p in pairs], jnp.int32)
    si = jnp.asarray([p[1] for p in pairs], jnp.int32)
    lf = jnp.asarray([int(s == steps_of(i) - 1) for (i, s) in pairs], jnp.int32)
    cfg = dict(pps=pps, tq=tq, use_sel=use_sel, q_pos0=q_pos0, kbase=kbase, w_pos0=w_pos0)
    kt_raw = [tile_of(i, s, pp) for (i, s) in pairs for pp in range(pps)]
    kt = jnp.asarray(kt_raw, jnp.int32)
    ktc = jnp.asarray([min(max(k, 0), npt - 1) for k in kt_raw], jnp.int32)
    d_neg = tq // TK - 1
    dd = jnp.asarray([min(max((q_pos0 + i * tq - kbase - tile_of(i, s, pp) * TK) // TK + d_neg, 0), n_delta - 1)
                      for (i, s) in pairs for pp in range(pps)], jnp.int32)

    qmap = lambda b, h, n, pt, qi_, *_: (b * nqt + qi_[n], h)
    in_specs = [pl.BlockSpec((tq, NSA_G * HD), qmap)]
    args = [q]
    for col in (kcol, vcol):
        for pp in range(pps):
            in_specs.append(pl.BlockSpec(
                (1, TK, HD),
                lambda b, h, n, pt, qi_, si_, lf_, kt_, ktc_, dd_, pp=pp, col=col:
                (pt[b * npt + ktc_[n * pps + pp]], 0, col + h)))
            args.append(pages)
    for pp in range(pps):
        in_specs.append(pl.BlockSpec(
            (1, NSA_G, 1, TK, tq),
            lambda b, h, n, pt, qi_, si_, lf_, kt_, ktc_, dd_, pp=pp: (h, 0, dd_[n * pps + pp], 0, 0)))
        args.append(bias_t)
    if use_sel:
        nsr = sel.shape[2]
        in_specs.append(pl.BlockSpec((1, 1, nsr, tq), lambda b, h, n, pt, qi_, *_: (b, h, 0, qi_[n])))
        args.append(sel)
    in_specs.append(pl.BlockSpec((1, 1, 1, NSA_G, tq), lambda b, h, n, pt, qi_, *_: (branch, b, h, 0, qi_[n])))
    args.append(gate_t)
    in_specs.append(pl.BlockSpec((tq, NSA_G * HD), qmap))
    args.append(prev)

    grid_spec = pltpu.PrefetchScalarGridSpec(
        num_scalar_prefetch=7,
        grid=(batch, NSA_KV, len(pairs)),
        in_specs=in_specs,
        out_specs=pl.BlockSpec((tq, NSA_G * HD), qmap),
        scratch_shapes=[pltpu.VMEM((HD, cols), BF16), pltpu.VMEM((1, cols), F32), pltpu.VMEM((1, cols), F32),
                        pltpu.VMEM((HD, cols), F32)],
    )
    return pl.pallas_call(
        functools.partial(_flasht_kernel, cfg=cfg),
        grid_spec=grid_spec,
        out_shape=jax.ShapeDtypeStruct((batch * seq, NSA_HEADS * HD), out_dtype),
        compiler_params=_cparams(("parallel", "parallel", "arbitrary")),
        name="flasht_sel" if use_sel else "flasht_win",
    )(ptab, qi, si, lf, kt, ktc, dd, *args)


def _selg_kernel(idx_ref, pt_ref, q_ref, *refs, nblk, tq, q_pos0, tail_pos0):
    kv_refs = refs[:nblk]
    tk_ref, tv_ref, bt_ref, gate_ref, prev_ref, o_ref, osc_ref = refs[nblk:]
    ns = bt_ref.shape[2]
    b = pl.program_id(0)
    h = pl.program_id(1)
    qi = pl.program_id(2)
    nq = pl.num_programs(2)
    rows = NSA_G * tq
    nch = 2 * NSA_KV

    @pl.when(qi == 0)
    def _():
        osc_ref[...] = jnp.zeros((rows, HD), F32)

    qs = _stack_heads(q_ref[...])
    t = q_pos0 + qi
    base = ((b * NSA_KV + h) * nq + qi) * nblk
    jj = lax.broadcasted_iota(jnp.int32, (rows, SEL_BLOCK), 1)

    def bias_rows(blk):
        bb = bt_ref[0, 0, blk]
        return jnp.concatenate([jnp.broadcast_to(bb[g:g + 1, :], (tq, SEL_BLOCK)) for g in range(NSA_G)], axis=0)

    scores, vals = [], []
    for n in range(nblk):
        k = kv_refs[n][pl.ds(h, SEL_BLOCK, stride=nch), :]
        v = kv_refs[n][pl.ds(NSA_KV + h, SEL_BLOCK, stride=nch), :]
        blk = idx_ref[base + n]
        s = _dot_nt(qs, k) + bias_rows(blk)
        pos = blk * SEL_BLOCK + jj
        scores.append(jnp.where(pos <= t, s, NEG))
        vals.append(v)
    s = _dot_nt(qs, tk_ref[0]) + bias_rows(ns - 1)
    scores.append(jnp.where(tail_pos0 + jj <= t, s, NEG))
    vals.append(tv_ref[0])

    m = functools.reduce(jnp.maximum, [jnp.max(s, axis=-1, keepdims=True) for s in scores])
    l = jnp.zeros((rows, 1), F32)
    acc = jnp.zeros((rows, HD), F32)
    for s, v in zip(scores, vals):
        p = jnp.exp(s - m)
        l = l + jnp.sum(p, axis=-1, keepdims=True)
        acc = acc + _dot(p, v)
    rowq = lax.broadcasted_iota(jnp.int32, (rows, 1), 0) % tq
    osc = jnp.where(rowq == qi, acc / l, osc_ref[...])
    osc_ref[...] = osc

    @pl.when(qi == nq - 1)
    def _():
        o_ref[...] = prev_ref[...] + _unstack_heads(osc * _gate_rows(gate_ref[0, 0, 0]), tq)


def _sel_gather(q, cache_rows, ptab, idx, bias_blk, tail, gate, prev, *, batch, tq, nq, q_pos0, npt, ns):
    nblk = idx.shape[0] // (batch * NSA_KV * nq)
    half = SEL_BLOCK * 2 * NSA_KV
    per_page = PAGE // SEL_BLOCK

    def blk(b, h, qi, n, idx_ref):
        return idx_ref[((b * NSA_KV + h) * nq + qi) * nblk + n]

    def kv_map(n):
        def f(b, h, qi, idx_ref, pt_ref):
            s = blk(b, h, qi, n, idx_ref)
            return (pt_ref[b * npt + s // per_page] * per_page + s % per_page, 0)
        return f

    in_specs = [pl.BlockSpec((tq, NSA_G * HD), lambda b, h, qi, i_, p_: (b, h))]
    in_specs += [pl.BlockSpec((half, HD), kv_map(n)) for n in range(nblk)]
    in_specs += [
        pl.BlockSpec((1, SEL_BLOCK, HD), lambda b, h, qi, i_, p_: (b, 0, h)),
        pl.BlockSpec((1, SEL_BLOCK, HD), lambda b, h, qi, i_, p_: (b, 0, NSA_KV + h)),
        pl.BlockSpec((1, 1, ns, NSA_G, SEL_BLOCK), lambda b, h, qi, i_, p_: (qi, h, 0, 0, 0)),
        pl.BlockSpec((1, 1, 1, tq, NSA_G), lambda b, h, qi, i_, p_: (1, b, h, 0, 0)),
        pl.BlockSpec((tq, NSA_G * HD), lambda b, h, qi, i_, p_: (b, h)),
    ]
    grid_spec = pltpu.PrefetchScalarGridSpec(
        num_scalar_prefetch=2,
        grid=(batch, NSA_KV, nq),
        in_specs=in_specs,
        out_specs=pl.BlockSpec((tq, NSA_G * HD), lambda b, h, qi, i_, p_: (b, h)),
        scratch_shapes=[pltpu.VMEM((NSA_G * tq, HD), F32)],
    )
    kern = functools.partial(_selg_kernel, nblk=nblk, tq=tq, q_pos0=q_pos0, tail_pos0=(ns - 1) * SEL_BLOCK)
    return pl.pallas_call(
        kern,
        grid_spec=grid_spec,
        out_shape=jax.ShapeDtypeStruct((batch * tq, NSA_HEADS * HD), F32),
        compiler_params=_cparams(("parallel", "parallel", "arbitrary")),
        name="sel_gather",
    )(idx, ptab, q, *([cache_rows] * nblk), tail, tail, bias_blk, gate, prev)


def _rel_bucket(dist):
    n = jnp.maximum(dist, 0)
    max_exact = REL_BUCKETS // 2
    nf = jnp.maximum(n, 1).astype(F32)
    large = max_exact + (jnp.log(nf / max_exact) / math.log(REL_MAX_DIST / max_exact)
                         * (REL_BUCKETS - max_exact)).astype(jnp.int32)
    return jnp.where(n < max_exact, n, jnp.minimum(large, REL_BUCKETS - 1))


def _bias_by_distance(rel_bias):
    onehot = jax.nn.one_hot(_rel_bucket(jnp.arange(REL_MAX_DIST)), REL_BUCKETS, dtype=F32)
    return jnp.dot(onehot, rel_bias.astype(F32), precision=lax.Precision.HIGHEST)


def _toeplitz(r, nrows, ncols):
    p = r.shape[-1]
    flat = jnp.tile(r, (1,) * (r.ndim - 1) + (nrows,))[..., :nrows * (p - 1)]
    return flat.reshape(r.shape[:-1] + (nrows, p - 1))[..., :ncols]


def _bias_tiles(rel_bias, tq, n_delta, transposed=False, d_neg=0):
    fd = _bias_by_distance(rel_bias).T
    lo = TK * (1 + d_neg)
    span = TK * n_delta + tq
    hi = span - REL_MAX_DIST
    padded = jnp.concatenate([jnp.tile(fd[:, :1], (1, lo)), fd[:, :span]]
                             + ([jnp.tile(fd[:, -1:], (1, hi))] if hi > 0 else []), axis=1)
    z = jnp.stack([padded[:, TK * d + 1:TK * d + TK + tq] for d in range(n_delta + d_neg)], axis=1)
    if transposed:
        r = jnp.roll(z, -(TK - 1), axis=2).astype(BF16)
        return _toeplitz(r, TK, tq).reshape(NSA_KV, NSA_G, n_delta + d_neg, TK, tq)
    r = jnp.roll(z[:, :, ::-1], -(tq - 1), axis=2)
    t = _toeplitz(r, tq, TK)
    return t.reshape(NSA_KV, NSA_G, n_delta, tq, TK).transpose(0, 2, 1, 3, 4)


def _bias_cmp(rel_bias, q_pos0, tqs, ncp):
    fd = _bias_by_distance(rel_bias)
    last = CMP_BLOCK - 1
    if tqs <= SEL_BLOCK:
        dist = (q_pos0 + jnp.arange(tqs))[:, None] - (jnp.arange(ncp) * CMP_STRIDE + last)[None, :]
        b = fd[jnp.clip(dist, 0, REL_MAX_DIST - 1)]
        return b.reshape(tqs, ncp, NSA_KV, NSA_G).transpose(2, 3, 0, 1)
    assert q_pos0 == 0 and tqs % CMP_STRIDE == 0
    ntau = tqs // CMP_STRIDE
    period = ntau + ncp
    kappa = period - jnp.arange(period)
    dist = CMP_STRIDE * kappa[None, :] + jnp.arange(CMP_STRIDE)[:, None] - last
    dist = jnp.where(kappa[None, :] < ntau, dist, 0)
    r = fd[jnp.clip(dist, 0, REL_MAX_DIST - 1)]
    t = _toeplitz(r.transpose(2, 0, 1), ntau, ncp)
    return t.transpose(0, 2, 1, 3).reshape(NSA_KV, NSA_G, tqs, ncp)


def _bias_blocks(rel_bias, q_pos0, nq, ns):
    fd = _bias_by_distance(rel_bias)
    n = ns * SEL_BLOCK
    rows = []
    for qq in range(nq):
        t = q_pos0 + qq
        far = max(min(t - (REL_MAX_DIST - 1), n), 0)
        mid_hi = min(t + 1, n)
        parts = [jnp.tile(fd[-1:], (far, 1))] if far else []
        if mid_hi > far:
            parts.append(fd[t - mid_hi + 1:t - far + 1][::-1])
        if n > mid_hi:
            parts.append(jnp.tile(fd[:1], (n - mid_hi, 1)))
        rows.append(jnp.concatenate(parts, axis=0))
    t = jnp.stack(rows)
    return t.reshape(nq, ns, SEL_BLOCK, NSA_KV, NSA_G).transpose(0, 3, 1, 4, 2)


def _forward_group(x, conv0, ssm0, past, P, *, batch, seq, t_valid, q_pos0, tq):
    M = batch * seq
    conv_out, ssm_out = [], []
    for l in range(2):
        proj = _nmm(x, P["mix_norm"][l], P["gdn_w_main"], tn=1024, tm=1024, layer=l)
        bg = _nmm(x, P["mix_norm"][l], P["gdn_w_gate"][l], tn=128, mode="gdn_gate", aux=P["gdn_gate_aux"][l],
                  seq=seq, t_valid=t_valid)
        o, s_new = _gdn(proj, bg, P["gdn_conv_w"][l], conv0[l], ssm0[l], P["gdn_out_norm"][l],
                        batch=batch, seq=seq)
        conv_out.append(proj.reshape(batch, seq, -1)[:, t_valid - (GDN_CONV - 1):t_valid, :GDN_CONV_DIM])
        ssm_out.append(s_new)
        x = _mm_res(o, P["gdn_w_out"], x, l, tn=1024)
        x = _mlp(x, P["mlp_norm"][l], P["mlp_w1"], P["mlp_w2"], l)
    x, cmp_rows, sel_rows, win_state = _nsa_layers(x, past, P, batch=batch, seq=seq, t_valid=t_valid,
                                                   q_pos0=q_pos0, tq=tq)
    return x, jnp.stack(conv_out), jnp.stack(ssm_out), cmp_rows, sel_rows, win_state


def _nsa_layers(x, past, P, *, batch, seq, t_valid, q_pos0, tq):
    M = batch * seq
    kv = _nmm(x, P["kv_norm"], P["nsa_w_kv"], tn=512, tm=1024, mode="headnorm", aux=P["kv_aux"],
              norm_tiles=(2, 4), n_split=3)
    kv4 = kv.reshape(3, batch, seq, 2 * NSA_KV * HD)
    new_cmp, new_sel, win_new = (kv4[br][:, :t_valid] for br in range(3))
    cmp_rows = new_cmp.reshape(batch, t_valid, 2, NSA_KV, HD)
    sel_rows = new_sel.reshape(batch, t_valid, 2, NSA_KV, HD)

    ident = jnp.arange(M // PAGE, dtype=jnp.int32) if seq % PAGE == 0 else None
    if past is None:
        n_tot = t_valid
        npages = seq // PAGE
        kv_pages = kv.reshape(3 * M // PAGE, PAGE, 2 * NSA_KV * HD)
        first = _cmp_stage1(kv_pages, ident, P["cmp_w1cat"], row_packed=False)
        f6 = first.reshape(2, NSA_KV, batch, npages * (PAGE // CMP_STRIDE), 2 * HD)
        win_seq = win_new
        w_pos0 = 0
    else:
        n_past = past["page_table"].shape[1] * PAGE
        n_tot = n_past + t_valid
        npages = n_past // PAGE
        ptab = past["page_table"].reshape(-1)
        first = _cmp_stage1(past["cmp_rows"], ptab, P["cmp_w1cat"], row_packed=True)
        f6 = first.reshape(2, NSA_KV, batch, npages * (PAGE // CMP_STRIDE), 2 * HD)
        tail_cmp = jnp.pad(new_cmp, ((0, 0), (0, PAGE - t_valid), (0, 0)))
        tail_cmp = jnp.pad(tail_cmp, ((0, (-batch) % CMP_PPS), (0, 0), (0, 0)))
        tfirst = _cmp_stage1(tail_cmp, jnp.arange(tail_cmp.shape[0], dtype=jnp.int32), P["cmp_w1cat"],
                             row_packed=False)
        t6 = tfirst.reshape(2, NSA_KV, -1, PAGE // CMP_STRIDE, 2 * HD)[:, :, :batch, :(-(-t_valid // CMP_STRIDE))]
        f6 = jnp.concatenate([f6, t6], axis=3)
        win_seq = jnp.concatenate([past["win"], win_new], axis=1)
        w_pos0 = q_pos0 + t_valid - win_seq.shape[1]
    nc = -(-n_tot // CMP_STRIDE) - 1
    ns = -(-n_tot // SEL_BLOCK)
    ncp = -(-nc // 128) * 128
    nsp = -(-ns // 128) * 128
    a = f6[:, :, :, 0:nc, :HD]
    b = f6[:, :, :, 1:nc + 1, HD:]
    if b.shape[3] < nc:
        b = jnp.pad(b, ((0, 0), (0, 0), (0, 0), (0, nc - b.shape[3]), (0, 0)))
    a = jnp.pad(a, ((0, 0), (0, 0), (0, 0), (0, ncp - nc), (0, 0))).reshape(2, batch * NSA_KV * ncp, HD)
    b = jnp.pad(b, ((0, 0), (0, 0), (0, 0), (0, ncp - nc), (0, 0))).reshape(2, batch * NSA_KV * ncp, HD)
    R = batch * NSA_KV * ncp
    kcvc = _cmp_stage2(a, b, P["cmp_pe8"], P["cmp_w1flat"], P["cmp_b1"], P["cmp_w2"], P["cmp_b2"],
                       P["k_cmp_norm"], tr=min(R, 2048))
    kcvc = kcvc.reshape(2, NSA_KV, batch, ncp, HD)

    n_keep = min(WINDOW, win_seq.shape[1])
    win_state = win_seq[:, win_seq.shape[1] - n_keep:].reshape(batch, n_keep, 2, NSA_KV, HD)

    seq_q = seq if past is None else tq
    bias_c = _bias_cmp(P["rel_bias"], q_pos0, seq_q, ncp)
    n_delta = min(N_DELTA, (q_pos0 + seq_q) // TK + 1)
    if past is None:
        tqf = FLASH_TQ
        btiles_t = _bias_tiles(P["rel_bias"], tqf, n_delta, transposed=True, d_neg=tqf // TK - 1)
        sel_pages, sel_ptab, sel_npt = kv_pages, ident + M // PAGE, seq // PAGE
        sel_kcol, sel_vcol = 0, NSA_KV
        sel_pps = 4
        sel_tile_of = lambda i, s, pp: s * sel_pps + pp
        win_pages, win_ptab, win_npt = kv_pages, ident + 2 * (M // PAGE), seq // PAGE
        win_kcol, win_vcol = 0, NSA_KV
        win_pps = WINDOW // TK + tqf // TK
        win_tile_of = lambda i, s, pp: (i * tqf) // TK - WINDOW // TK + pp
        win_kbase = 0
    else:
        btiles = _bias_tiles(P["rel_bias"], tq, n_delta)
        assert n_past % SEL_BLOCK == 0 and t_valid <= SEL_BLOCK and ns - 1 > SEL_TOPK
        tail_sel = jnp.pad(new_sel, ((0, 0), (0, SEL_BLOCK - t_valid), (0, 0)))
        bias_blk = _bias_blocks(P["rel_bias"], q_pos0, t_valid, ns)
        nwt = -(-win_seq.shape[1] // TK)
        win_pages = jnp.pad(win_seq, ((0, 0), (0, nwt * TK - win_seq.shape[1]), (0, 0)))
        win_pages = win_pages.reshape(batch * nwt, TK, 2 * NSA_KV * HD)
        win_ptab, win_npt = jnp.arange(batch * nwt, dtype=jnp.int32), nwt
        win_kcol, win_vcol = 0, 4
        win_pps = nwt
        win_tile_of = lambda i, s, pp: pp
        win_active = lambda i, s, pp: True
        win_kbase = w_pos0

    for jj in range(2):
        l = 2 + jj
        q = _nmm(x, P["mix_norm"][l], P["nsa_w_q"], tn=512, tm=1024, layer=jj, out_dtype=BF16, mode="headnorm",
                 aux=P["nsa_q_aux"][jj], scale=HD ** -0.5)
        gates = _nmm(x, P["mix_norm"][l], P["nsa_w_g"][jj], tn=128, mode="sigmoid")
        gate = gates[:, :NSA_HEADS * 3].reshape(batch, seq, NSA_KV, NSA_G, 3).transpose(4, 0, 2, 1, 3)
        if seq_q != seq:
            q = q.reshape(batch, seq, -1)[:, :seq_q].reshape(batch * seq_q, -1)
            gate = gate[:, :, :, :seq_q]
        if past is None:
            gate_t = gates[:, :NSA_HEADS * 3].reshape(batch, seq, NSA_KV, NSA_G, 3).transpose(4, 0, 2, 3, 1)
            o_c, sel = _attn_cmp(q, kcvc, bias_c, gate, batch=batch, seq=seq_q, tq=tq, q_pos0=q_pos0, nc=nc,
                                 ns=ns, nsp=nsp, sel_t=True)
            o_s = _flasht(q, sel_pages, sel_ptab, sel_kcol, sel_vcol, btiles_t, sel, gate_t, 1, o_c,
                          batch=batch, seq=seq_q, tq=tqf, q_pos0=q_pos0, pps=sel_pps,
                          steps_of=lambda i: (i * tqf + tqf - 1) // (TK * sel_pps) + 1,
                          tile_of=sel_tile_of, kbase=0, w_pos0=0, npt=sel_npt, out_dtype=F32)
            o_w = _flasht(q, win_pages, win_ptab, win_kcol, win_vcol, btiles_t, None, gate_t, 2, o_s,
                          batch=batch, seq=seq_q, tq=tqf, q_pos0=q_pos0, pps=win_pps, steps_of=lambda i: 1,
                          tile_of=win_tile_of, kbase=win_kbase, w_pos0=w_pos0, npt=win_npt, out_dtype=BF16)
        else:
            o_c, sel, idx = _attn_cmp(q, kcvc, bias_c, gate, batch=batch, seq=seq_q, tq=tq, q_pos0=q_pos0,
                                      nc=nc, ns=ns, nsp=nsp, n_idx=SEL_TOPK - 1)
            o_s = _sel_gather(q, past["sel_rows"], past["page_table"].reshape(-1),
                              idx[:, :, :t_valid, :SEL_TOPK - 1].reshape(-1), bias_blk, tail_sel, gate, o_c,
                              batch=batch, tq=tq, nq=t_valid, q_pos0=q_pos0, npt=npages, ns=ns)
            o_w = _flash(q, win_pages, win_ptab, win_kcol, win_vcol, btiles, None, None, gate, 2, o_s,
                         batch=batch, seq=seq_q, tq=tq, q_pos0=q_pos0, pps=win_pps,
                         nsteps=1, tile_of=win_tile_of, active=win_active, kbase=win_kbase, w_pos0=w_pos0,
                         npt=win_npt, tail_tile=0, out_dtype=F32)
        if seq_q != seq:
            o_w = jnp.pad(o_w.reshape(batch, seq_q, -1), ((0, 0), (0, seq - seq_q), (0, 0))).reshape(M, -1)
        x = _mm_res(o_w.astype(BF16), P["nsa_w_out"], x, jj, tm=1024, tn=1024)
        x = _mlp(x, P["mlp_norm"][l], P["mlp_w1"], P["mlp_w2"], l)
    return x, cmp_rows, sel_rows, win_state


def _prepare_params(mix_norm, mlp_norm, mlp_w1, mlp_w2, gdn_w_in, gdn_conv_w, gdn_a_log, gdn_dt_bias,
                    gdn_out_norm, gdn_w_out, kv_norm, nsa_w_kv, k_sel_norm, k_win_norm, k_cmp_norm, cmp_pe,
                    cmp_w1, cmp_b1, cmp_w2, cmp_b2, nsa_w_in, nsa_q_norm, nsa_w_out, rel_bias):
    n_lay = gdn_w_in.shape[0]
    main = GDN_CONV_DIM + GDN_VAL_DIM
    zpad = lambda n: jnp.zeros((1, n), F32)
    gate_aux = jnp.stack([
        jnp.concatenate([
            jnp.concatenate([zpad(GDN_V_HEADS), gdn_a_log[l][None].astype(F32), zpad(HD - 2 * GDN_V_HEADS)], 1),
            jnp.concatenate([zpad(GDN_V_HEADS), gdn_dt_bias[l][None].astype(F32), zpad(HD - 2 * GDN_V_HEADS)], 1),
        ], 0)[None] for l in range(n_lay)])
    tile4 = lambda w: jnp.tile(w.astype(F32), NSA_KV)[None, None]
    kv_aux = jnp.concatenate([jnp.ones((2, 1, 512), F32), tile4(k_sel_norm), jnp.ones((1, 1, 512), F32),
                              tile4(k_win_norm), jnp.ones((1, 1, 512), F32)], 0)
    nq = NSA_HEADS * HD
    w1r = cmp_w1.reshape(2, 2, CMP_STRIDE, HD, HD)
    P = dict(
        mix_norm=mix_norm, mlp_norm=mlp_norm,
        mlp_w1=mlp_w1.astype(BF16), mlp_w2=mlp_w2.astype(BF16),
        gdn_w_main=gdn_w_in[:, :, :main].astype(BF16),
        gdn_w_gate=jnp.pad(gdn_w_in[:, :, main:], ((0, 0), (0, 0), (0, HD - 2 * GDN_V_HEADS))).astype(BF16),
        gdn_gate_aux=gate_aux, gdn_conv_w=gdn_conv_w, gdn_out_norm=gdn_out_norm,
        gdn_w_out=gdn_w_out.astype(BF16),
        kv_norm=kv_norm, nsa_w_kv=nsa_w_kv.astype(BF16), kv_aux=kv_aux, k_cmp_norm=k_cmp_norm,
        cmp_w1cat=jnp.concatenate([w1r[:, 0], w1r[:, 1]], axis=-1).astype(BF16),
        cmp_w1flat=cmp_w1.reshape(2, CMP_BLOCK * HD, HD).astype(BF16),
        cmp_pe8=jnp.pad(cmp_pe.reshape(2, 1, CMP_BLOCK * HD), ((0, 0), (0, 7), (0, 0))),
        cmp_b1=cmp_b1, cmp_w2=cmp_w2.astype(BF16), cmp_b2=cmp_b2,
        nsa_w_q=nsa_w_in[:, :, :nq].astype(BF16),
        nsa_w_g=jnp.pad(nsa_w_in[:, :, nq:], ((0, 0), (0, 0), (0, HD - 3 * NSA_HEADS))).astype(BF16),
        nsa_q_aux=jnp.stack([jnp.tile(tile4(nsa_q_norm[jj]), (nq // 512, 1, 1)) for jj in range(2)]),
        nsa_w_out=nsa_w_out.astype(BF16), rel_bias=rel_bias,
    )
    return P


def kernel(x_prompt, x_sample, state_conv, state_ssm, cache_cmp, cache_sel, cache_win, page_table, mix_norm,
           mlp_norm, mlp_w1, mlp_w2, gdn_w_in, gdn_conv_w, gdn_a_log, gdn_dt_bias, gdn_out_norm, gdn_w_out,
           kv_norm, nsa_w_kv, k_sel_norm, k_win_norm, k_cmp_norm, cmp_pe, cmp_w1, cmp_b1, cmp_w2, cmp_b2,
           nsa_w_in, nsa_q_norm, nsa_w_out, rel_bias):
    bp, tp, _ = x_prompt.shape
    bs, ts, _ = x_sample.shape
    n_lay = gdn_w_in.shape[0]
    P = _prepare_params(mix_norm, mlp_norm, mlp_w1, mlp_w2, gdn_w_in, gdn_conv_w, gdn_a_log, gdn_dt_bias,
                        gdn_out_norm, gdn_w_out, kv_norm, nsa_w_kv, k_sel_norm, k_win_norm, k_cmp_norm, cmp_pe,
                        cmp_w1, cmp_b1, cmp_w2, cmp_b2, nsa_w_in, nsa_q_norm, nsa_w_out, rel_bias)

    conv0 =jnp.zeros((n_lay, bp, GDN_CONV - 1, GDN_CONV_DIM), F32)
    ssm0 = jnp.zeros((n_lay, bp, GDN_V_HEADS, HD, HD), F32)
    yp, conv_p, ssm_p, cmp_p, sel_p, win_p = _forward_group(
        x_prompt.reshape(bp * tp, D_MODEL), conv0, ssm0, None, P,
        batch=bp, seq=tp, t_valid=tp, q_pos0=0, tq=128)

    seq_s = GDN_CHUNK
    xs = jnp.pad(x_sample, ((0, 0), (0, seq_s - ts), (0, 0))).reshape(bs * seq_s, D_MODEL)
    n_pool = cache_cmp.shape[0]
    past = dict(cmp_rows=cache_cmp.reshape(n_pool * PAGE * 2 * NSA_KV, HD),
                sel_rows=cache_sel.reshape(n_pool * PAGE * 2 * NSA_KV, HD),
                page_table=page_table.astype(jnp.int32),
                win=cache_win.reshape(bs, cache_win.shape[1], 2 * NSA_KV * HD))
    n_past = page_table.shape[1] * PAGE
    ys, conv_s, ssm_s, cmp_s, sel_s, win_s = _forward_group(
        xs, state_conv, state_ssm, past, P, batch=bs, seq=seq_s, t_valid=ts, q_pos0=n_past, tq=8)
    y_sample = ys.reshape(bs, seq_s, D_MODEL)[:, :ts]
    return (yp.reshape(bp, tp, D_MODEL), y_sample, conv_p, ssm_p, cmp_p, sel_p, win_p,
            conv_s, ssm_s, cmp_s, sel_s, win_s)
```
